```python
import jax, jax.numpy as jnp
from jax import lax
import numpy as np

D_MODEL = 2048
BATCH = 8
SEQ = 2048
DEPTH = 1
DEC_BATCH = 32
DEC_SEQ = 4
PAST_LEN = 16384
PAGE_SIZE = 128

N_ATTN_HEADS = 8
ATTN_HEAD_DIM = 128
ATTN_WIDTH = N_ATTN_HEADS * ATTN_HEAD_DIM
MOBA_BLOCK = 256
MOBA_TOP_K = 3
MOBA_Q_CHUNK = 16
SSM_WIDTH = D_MODEL - ATTN_WIDTH
SSM_HEAD_DIM = 64
N_SSM_HEADS = SSM_WIDTH // SSM_HEAD_DIM
SSM_GROUPS = 2
SSM_HEADS_PER_GROUP = N_SSM_HEADS // SSM_GROUPS
SSM_STATE = 128
CONV_WIDTH = 4
SSD_CHUNK = 128
CONV_CH = SSM_WIDTH + 2 * SSM_GROUPS * SSM_STATE
MIX_WIDTH = ATTN_WIDTH + SSM_WIDTH
SPLITS = (ATTN_WIDTH, 2 * ATTN_WIDTH, 3 * ATTN_WIDTH, 3 * ATTN_WIDTH + SSM_WIDTH,
          3 * ATTN_WIDTH + SSM_WIDTH + CONV_CH)
IN_PROJ = 3 * ATTN_WIDTH + SSM_WIDTH + CONV_CH + N_SSM_HEADS
D_FF = ((8 * D_MODEL + 3 * 256 - 1) // (3 * 256)) * 256
EPS = 1e-6
NEG = -1e30
F32 = jnp.float32

kernel_name = "hymba_moba_ssd_decode_step"


def rmsnorm(x, g):
    xf = x.astype(F32)
    y = xf * lax.rsqrt(jnp.mean(xf * xf, axis=-1, keepdims=True) + EPS)
    return (y * g.astype(F32)).astype(x.dtype)


def alibi_slopes():
    return jnp.asarray(2.0 ** (-8.0 * (np.arange(N_ATTN_HEADS) + 1) / N_ATTN_HEADS), dtype=F32)


def mixer_inputs(x, norm_mix_g, w_in, q_norm_g, k_norm_g):
    b, t, _ = x.shape
    proj = rmsnorm(x, norm_mix_g) @ w_in
    q, k, v, z, xbc, dtr = jnp.split(proj, SPLITS, axis=-1)
    shp = (b, t, N_ATTN_HEADS, ATTN_HEAD_DIM)
    q = rmsnorm(q.reshape(shp), q_norm_g)
    k = rmsnorm(k.reshape(shp), k_norm_g)
    return q, k, v.reshape(shp), z, xbc, dtr


def select_blocks(gate, past):
    gate = jnp.where(past, gate, NEG)
    n = gate.shape[-1]
    if n < MOBA_TOP_K:
        gate = jnp.pad(gate, [(0, 0)] * (gate.ndim - 1) + [(0, MOBA_TOP_K - n)], constant_values=NEG)
    vals, idx = lax.top_k(gate, MOBA_TOP_K)
    return jnp.minimum(idx, max(n - 1, 0)), vals > 0.5 * NEG


def moba_attend(q, q_pos, k_sel, v_sel, sel_pos, sel_ok, k_own, v_own, own_pos):
    slopes = alibi_slopes()[:, None, None]
    scale = ATTN_HEAD_DIM ** -0.5
    qf = q.astype(F32)
    dist_sel = (q_pos[:, None] - sel_pos).astype(F32)
    l_sel = jnp.einsum('bhtd,bhtkd->bhtk', qf, k_sel.astype(F32)) * scale - slopes * dist_sel
    l_sel = jnp.where(sel_ok, l_sel, NEG)
    blk_start = (q_pos // MOBA_BLOCK) * MOBA_BLOCK
    own_ok = (own_pos[None, :] <= q_pos[:, None]) & (own_pos[None, :] >= blk_start[:, None])
    dist_own = (q_pos[:, None] - own_pos[None, :]).astype(F32)
    l_own = jnp.einsum('bhtd,bhmd->bhtm', qf, k_own.astype(F32)) * scale - slopes * dist_own
    l_own = jnp.where(own_ok, l_own, NEG)
    ks = l_sel.shape[-1]
    p = jax.nn.softmax(jnp.concatenate([l_sel, l_own], axis=-1), axis=-1)
    out = (jnp.einsum('bhtk,bhtkd->bhtd', p[..., :ks], v_sel.astype(F32))
           + jnp.einsum('bhtm,bhmd->bhtd', p[..., ks:], v_own.astype(F32)))
    return out.astype(q.dtype)


def moba_prompt(q, k, v):
    b, s, h, dh = q.shape
    nb = -(-s // MOBA_BLOCK)
    pad = ((0, 0), (0, nb * MOBA_BLOCK - s), (0, 0), (0, 0))
    to_blk = lambda t: jnp.pad(t, pad).reshape(b, nb, MOBA_BLOCK, h, dh).transpose(0, 3, 1, 2, 4)
    k_blk, v_blk = to_blk(k), to_blk(v)
    blk_mean = jnp.mean(k_blk.astype(F32), axis=3)
    pos = jnp.arange(s)
    qh = q.transpose(0, 2, 1, 3)
    gate = jnp.einsum('bhsd,bhnd->bhsn', qh.astype(F32), blk_mean)
    past = jnp.arange(nb)[None, :] < (pos // MOBA_BLOCK)[:, None]
    idx, ok = select_blocks(gate, past)
    nc = s // MOBA_Q_CHUNK

    def split(t):
        t = t.reshape(t.shape[:2] + (nc, MOBA_Q_CHUNK) + t.shape[3:])
        return jnp.moveaxis(t, 2, 0)

    bi = jnp.arange(b)[:, None, None, None]
    hi = jnp.arange(h)[None, :, None, None]
    offs = jnp.arange(MOBA_BLOCK)

    def one_chunk(args):
        q_c, idx_c, ok_c, start = args
        q_pos = start + jnp.arange(MOBA_Q_CHUNK)
        sel_shape = (b, h, MOBA_Q_CHUNK, MOBA_TOP_K * MOBA_BLOCK)
        k_sel = k_blk[bi, hi, idx_c].reshape(sel_shape + (dh,))
        v_sel = v_blk[bi, hi, idx_c].reshape(sel_shape + (dh,))
        sel_pos = (idx_c[..., None] * MOBA_BLOCK + offs).reshape(sel_shape)
        sel_ok = jnp.repeat(ok_c, MOBA_BLOCK, axis=-1)
        blk = start // MOBA_BLOCK
        k_own = lax.dynamic_index_in_dim(k_blk, blk, axis=2, keepdims=False)
        v_own = lax.dynamic_index_in_dim(v_blk, blk, axis=2, keepdims=False)
        return moba_attend(q_c, q_pos, k_sel, v_sel, sel_pos, sel_ok, k_own, v_own, blk * MOBA_BLOCK + offs)

    starts = jnp.arange(nc) * MOBA_Q_CHUNK
    out = lax.map(one_chunk, (split(qh), split(idx), split(ok), starts))
    return jnp.moveaxis(out, 0, 2).reshape(b, h, s, dh).transpose(0, 2, 1, 3)


def moba_sample(q, k_new, v_new, cache_k, cache_v, page_table):
    db, t, h, dh = q.shape
    n_pages = page_table.shape[1]
    past_len = n_pages * PAGE_SIZE
    ppb = MOBA_BLOCK // PAGE_SIZE
    nb = past_len // MOBA_BLOCK
    q_pos = past_len + jnp.arange(t)
    qh = q.transpose(0, 2, 1, 3)
    page_sum = jnp.sum(cache_k, axis=1, dtype=F32)
    seq_sum = page_sum[page_table[:, :nb * ppb]]
    blk_mean = seq_sum.reshape(db, nb, ppb, h, dh).sum(axis=2) / MOBA_BLOCK
    gate = jnp.einsum('bhtd,bnhd->bhtn', qh.astype(F32), blk_mean)
    past = jnp.arange(nb)[None, :] < (q_pos // MOBA_BLOCK)[:, None]
    idx, ok = select_blocks(gate, past)
    lp = jnp.clip(idx[..., None] * ppb + jnp.arange(ppb), 0, n_pages - 1)
    bi = jnp.arange(db)[:, None, None, None, None]
    hi = jnp.arange(h)[None, :, None, None, None]
    phys = page_table[bi, lp]
    sel_shape = (db, h, t, MOBA_TOP_K * MOBA_BLOCK)
    k_sel = cache_k[phys, :, hi, :].reshape(sel_shape + (dh,))
    v_sel = cache_v[phys, :, hi, :].reshape(sel_shape + (dh,))
    sel_pos = (lp[..., None] * PAGE_SIZE + jnp.arange(PAGE_SIZE)).reshape(sel_shape)
    sel_ok = jnp.repeat(ok, MOBA_BLOCK, axis=-1)
    last = page_table[:, n_pages - 1]
    k_own = jnp.concatenate([cache_k[last].astype(k_new.dtype), k_new], axis=1).transpose(0, 2, 1, 3)
    v_own = jnp.concatenate([cache_v[last].astype(v_new.dtype), v_new], axis=1).transpose(0, 2, 1, 3)
    own_pos = jnp.concatenate([past_len - PAGE_SIZE + jnp.arange(PAGE_SIZE), q_pos])
    out = moba_attend(qh, q_pos, k_sel, v_sel, sel_pos, sel_ok, k_own, v_own, own_pos)
    return out.transpose(0, 2, 1, 3)


def causal_conv(xbc, conv_state, conv_w, conv_b):
    t = xbc.shape[1]
    xpad = jnp.concatenate([conv_state.astype(xbc.dtype), xbc], axis=1)
    out = conv_b
    for i in range(CONV_WIDTH):
        out = out + xpad[:, i:i + t] * conv_w[i]
    return jax.nn.silu(out), xpad[:, t:]


def ssd_scan(xs, dt, a, bm, cm, h0, chunk):
    b, t = xs.shape[:2]
    nc = t // chunk
    r = lambda u: u.reshape((b, nc, chunk) + u.shape[2:])
    xs_c, dt_c, b_c, c_c = r(xs), r(dt), r(bm), r(cm)
    cum = jnp.cumsum(dt_c * a, axis=2)
    diff = cum[:, :, :, None] - cum[:, :, None, :]
    causal = (jnp.arange(chunk)[:, None] >= jnp.arange(chunk)[None, :])[:, :, None, None]
    decay = jnp.exp(jnp.where(causal, diff, -jnp.inf))
    cb = jnp.einsum('bclgn,bcsgn->bclsg', c_c, b_c)
    w = cb[..., None] * decay * dt_c[:, :, None]
    y_diag = jnp.einsum('bclsge,bcsgep->bclgep', w, xs_c)
    decay_end = jnp.exp(cum[:, :, -1:] - cum) * dt_c
    chunk_states = jnp.einsum('bcsgn,bcsge,bcsgep->bcgepn', b_c, decay_end, xs_c)
    chunk_decay = jnp.exp(cum[:, :, -1])

    def step(hc, inp):
        dec, st = inp
        return dec[..., None, None] * hc + st, hc

    h_last, h_enter = lax.scan(step, h0, (jnp.moveaxis(chunk_decay, 1, 0), jnp.moveaxis(chunk_states, 1, 0)))
    h_enter = jnp.moveaxis(h_enter, 0, 1)
    y_off = jnp.einsum('bclgn,bclge,bcgepn->bclgep', c_c, jnp.exp(cum), h_enter)
    return (y_diag + y_off).reshape(xs.shape), h_last


def ssd_mixer(xbc_act, dtr, dt_bias, a_log, d_skip, h0):
    b, t, _ = xbc_act.shape
    g, e, p, n = SSM_GROUPS, SSM_HEADS_PER_GROUP, SSM_HEAD_DIM, SSM_STATE
    xf = xbc_act.astype(F32)
    xs = xf[..., :SSM_WIDTH].reshape(b, t, g, e, p)
    bm = xf[..., SSM_WIDTH:SSM_WIDTH + g * n].reshape(b, t, g, n)
    cm = xf[..., SSM_WIDTH + g * n:].reshape(b, t, g, n)
    dt = jax.nn.softplus(dtr.astype(F32) + dt_bias.astype(F32)).reshape(b, t, g, e)
    a = -jnp.exp(a_log.astype(F32)).reshape(g, e)
    chunk = SSD_CHUNK if t % SSD_CHUNK == 0 else t
    y, h_last = ssd_scan(xs, dt, a, bm, cm, h0.astype(F32).reshape(b, g, e, p, n), chunk)
    y = y + d_skip.astype(F32).reshape(g, e)[:, :, None] * xs
    return (y.reshape(b, t, SSM_WIDTH).astype(xbc_act.dtype),
            h_last.reshape(b, N_SSM_HEADS, p, n).astype(h0.dtype))


def gated_group_rmsnorm(y, z, g):
    yz = y.astype(F32) * jax.nn.silu(z.astype(F32))
    shp = yz.shape
    yz = yz.reshape(shp[:-1] + (SSM_GROUPS, shp[-1] // SSM_GROUPS))
    yz = yz * lax.rsqrt(jnp.mean(yz * yz, axis=-1, keepdims=True) + EPS)
    return (yz.reshape(shp) * g.astype(F32)).astype(y.dtype)


def mix_out(attn, y_ssm, z, attn_out_g, ssm_out_g, w_out):
    b, t = attn.shape[:2]
    a = rmsnorm(attn.reshape(b, t, ATTN_WIDTH), attn_out_g)
    s = gated_group_rmsnorm(y_ssm, z, ssm_out_g)
    return jnp.concatenate([a, s], axis=-1) @ w_out


def swiglu_ffn(x, g, w_gate, w_up, w_down):
    h = rmsnorm(x, g)
    return (jax.nn.silu(h @ w_gate) * (h @ w_up)) @ w_down


def setup_inputs(seed: int = 0) -> dict:
    key = jax.random.key(seed)
    ks = jax.random.split(key, 24)
    nrm = lambda k, shp, s=1.0: s * jax.random.normal(k, shp, F32)
    n_pages = PAST_LEN // PAGE_SIZE
    n_used = DEC_BATCH * n_pages
    n_phys = n_used + max(1, n_used // 4)
    perm = jax.random.permutation(ks[0], n_phys)
    page_table = perm[:n_used].reshape(DEC_BATCH, n_pages).astype(jnp.int32)
    dt0 = jnp.exp(jax.random.uniform(ks[10], (N_SSM_HEADS,), F32, np.log(1e-3), np.log(1e-1)))
    return {
        "x_prompt": nrm(ks[1], (BATCH, SEQ, D_MODEL)),
        "x_sample": nrm(ks[2], (DEC_BATCH, DEC_SEQ, D_MODEL)),
        "cache_k": nrm(ks[3], (n_phys, PAGE_SIZE, N_ATTN_HEADS, ATTN_HEAD_DIM)),
        "cache_v": nrm(ks[4], (n_phys, PAGE_SIZE, N_ATTN_HEADS, ATTN_HEAD_DIM)),
        "page_table": page_table,
        "state_conv": nrm(ks[5], (DEC_BATCH, CONV_WIDTH - 1, CONV_CH)),
        "state_ssm": nrm(ks[6], (DEC_BATCH, N_SSM_HEADS, SSM_HEAD_DIM, SSM_STATE), 0.1),
        "norm_mix_g": 1.0 + nrm(ks[7], (D_MODEL,), 0.02),
        "w_in": nrm(ks[8], (D_MODEL, IN_PROJ), D_MODEL ** -0.5),
        "q_norm_g": 1.0 + nrm(ks[9], (ATTN_HEAD_DIM,), 0.02),
        "k_norm_g": 1.0 + nrm(ks[11], (ATTN_HEAD_DIM,), 0.02),
        "conv_w": nrm(ks[12], (CONV_WIDTH, CONV_CH), 0.5),
        "conv_b": nrm(ks[13], (CONV_CH,), 0.02),
        "dt_bias": dt0 + jnp.log(-jnp.expm1(-dt0)),
        "a_log": jnp.log(jax.random.uniform(ks[14], (N_SSM_HEADS,), F32, 1.0, 16.0)),
        "d_skip": 1.0 + nrm(ks[15], (N_SSM_HEADS,), 0.1),
        "attn_out_g": 1.0 + nrm(ks[16], (ATTN_WIDTH,), 0.02),
        "ssm_out_g": 1.0 + nrm(ks[17], (SSM_WIDTH,), 0.02),
        "w_out": nrm(ks[18], (MIX_WIDTH, D_MODEL), MIX_WIDTH ** -0.5),
        "norm_ffn_g": 1.0 + nrm(ks[19], (D_MODEL,), 0.02),
        "w_gate": nrm(ks[20], (D_MODEL, D_FF), D_MODEL ** -0.5),
        "w_up": nrm(ks[21], (D_MODEL, D_FF), D_MODEL ** -0.5),
        "w_down": nrm(ks[22], (D_FF, D_MODEL), D_FF ** -0.5),
    }


def reference(x_prompt, x_sample, cache_k, cache_v, page_table, state_conv, state_ssm,
              norm_mix_g, w_in, q_norm_g, k_norm_g, conv_w, conv_b, dt_bias, a_log, d_skip,
              attn_out_g, ssm_out_g, w_out, norm_ffn_g, w_gate, w_up, w_down):
    y_p, y_s = x_prompt, x_sample
    for _layer in range(DEPTH):
        b, s, _ = y_p.shape
        q, k, v, z, xbc, dtr = mixer_inputs(y_p, norm_mix_g, w_in, q_norm_g, k_norm_g)
        attn = moba_prompt(q, k, v)
        xbc_a, conv_p = causal_conv(xbc, jnp.zeros((b, CONV_WIDTH - 1, CONV_CH), xbc.dtype), conv_w, conv_b)
        ys, ssm_p = ssd_mixer(xbc_a, dtr, dt_bias, a_log, d_skip,
                              jnp.zeros((b, N_SSM_HEADS, SSM_HEAD_DIM, SSM_STATE), xbc.dtype))
        y_p = y_p + mix_out(attn, ys, z, attn_out_g, ssm_out_g, w_out)
        y_p = y_p + swiglu_ffn(y_p, norm_ffn_g, w_gate, w_up, w_down)
        page_shape = (b, s // PAGE_SIZE, PAGE_SIZE, N_ATTN_HEADS, ATTN_HEAD_DIM)
        k_p, v_p = k.reshape(page_shape), v.reshape(page_shape)
        qs, k_s, v_s, zs, xbcs, dtrs = mixer_inputs(y_s, norm_mix_g, w_in, q_norm_g, k_norm_g)
        attn_s = moba_sample(qs, k_s, v_s, cache_k, cache_v, page_table)
        xbcs_a, conv_s = causal_conv(xbcs, state_conv, conv_w, conv_b)
        yss, ssm_s = ssd_mixer(xbcs_a, dtrs, dt_bias, a_log, d_skip, state_ssm)
        y_s = y_s + mix_out(attn_s, yss, zs, attn_out_g, ssm_out_g, w_out)
        y_s = y_s + swiglu_ffn(y_s, norm_ffn_g, w_gate, w_up, w_down)
    return (y_p, y_s, k_p, v_p, conv_p, ssm_p, k_s, v_s, conv_s, ssm_s)
```

```python
import functools

import jax
import jax.numpy as jnp
from jax import lax
from jax.experimental import pallas as pl
from jax.experimental.pallas import tpu as pltpu

F32 = jnp.float32
BF16 = jnp.bfloat16
EPS = 1e-6
NEG = -1e30

MOBA_BLOCK = 256
MOBA_TOP_K = 3
SSD_CHUNK = 128
SSM_HEAD_DIM = 64
LANES = 128
SUBLANES = 8
VMEM_LIMIT = 56 * 1024 * 1024

_NT = (((1,), (1,)), ((), ()))
_TN = (((0,), (0,)), ((), ()))


def _params(sem, vmem=VMEM_LIMIT):
    return pltpu.CompilerParams(dimension_semantics=sem, vmem_limit_bytes=vmem)


def _rms(x, g):
    return x * lax.rsqrt(jnp.mean(x * x, axis=-1, keepdims=True) + EPS) * g


def _silu(x):
    return x * (1.0 / (1.0 + jnp.exp(-x)))


def _softplus(x):
    return jnp.maximum(x, 0.0) + jnp.log1p(jnp.exp(-jnp.abs(x)))


def _alibi_slope(h):
    bits = (jnp.full((1, 1), 126, jnp.int32) - h) << 23
    return lax.bitcast_convert_type(bits, F32)


def _inproj_kernel(x_ref, g_ref, w_ref, wdt_ref, qg_ref, kg_ref,
                   q_ref, k_ref, v_ref, z_ref, xbc_ref, dt_ref, xn_ref, *, tn, seg):
    j = pl.program_id(1)

    @pl.when(j == 0)
    def _():
        xn_ref[...] = _rms(x_ref[...], g_ref[...]).astype(BF16)
        dt_ref[...] = jnp.dot(xn_ref[...], wdt_ref[...], preferred_element_type=F32)

    acc = jnp.dot(xn_ref[...], w_ref[...], preferred_element_type=F32)

    def headnorm(out_ref, g):
        for hh in range(tn // LANES):
            sl = slice(hh * LANES, (hh + 1) * LANES)
            out_ref[:, sl] = _rms(acc[:, sl], g)

    @pl.when(j < seg[0])
    def _():
        headnorm(q_ref, qg_ref[...])

    @pl.when((j >= seg[0]) & (j < seg[1]))
    def _():
        headnorm(k_ref, kg_ref[...])

    @pl.when((j >= seg[1]) & (j < seg[2]))
    def _():
        v_ref[...] = acc

    @pl.when((j >= seg[2]) & (j < seg[3]))
    def _():
        z_ref[...] = acc

    @pl.when(j >= seg[3])
    def _():
        xbc_ref[...] = acc


def _inproj(x2, norm_g, w_main, w_dt, q_g, k_g, *, attn_w, ssm_w, conv_ch, tm, tn=512):
    m, d = x2.shape
    n_main = w_main.shape[1]
    seg = (attn_w // tn, 2 * attn_w // tn, 3 * attn_w // tn, (3 * attn_w + ssm_w) // tn)
    nj = n_main // tn

    def col(lo, hi):
        return lambda i, j: (i, jnp.clip(j - lo, 0, hi - lo - 1))

    out_shape = (
        jax.ShapeDtypeStruct((m, attn_w), F32), jax.ShapeDtypeStruct((m, attn_w), F32),
        jax.ShapeDtypeStruct((m, attn_w), F32), jax.ShapeDtypeStruct((m, ssm_w), F32),
        jax.ShapeDtypeStruct((m, conv_ch), F32), jax.ShapeDtypeStruct((m, LANES), F32))
    return pl.pallas_call(
        functools.partial(_inproj_kernel, tn=tn, seg=seg),
        grid=(m // tm, nj),
        in_specs=[
            pl.BlockSpec((tm, d), lambda i, j: (i, 0)),
            pl.BlockSpec((1, d), lambda i, j: (0, 0)),
            pl.BlockSpec((d, tn), lambda i, j: (0, j)),
            pl.BlockSpec((d, LANES), lambda i, j: (0, 0)),
            pl.BlockSpec((1, LANES), lambda i, j: (0, 0)),
            pl.BlockSpec((1, LANES), lambda i, j: (0, 0)),
        ],
        out_specs=(
            pl.BlockSpec((tm, tn), col(0, seg[0])),
            pl.BlockSpec((tm, tn), col(seg[0], seg[1])),
            pl.BlockSpec((tm, tn), col(seg[1], seg[2])),
            pl.BlockSpec((tm, tn), col(seg[2], seg[3])),
            pl.BlockSpec((tm, tn), col(seg[3], nj)),
            pl.BlockSpec((tm, LANES), lambda i, j: (i, 0)),
        ),
        out_shape=out_shape,
        scratch_shapes=[pltpu.VMEM((tm, d), BF16)],
        compiler_params=_params(("parallel", "arbitrary")),
        name="inproj",
    )(x2, norm_g, w_main, w_dt, q_g, k_g)


def _block_select(gate, n_past):
    lane = lax.broadcasted_iota(jnp.int32, gate.shape, 1)
    valid = lane < n_past
    if n_past <= MOBA_TOP_K:
        return valid & (gate > 0.5 * NEG)
    gm = jnp.where(valid, gate, -jnp.inf)
    rank = jnp.zeros(gate.shape, jnp.int32)
    for d in range(1, n_past):
        lower = pltpu.roll(gm, d, axis=1)
        upper = pltpu.roll(gm, LANES - d, axis=1)
        rank = rank + (lower >= gm).astype(jnp.int32) + (upper > gm).astype(jnp.int32)
    return valid & (rank < MOBA_TOP_K) & (gate > 0.5 * NEG)


def _moba_prompt_kernel(q_ref, k_ref, v_ref, o_ref, *, nb, scale):
    blk = MOBA_BLOCK
    h = pl.program_id(1)
    slope = _alibi_slope(h)
    k = k_ref[...]
    kb = k.astype(BF16)
    vb = v_ref[...].astype(BF16)
    means = [jnp.mean(k[j * blk:(j + 1) * blk], axis=0, keepdims=True) for j in range(nb)]
    means = jnp.concatenate(means + [jnp.zeros((LANES - nb, LANES), F32)], axis=0)
    for i in range(nb):
        qi = q_ref[i * blk:(i + 1) * blk, :]
        nk = (i + 1) * blk
        s = lax.dot_general(qi.astype(BF16), kb[:nk], _NT, preferred_element_type=F32)
        col = lax.broadcasted_iota(jnp.int32, (blk, nk), 1)
        logits = s * scale + slope * col.astype(F32)
        row = lax.broadcasted_iota(jnp.int32, (blk, blk), 0)
        colb = lax.broadcasted_iota(jnp.int32, (blk, blk), 1)
        pieces = []
        if i > 0:
            gate = lax.dot_general(qi, means, _NT, precision=lax.Precision.HIGHEST,
                                   preferred_element_type=F32)
            sel = _block_select(gate, i).astype(F32)
            pieces = [jnp.broadcast_to(sel[:, j:j + 1], (blk, blk)) for j in range(i)]
        pieces.append((colb <= row).astype(F32))
        allowed = jnp.concatenate(pieces, axis=1) if len(pieces) > 1 else pieces[0]
        logits = jnp.where(allowed > 0.5, logits, NEG)
        m = jnp.max(logits, axis=-1, keepdims=True)
        p = jnp.exp(logits - m)
        denom = jnp.sum(p, axis=-1, keepdims=True)
        out = jnp.dot(p.astype(BF16), vb[:nk], preferred_element_type=F32)
        o_ref[i * blk:(i + 1) * blk, :] = out / denom


def _moba_prompt(q, k, v, *, n_heads):
    b, s, w = q.shape
    dh = w // n_heads
    spec = pl.BlockSpec((None, s, dh), lambda bi, hi: (bi, 0, hi))
    return pl.pallas_call(
        functools.partial(_moba_prompt_kernel, nb=s // MOBA_BLOCK, scale=dh ** -0.5),
        grid=(b, n_heads),
        in_specs=[spec, spec, spec],
        out_specs=spec,
        out_shape=jax.ShapeDtypeStruct((b, s, w), F32),
        compiler_params=_params(("parallel", "parallel")),
        name="moba_prompt",
    )(q, k, v)


def _ssd_prompt_kernel(xbc_ref, dtr_ref, cw_ref, cb_ref, dtb_ref, alog_ref, dskip_ref, expand_ref,
                       y_ref, convp_ref, h_ref, xp_ref, *, cw, ssm_w, groups, n_state):
    c = pl.program_id(1)
    nc = pl.num_programs(1)
    lc = SSD_CHUNK
    p_dim = SSM_HEAD_DIM
    heads = ssm_w // p_dim
    hpg = heads // groups
    gw = hpg * p_dim
    tail = cw - 1

    @pl.when(c == 0)
    def _():
        xp_ref[0:SUBLANES, :] = jnp.zeros((SUBLANES, xp_ref.shape[1]), F32)
        h_ref[...] = jnp.zeros(h_ref.shape, F32)

    xc = xbc_ref[...]
    xp_ref[SUBLANES:SUBLANES + lc, :] = xc
    conv = cb_ref[...]
    for i in range(cw):
        conv = conv + xp_ref[SUBLANES - tail + i:SUBLANES - tail + i + lc, :] * cw_ref[i:i + 1, :]
    act = _silu(conv)
    xp_ref[SUBLANES - tail:SUBLANES, :] = xc[lc - tail:lc, :]

    @pl.when(c == nc - 1)
    def _():
        convp_ref[...] = xc[lc - tail:lc, :]

    dt = _softplus(dtr_ref[...] + dtb_ref[...])
    a = -jnp.exp(alog_ref[...])
    row = lax.broadcasted_iota(jnp.int32, (lc, lc), 0)
    colm = lax.broadcasted_iota(jnp.int32, (lc, lc), 1)
    causal = row >= colm
    tri = causal.astype(F32)
    cum = jnp.dot(tri, dt * a, precision=lax.Precision.HIGHEST, preferred_element_type=F32)
    cum_t = cum.T
    dt_t = dt.T
    cum_last = cum[lc - 1:lc, :]
    dec_end = jnp.exp(cum_last - cum) * dt
    ecum = jnp.exp(cum)
    expand = expand_ref[...]
    ecum_x = jnp.dot(ecum, expand, precision=lax.Precision.HIGHEST, preferred_element_type=F32)
    dec_end_x = jnp.dot(dec_end, expand, precision=lax.Precision.HIGHEST,
                        preferred_element_type=F32)

    xs = act[:, :ssm_w]
    xs_b = xs.astype(BF16)
    y = dskip_ref[...] * xs
    lane = lax.broadcasted_iota(jnp.int32, (lc, 2 * p_dim), 1)
    for g in range(groups):
        bg = act[:, ssm_w + g * n_state:ssm_w + (g + 1) * n_state].astype(BF16)
        cg = act[:, ssm_w + (groups + g) * n_state:ssm_w + (groups + g + 1) * n_state].astype(BF16)
        cb = lax.dot_general(cg, bg, _NT, preferred_element_type=F32)
        hg = h_ref[g * hpg:(g + 1) * hpg].reshape(gw, n_state)
        y_off = lax.dot_general(cg, hg.astype(BF16), _NT, preferred_element_type=F32)
        y_off = y_off * ecum_x[:, g * gw:(g + 1) * gw]
        y_diag = []
        for pr in range(hpg // 2):
            ws = []
            for e in (g * hpg + 2 * pr, g * hpg + 2 * pr + 1):
                diff = cum[:, e:e + 1] - cum_t[e:e + 1, :]
                decay = jnp.where(causal, jnp.exp(diff), 0.0)
                ws.append((cb * decay * dt_t[e:e + 1, :]).astype(BF16))
            lo = (g * hpg + 2 * pr) * p_dim
            xpair = xs_b[:, lo:lo + 2 * p_dim]
            zero = jnp.zeros_like(xpair)
            rhs = jnp.concatenate([jnp.where(lane < p_dim, xpair, zero),
                                   jnp.where(lane >= p_dim, xpair, zero)], axis=0)
            y_diag.append(jnp.dot(jnp.concatenate(ws, axis=1), rhs, preferred_element_type=F32))
        y_g = jnp.concatenate(y_diag, axis=1) + y_off
        xw = (xs[:, g * gw:(g + 1) * gw] * dec_end_x[:, g * gw:(g + 1) * gw]).astype(BF16)
        st = lax.dot_general(xw, bg, _TN, preferred_element_type=F32)
        for ee in range(hpg):
            e = g * hpg + ee
            cd = jnp.exp(cum_t[e:e + 1, lc - 1:lc])
            h_ref[e] = h_ref[e] * cd + st[ee * p_dim:(ee + 1) * p_dim, :]
        y_ref[:, g * gw:(g + 1) * gw] = y[:, g * gw:(g + 1) * gw] + y_g


def _ssd_prompt(xbc, dtr, conv_w, conv_b, dtb, alog, dskip_x, expand, *, ssm_w, groups, n_state):
    b, s, cc = xbc.shape
    cw = conv_w.shape[0]
    heads = ssm_w // SSM_HEAD_DIM
    lc = SSD_CHUNK
    const = lambda shape: pl.BlockSpec(shape, lambda bi, ci: (0,) * len(shape))
    return pl.pallas_call(
        functools.partial(_ssd_prompt_kernel, cw=cw, ssm_w=ssm_w, groups=groups, n_state=n_state),
        grid=(b, s // lc),
        in_specs=[
            pl.BlockSpec((None, lc, cc), lambda bi, ci: (bi, ci, 0)),
            pl.BlockSpec((None, lc, LANES), lambda bi, ci: (bi, ci, 0)),
            const((cw, cc)), const((1, cc)), const((1, LANES)), const((1, LANES)),
            const((1, ssm_w)), const((LANES, ssm_w)),
        ],
        out_specs=(
            pl.BlockSpec((None, lc, ssm_w), lambda bi, ci: (bi, ci, 0)),
            pl.BlockSpec((None, cw - 1, cc), lambda bi, ci: (bi, 0, 0)),
            pl.BlockSpec((None, heads, SSM_HEAD_DIM, n_state), lambda bi, ci: (bi, 0, 0, 0)),
        ),
        out_shape=(
            jax.ShapeDtypeStruct((b, s, ssm_w), F32),
            jax.ShapeDtypeStruct((b, cw - 1, cc), F32),
            jax.ShapeDtypeStruct((b, heads, SSM_HEAD_DIM, n_state), F32),
        ),
        scratch_shapes=[pltpu.VMEM((SUBLANES + lc, cc), F32)],
        compiler_params=_params(("parallel", "arbitrary")),
        name="ssd_prompt",
    )(xbc, dtr, conv_w, conv_b, dtb, alog, dskip_x, expand)


def _outproj_kernel(attn_ref, y_ref, z_ref, x_ref, ag_ref, sg_ref, w_ref, o_ref, mix_ref,
                    *, attn_w, groups):
    j = pl.program_id(1)

    @pl.when(j == 0)
    def _():
        mix_ref[:, :attn_w] = _rms(attn_ref[...], ag_ref[...]).astype(BF16)
        yz = y_ref[...] * _silu(z_ref[...])
        gw = yz.shape[1] // groups
        for g in range(groups):
            sl = slice(g * gw, (g + 1) * gw)
            mix_ref[:, attn_w + g * gw:attn_w + (g + 1) * gw] = _rms(yz[:, sl], sg_ref[:, sl]).astype(BF16)

    o_ref[...] = x_ref[...] + jnp.dot(mix_ref[...], w_ref[...], preferred_element_type=F32)


def _outproj(attn, y, z, x2, attn_g, ssm_g, w_out, *, groups, tm, tn=512):
    m, attn_w = attn.shape
    ssm_w = y.shape[1]
    d = x2.shape[1]
    return pl.pallas_call(
        functools.partial(_outproj_kernel, attn_w=attn_w, groups=groups),
        grid=(m // tm, d // tn),
        in_specs=[
            pl.BlockSpec((tm, attn_w), lambda i, j: (i, 0)),
            pl.BlockSpec((tm, ssm_w), lambda i, j: (i, 0)),
            pl.BlockSpec((tm, ssm_w), lambda i, j: (i, 0)),
            pl.BlockSpec((tm, tn), lambda i, j: (i, j)),
            pl.BlockSpec((1, attn_w), lambda i, j: (0, 0)),
            pl.BlockSpec((1, ssm_w), lambda i, j: (0, 0)),
            pl.BlockSpec((attn_w + ssm_w, tn), lambda i, j: (0, j)),
        ],
        out_specs=pl.BlockSpec((tm, tn), lambda i, j: (i, j)),
        out_shape=jax.ShapeDtypeStruct((m, d), F32),
        scratch_shapes=[pltpu.VMEM((tm, attn_w + ssm_w), BF16)],
        compiler_params=_params(("parallel", "arbitrary")),
        name="outproj",
    )(attn, y, z, x2, attn_g, ssm_g, w_out)


def _ffn_kernel(h_ref, g_ref, wg_ref, wu_ref, wd_ref, o_ref, hn_ref, acc_ref):
    f = pl.program_id(1)

    @pl.when(f == 0)
    def _():
        hn_ref[...] = _rms(h_ref[...], g_ref[...]).astype(BF16)

    hn = hn_ref[...]
    gate = jnp.dot(hn, wg_ref[...], preferred_element_type=F32)
    up = jnp.dot(hn, wu_ref[...], preferred_element_type=F32)
    part = jnp.dot((_silu(gate) * up).astype(BF16), wd_ref[...], preferred_element_type=F32)

    @pl.when(f == 0)
    def _():
        acc_ref[...] = part

    @pl.when(f > 0)
    def _():
        acc_ref[...] += part

    @pl.when(f == pl.num_programs(1) - 1)
    def _():
        o_ref[...] = h_ref[...] + acc_ref[...]


def _ffn(h, g, w_gate, w_up, w_down, *, tm, tf=512):
    m, d = h.shape
    dff = w_gate.shape[1]
    return pl.pallas_call(
        _ffn_kernel,
        grid=(m // tm, dff // tf),
        in_specs=[
            pl.BlockSpec((tm, d), lambda i, f: (i, 0)),
            pl.BlockSpec((1, d), lambda i, f: (0, 0)),
            pl.BlockSpec((d, tf), lambda i, f: (0, f)),
            pl.BlockSpec((d, tf), lambda i, f: (0, f)),
            pl.BlockSpec((tf, d), lambda i, f: (f, 0)),
        ],
        out_specs=pl.BlockSpec((tm, d), lambda i, f: (i, 0)),
        out_shape=jax.ShapeDtypeStruct((m, d), F32),
        scratch_shapes=[pltpu.VMEM((tm, d), BF16), pltpu.VMEM((tm, d), F32)],
        compiler_params=_params(("parallel", "arbitrary")),
        name="ffn",
    )(h, g, w_gate, w_up, w_down)


def _page_sum_kernel(ck_ref, o_ref):
    o_ref[...] = jnp.sum(ck_ref[...], axis=1)


def _page_sums(cache_k, *, pages_per_step=16):
    n_phys, page, h, dh = cache_k.shape
    return pl.pallas_call(
        _page_sum_kernel,
        grid=(n_phys // pages_per_step,),
        in_specs=[pl.BlockSpec((pages_per_step, page, h, dh), lambda i: (i, 0, 0, 0))],
        out_specs=pl.BlockSpec((pages_per_step, h, dh), lambda i: (i, 0, 0)),
        out_shape=jax.ShapeDtypeStruct((n_phys, h, dh), F32),
        compiler_params=_params(("parallel",)),
        name="page_sums",
    )(cache_k)


def _sample_select_kernel(pt_ref, ps_ref, q_ref, sel_ref, blk_ref, *, nb, ppb, n_heads, t_len):
    b = pl.program_id(0)
    for n in range(nb):
        acc = ps_ref[pt_ref[b, n * ppb]]
        for pp in range(1, ppb):
            acc = acc + ps_ref[pt_ref[b, n * ppb + pp]]
        blk_ref[n] = acc / MOBA_BLOCK
    lane = lax.broadcasted_iota(jnp.int32, (SUBLANES, LANES), 1)
    for h in range(n_heads):
        qh = q_ref[:, h * LANES:(h + 1) * LANES]
        q8 = jnp.concatenate([qh, jnp.zeros((SUBLANES - t_len, LANES), F32)], axis=0)
        bm = blk_ref[:, h, :]
        gate = lax.dot_general(q8, bm, _NT, precision=lax.Precision.HIGHEST,
                               preferred_element_type=F32)
        lane_n = lax.broadcasted_iota(jnp.int32, gate.shape, 1).astype(F32)
        out = jnp.zeros((SUBLANES, LANES), jnp.int32)
        for kk in range(MOBA_TOP_K):
            mx = jnp.max(gate, axis=-1, keepdims=True)
            idx_f = jnp.min(jnp.where(gate == mx, lane_n, float(nb)), axis=-1, keepdims=True)
            idx = idx_f.astype(jnp.int32)
            ok = (mx > 0.5 * NEG).astype(jnp.int32)
            out = jnp.where(lane == kk, idx, out)
            out = jnp.where(lane == MOBA_TOP_K + kk, ok, out)
            gate = jnp.where(lane_n == idx_f, -jnp.inf, gate)
        sel_ref[h] = out


def _sample_select(page_table, page_sum, q_s, *, n_heads, page):
    db, t_len, w = q_s.shape
    n_pages = page_table.shape[1]
    n_phys, h, dh = page_sum.shape
    ppb = MOBA_BLOCK // page
    nb = n_pages // ppb
    grid_spec = pltpu.PrefetchScalarGridSpec(
        num_scalar_prefetch=1,
        grid=(db,),
        in_specs=[
            pl.BlockSpec((n_phys, h, dh), lambda bi, pt: (0, 0, 0), pipeline_mode=pl.Buffered(1)),
            pl.BlockSpec((None, t_len, w), lambda bi, pt: (bi, 0, 0)),
        ],
        out_specs=pl.BlockSpec((None, n_heads, SUBLANES, LANES), lambda bi, pt: (bi, 0, 0, 0)),
        scratch_shapes=[pltpu.VMEM((nb, h, dh), F32)],
    )
    return pl.pallas_call(
        functools.partial(_sample_select_kernel, nb=nb, ppb=ppb, n_heads=n_heads, t_len=t_len),
        grid_spec=grid_spec,
        out_shape=jax.ShapeDtypeStruct((db, n_heads, SUBLANES, LANES), jnp.int32),
        compiler_params=_params(("arbitrary",)),
        name="sample_select",
    )(page_table, page_sum, q_s)


def _sample_attend_kernel(pt_ref, idx_ref, ok_ref, q_ref, kn_ref, vn_ref, ck_hbm, cv_hbm, o_ref,
                          kbuf, vbuf, sem, *, t_len, page, ppb, past_len, scale):
    b = pl.program_id(0)
    h = pl.program_id(1)
    n_sel = t_len * MOBA_TOP_K * ppb
    n_pages = past_len // page

    def copies(slot, phys):
        return (pltpu.make_async_copy(ck_hbm.at[phys, :, h, :], kbuf.at[slot], sem.at[0]),
                pltpu.make_async_copy(cv_hbm.at[phys, :, h, :], vbuf.at[slot], sem.at[1]))

    def slot_pages():
        out = []
        for t in range(t_len):
            for kk in range(MOBA_TOP_K):
                n = idx_ref[b, (h * t_len + t) * MOBA_TOP_K + kk]
                for pp in range(ppb):
                    lp = jnp.clip(n * ppb + pp, 0, n_pages - 1)
                    out.append(((t * MOBA_TOP_K + kk) * ppb + pp, pt_ref[b, lp], n * ppb + pp))
        out.append((n_sel, pt_ref[b, n_pages - 1], None))
        return out

    pages = slot_pages()
    for slot, phys, _ in pages:
        for cp in copies(slot, phys):
            cp.start()

    pad = jnp.zeros((page - t_len, LANES), F32)
    kbuf[n_sel + 1] = jnp.concatenate([kn_ref[...], pad], axis=0)
    vbuf[n_sel + 1] = jnp.concatenate([vn_ref[...], pad], axis=0)

    lane = lax.broadcasted_iota(jnp.int32, (1, page), 1)
    pos, okv = [], []
    for slot, _, lp in pages[:-1]:
        n_ok = ok_ref[b, (h * t_len + slot // (MOBA_TOP_K * ppb)) * MOBA_TOP_K + (slot // ppb) % MOBA_TOP_K]
        pos.append(lane + jnp.clip(lp, 0, n_pages - 1) * page)
        okv.append(jnp.broadcast_to(n_ok, (1, page)))
    pos.append(lane + (past_len - page))
    pos.append(lane + past_len)
    pos = jnp.concatenate(pos, axis=1)
    okv = jnp.concatenate(okv, axis=1)

    for slot, phys, _ in pages:
        for cp in copies(slot, phys):
            cp.wait()

    n_keys = (n_sel + 2) * page
    kall = kbuf[...].reshape(n_keys, LANES).astype(BF16)
    vall = vbuf[...].reshape(n_keys, LANES).astype(BF16)
    q8 = jnp.concatenate([q_ref[...], jnp.zeros((SUBLANES - t_len, LANES), F32)], axis=0)
    s = lax.dot_general(q8.astype(BF16), kall, _NT, preferred_element_type=F32)

    slope = _alibi_slope(h)
    trow = lax.broadcasted_iota(jnp.int32, (SUBLANES, n_keys), 0)
    kcol = lax.broadcasted_iota(jnp.int32, (SUBLANES, n_keys), 1)
    q_pos = past_len + trow
    dist = (q_pos - pos).astype(F32)
    logits = s * scale - slope * dist
    per_t = MOBA_TOP_K * ppb * page
    sel_cols = n_sel * page
    sel_ok = (kcol // per_t == trow) & (jnp.concatenate(
        [okv, jnp.zeros((1, n_keys - sel_cols), jnp.int32)], axis=1) > 0)
    blk_start = (q_pos // MOBA_BLOCK) * MOBA_BLOCK
    own_ok = (kcol >= sel_cols) & (pos <= q_pos) & (pos >= blk_start) & (kcol < sel_cols + page + t_len)
    logits = jnp.where(sel_ok | own_ok, logits, NEG)
    m = jnp.max(logits, axis=-1, keepdims=True)
    p = jnp.exp(logits - m)
    denom = jnp.sum(p, axis=-1, keepdims=True)
    out = jnp.dot(p.astype(BF16), vall, preferred_element_type=F32) / denom
    o_ref[...] = out[:t_len, :]


def _sample_attend(page_table, idx, ok, q_s, k_s, v_s, cache_k, cache_v, *, n_heads):
    db, t_len, w = q_s.shape
    page = cache_k.shape[1]
    n_pages = page_table.shape[1]
    ppb = MOBA_BLOCK // page
    n_slots = t_len * MOBA_TOP_K * ppb + 2
    tok = pl.BlockSpec((None, t_len, LANES), lambda bi, hi, *_: (bi, 0, hi))
    grid_spec = pltpu.PrefetchScalarGridSpec(
        num_scalar_prefetch=3,
        grid=(db, n_heads),
        in_specs=[tok, tok, tok,
                  pl.BlockSpec(memory_space=pl.ANY), pl.BlockSpec(memory_space=pl.ANY)],
        out_specs=tok,
        scratch_shapes=[pltpu.VMEM((n_slots, page, LANES), F32),
                        pltpu.VMEM((n_slots, page, LANES), F32),
                        pltpu.SemaphoreType.DMA((2,))],
    )
    return pl.pallas_call(
        functools.partial(_sample_attend_kernel, t_len=t_len, page=page, ppb=ppb,
                          past_len=n_pages * page, scale=LANES ** -0.5),
        grid_spec=grid_spec,
        out_shape=jax.ShapeDtypeStruct((db, t_len, w), F32),
        compiler_params=_params(("arbitrary", "arbitrary")),
        name="sample_attend",
    )(page_table, idx, ok, q_s, k_s, v_s, cache_k, cache_v)


def _ssd_sample_kernel(xbc_ref, sc_ref, dtr_ref, h0_ref, cw_ref, cb_ref, dtb_ref, alog_ref, dskip_ref,
                       y_ref, convs_ref, h_ref, xp_ref, xt_ref, yt_ref,
                       *, cw, t_len, ssm_w, groups, n_state):
    p_dim = SSM_HEAD_DIM
    heads = ssm_w // p_dim
    hpg = heads // groups
    tail = cw - 1
    xp_ref[0:tail, :] = sc_ref[...]
    xp_ref[tail:tail + t_len, :] = xbc_ref[...]
    conv = cb_ref[...]
    for i in range(cw):
        conv = conv + xp_ref[i:i + t_len, :] * cw_ref[i:i + 1, :]
    act = _silu(conv)
    convs_ref[...] = xp_ref[t_len:t_len + tail, :]

    dt = _softplus(dtr_ref[...] + dtb_ref[...])
    da = jnp.exp(dt * (-jnp.exp(alog_ref[...])))

    xs = act[:, :ssm_w]
    xpad = jnp.concatenate([xs, jnp.zeros((LANES - t_len, ssm_w), F32)], axis=0)
    for blk in range(ssm_w // LANES):
        xt_ref[blk * LANES:(blk + 1) * LANES, :] = xpad[:, blk * LANES:(blk + 1) * LANES].T
    yt_ref[...] = jnp.zeros(yt_ref.shape, F32)

    for e in range(heads):
        g = e // hpg
        hs = h0_ref[e]
        for t in range(t_len):
            b_row = act[t:t + 1, ssm_w + g * n_state:ssm_w + (g + 1) * n_state]
            c_row = act[t:t + 1, ssm_w + (groups + g) * n_state:ssm_w + (groups + g + 1) * n_state]
            x_col = xt_ref[e * p_dim:(e + 1) * p_dim, t:t + 1]
            hs = hs * da[t:t + 1, e:e + 1] + (x_col * dt[t:t + 1, e:e + 1]) * b_row
            yt_ref[e * p_dim:(e + 1) * p_dim, t:t + 1] = jnp.sum(hs * c_row, axis=-1, keepdims=True)
        h_ref[e] = hs

    rows = []
    for blk in range(ssm_w // LANES):
        rows.append(yt_ref[blk * LANES:(blk + 1) * LANES, :].T[:t_len, :])
    y_ref[...] = jnp.concatenate(rows, axis=1) + dskip_ref[...] * xs


def _ssd_sample(xbc, state_conv, dtr, state_ssm, conv_w, conv_b, dtb, alog, dskip_x,
                *, ssm_w, groups, n_state):
    db, t_len, cc = xbc.shape
    cw = conv_w.shape[0]
    heads = ssm_w // SSM_HEAD_DIM
    const = lambda shape: pl.BlockSpec(shape, lambda bi: (0,) * len(shape))
    return pl.pallas_call(
        functools.partial(_ssd_sample_kernel, cw=cw, t_len=t_len, ssm_w=ssm_w, groups=groups,
                          n_state=n_state),
        grid=(db,),
        in_specs=[
            pl.BlockSpec((None, t_len, cc), lambda bi: (bi, 0, 0)),
            pl.BlockSpec((None, cw - 1, cc), lambda bi: (bi, 0, 0)),
            pl.BlockSpec((None, t_len, LANES), lambda bi: (bi, 0, 0)),
            pl.BlockSpec((None, heads, SSM_HEAD_DIM, n_state), lambda bi: (bi, 0, 0, 0)),
            const((cw, cc)), const((1, cc)), const((1, LANES)), const((1, LANES)), const((1, ssm_w)),
        ],
        out_specs=(
            pl.BlockSpec((None, t_len, ssm_w), lambda bi: (bi, 0, 0)),
            pl.BlockSpec((None, cw - 1, cc), lambda bi: (bi, 0, 0)),
            pl.BlockSpec((None, heads, SSM_HEAD_DIM, n_state), lambda bi: (bi, 0, 0, 0)),
        ),
        out_shape=(
            jax.ShapeDtypeStruct((db, t_len, ssm_w), F32),
            jax.ShapeDtypeStruct((db, cw - 1, cc), F32),
            jax.ShapeDtypeStruct((db, heads, SSM_HEAD_DIM, n_state), F32),
        ),
        scratch_shapes=[pltpu.VMEM((2 * SUBLANES, cc), F32),
                        pltpu.VMEM((ssm_w, LANES), F32),
                        pltpu.VMEM((ssm_w, LANES), F32)],
        compiler_params=_params(("parallel",)),
        name="ssd_sample",
    )(xbc, state_conv, dtr, state_ssm, conv_w, conv_b, dtb, alog, dskip_x)


def _row_tile(m, target):
    tm = min(m, target)
    assert m % tm == 0, (m, tm)
    return tm


def kernel(x_prompt, x_sample, cache_k, cache_v, page_table, state_conv, state_ssm, norm_mix_g, w_in, q_norm_g, k_norm_g, conv_w, conv_b, dt_bias, a_log, d_skip, attn_out_g, ssm_out_g, w_out, norm_ffn_g, w_gate, w_up, w_down):
    b, s, d = x_prompt.shape
    db, t_len, _ = x_sample.shape
    n_phys, page, n_heads, dh = cache_k.shape
    attn_w = n_heads * dh
    heads, p_dim, n_state = state_ssm.shape[1:]
    assert p_dim == SSM_HEAD_DIM and dh == LANES and heads <= LANES
    ssm_w = heads * p_dim
    conv_ch = state_conv.shape[2]
    groups = (conv_ch - ssm_w) // (2 * n_state)
    n_main = 3 * attn_w + ssm_w + conv_ch

    w_main = w_in[:, :n_main].astype(BF16)
    w_dt = jnp.pad(w_in[:, n_main:], ((0, 0), (0, LANES - heads))).astype(BF16)
    w_out_b, w_gate_b, w_up_b, w_down_b = (w.astype(BF16) for w in (w_out, w_gate, w_up, w_down))
    row = lambda v: v.reshape(1, -1).astype(F32)
    padl = lambda v: jnp.pad(v.astype(F32), (0, LANES - v.shape[0])).reshape(1, LANES)
    dskip_x = jnp.repeat(d_skip.astype(F32), p_dim).reshape(1, ssm_w)
    expand = (jnp.arange(LANES)[:, None] == (jnp.arange(ssm_w)[None, :] // p_dim)).astype(F32)
    in_args = (row(norm_mix_g), w_main, w_dt, row(q_norm_g), row(k_norm_g))
    in_kw = dict(attn_w=attn_w, ssm_w=ssm_w, conv_ch=conv_ch)
    ssd_kw = dict(ssm_w=ssm_w, groups=groups, n_state=n_state)
    ssd_par = (conv_w, row(conv_b), padl(dt_bias), padl(a_log), dskip_x)

    xp2 = x_prompt.reshape(b * s, d)
    q, k, v, z, xbc, dtr = _inproj(xp2, *in_args, tm=_row_tile(b * s, 1024), **in_kw)
    attn = _moba_prompt(q.reshape(b, s, attn_w), k.reshape(b, s, attn_w), v.reshape(b, s, attn_w),
                        n_heads=n_heads)
    ys, conv_p, ssm_p = _ssd_prompt(xbc.reshape(b, s, conv_ch), dtr.reshape(b, s, LANES), *ssd_par,
                                    expand, **ssd_kw)
    tm = _row_tile(b * s, 512)
    h_p = _outproj(attn.reshape(b * s, attn_w), ys.reshape(b * s, ssm_w), z, xp2,
                   row(attn_out_g), row(ssm_out_g), w_out_b, groups=groups, tm=tm)
    y_p = _ffn(h_p, row(norm_ffn_g), w_gate_b, w_up_b, w_down_b, tm=tm).reshape(b, s, d)
    page_shape = (b, s // page, page, n_heads, dh)
    k_p, v_p = k.reshape(page_shape), v.reshape(page_shape)

    ms = db * t_len
    xs2 = x_sample.reshape(ms, d)
    qs, k_s, v_s, zs, xbcs, dtrs = _inproj(xs2, *in_args, tm=_row_tile(ms, 1024), **in_kw)
    qs3, ks3, vs3 = (a.reshape(db, t_len, attn_w) for a in (qs, k_s, v_s))
    page_sum = _page_sums(cache_k)
    sel = _sample_select(page_table, page_sum, qs3, n_heads=n_heads, page=page)
    idx = sel[:, :, :t_len, :MOBA_TOP_K].reshape(db, n_heads * t_len * MOBA_TOP_K)
    ok = sel[:, :, :t_len, MOBA_TOP_K:2 * MOBA_TOP_K].reshape(db, n_heads * t_len * MOBA_TOP_K)
    attn_s = _sample_attend(page_table, idx, ok, qs3, ks3, vs3, cache_k, cache_v, n_heads=n_heads)
    yss, conv_s, ssm_s = _ssd_sample(xbcs.reshape(db, t_len, conv_ch), state_conv,
                                     dtrs.reshape(db, t_len, LANES), state_ssm, *ssd_par, **ssd_kw)
    tms = _row_tile(ms, 512)
    h_s = _outproj(attn_s.reshape(ms, attn_w), yss.reshape(ms, ssm_w), zs, xs2,
                   row(attn_out_g), row(ssm_out_g), w_out_b, groups=groups, tm=tms)
    y_s = _ffn(h_s, row(norm_ffn_g), w_gate_b, w_up_b, w_down_b, tm=tms).reshape(db, t_len, d)
    tok_shape = (db, t_len, n_heads, dh)
    return (y_p, y_s, k_p, v_p, conv_p, ssm_p, k_s.reshape(tok_shape), v_s.reshape(tok_shape),
            conv_s, ssm_s)
```

```python
import functools

import jax
import jax.numpy as jnp
from jax import lax
from jax.experimental import pallas as pl
from jax.experimental.pallas import tpu as pltpu

F32 = jnp.float32
BF16 = jnp.bfloat16
EPS = 1e-6
NEG = -1e30

MOBA_BLOCK = 256
MOBA_TOP_K = 3
SSD_CHUNK = 128
SSM_HEAD_DIM = 64
LANES = 128
SUBLANES = 8
VMEM_LIMIT = 56 * 1024 * 1024
FFN_VMEM_LIMIT = 60 * 1024 * 1024

_NT = (((1,), (1,)), ((), ()))
_TN = (((0,), (0,)), ((), ()))


def _params(sem, vmem=VMEM_LIMIT):
    return pltpu.CompilerParams(dimension_semantics=sem, vmem_limit_bytes=vmem)


def _rms(x, g):
    return x * lax.rsqrt(jnp.mean(x * x, axis=-1, keepdims=True) + EPS) * g


def _silu(x):
    return x * (1.0 / (1.0 + jnp.exp(-x)))


def _softplus(x):
    return jnp.maximum(x, 0.0) + jnp.log1p(jnp.exp(-jnp.abs(x)))


def _alibi_slope(h):
    bits = (jnp.full((1, 1), 126, jnp.int32) - h) << 23
    return lax.bitcast_convert_type(bits, F32)


def _inproj_kernel(x_ref, g_ref, w_ref, wdt_ref, qg_ref, kg_ref,
                   q_ref, k_ref, v_ref, z_ref, xbc_ref, dt_ref, xn_ref, *, tn, seg):
    j = pl.program_id(1)

    @pl.when(j == 0)
    def _():
        xn_ref[...] = _rms(x_ref[...], g_ref[...]).astype(BF16)
        dt_ref[...] = jnp.dot(xn_ref[...], wdt_ref[...], preferred_element_type=F32)

    acc = jnp.dot(xn_ref[...], w_ref[...], preferred_element_type=F32)

    def headnorm(out_ref, g):
        for hh in range(tn // LANES):
            sl = slice(hh * LANES, (hh + 1) * LANES)
            out_ref[:, sl] = _rms(acc[:, sl], g)

    @pl.when(j < seg[0])
    def _():
        headnorm(q_ref, qg_ref[...])

    @pl.when((j >= seg[0]) & (j < seg[1]))
    def _():
        headnorm(k_ref, kg_ref[...])

    @pl.when((j >= seg[1]) & (j < seg[2]))
    def _():
        v_ref[...] = acc

    @pl.when((j >= seg[2]) & (j < seg[3]))
    def _():
        z_ref[...] = acc

    @pl.when(j >= seg[3])
    def _():
        xbc_ref[...] = acc


def _inproj(x2, norm_g, w_all, w_dt, q_g, k_g, *, attn_w, ssm_w, conv_ch, tm, tn=512):
    m, d = x2.shape
    n_main = 3 * attn_w + ssm_w + conv_ch
    assert n_main % tn == 0
    seg = (attn_w // tn, 2 * attn_w // tn, 3 * attn_w // tn, (3 * attn_w + ssm_w) // tn)
    nj = n_main // tn

    def col(lo, hi):
        return lambda i, j: (i, jnp.clip(j - lo, 0, hi - lo - 1))

    out_shape = (
        jax.ShapeDtypeStruct((m, attn_w), F32), jax.ShapeDtypeStruct((m, attn_w), F32),
        jax.ShapeDtypeStruct((m, attn_w), F32), jax.ShapeDtypeStruct((m, ssm_w), F32),
        jax.ShapeDtypeStruct((m, conv_ch), F32), jax.ShapeDtypeStruct((m, LANES), F32))
    return pl.pallas_call(
        functools.partial(_inproj_kernel, tn=tn, seg=seg),
        grid=(m // tm, nj),
        in_specs=[
            pl.BlockSpec((tm, d), lambda i, j: (i, 0)),
            pl.BlockSpec((1, d), lambda i, j: (0, 0)),
            pl.BlockSpec((d, tn), lambda i, j: (0, j)),
            pl.BlockSpec((d, LANES), lambda i, j: (0, 0)),
            pl.BlockSpec((1, LANES), lambda i, j: (0, 0)),
            pl.BlockSpec((1, LANES), lambda i, j: (0, 0)),
        ],
        out_specs=(
            pl.BlockSpec((tm, tn), col(0, seg[0])),
            pl.BlockSpec((tm, tn), col(seg[0], seg[1])),
            pl.BlockSpec((tm, tn), col(seg[1], seg[2])),
            pl.BlockSpec((tm, tn), col(seg[2], seg[3])),
            pl.BlockSpec((tm, tn), col(seg[3], nj)),
            pl.BlockSpec((tm, LANES), lambda i, j: (i, 0)),
        ),
        out_shape=out_shape,
        scratch_shapes=[pltpu.VMEM((tm, d), BF16)],
        compiler_params=_params(("parallel", "arbitrary")),
        name="inproj",
    )(x2, norm_g, w_all, w_dt, q_g, k_g)


def _block_select(gate, n_past):
    lane = lax.broadcasted_iota(jnp.int32, gate.shape, 1)
    valid = lane < n_past
    if n_past <= MOBA_TOP_K:
        return valid & (gate > 0.5 * NEG)
    gm = jnp.where(valid, gate, -jnp.inf)
    rank = jnp.zeros(gate.shape, jnp.int32)
    for d in range(1, n_past):
        lower = pltpu.roll(gm, d, axis=1)
        upper = pltpu.roll(gm, LANES - d, axis=1)
        rank = rank + (lower >= gm).astype(jnp.int32) + (upper > gm).astype(jnp.int32)
    return valid & (rank < MOBA_TOP_K) & (gate > 0.5 * NEG)


def _moba_prompt_kernel(q_ref, k_ref, v_ref, o_ref, *, nb, scale):
    blk = MOBA_BLOCK
    h = pl.program_id(1)
    slope = _alibi_slope(h)
    k = k_ref[...]
    kb = k.astype(BF16)
    vb = v_ref[...].astype(BF16)
    means = [jnp.mean(k[j * blk:(j + 1) * blk], axis=0, keepdims=True) for j in range(nb)]
    means = jnp.concatenate(means + [jnp.zeros((LANES - nb, LANES), F32)], axis=0)
    for i in range(nb):
        qi = q_ref[i * blk:(i + 1) * blk, :]
        nk = (i + 1) * blk
        s = lax.dot_general(qi.astype(BF16), kb[:nk], _NT, preferred_element_type=F32)
        col = lax.broadcasted_iota(jnp.int32, (blk, nk), 1)
        logits = s * scale + slope * col.astype(F32)
        row = lax.broadcasted_iota(jnp.int32, (blk, blk), 0)
        colb = lax.broadcasted_iota(jnp.int32, (blk, blk), 1)
        pieces = []
        if i > 0:
            gate = lax.dot_general(qi, means, _NT, precision=lax.Precision.HIGHEST,
                                   preferred_element_type=F32)
            sel = _block_select(gate, i).astype(F32)
            pieces = [jnp.broadcast_to(sel[:, j:j + 1], (blk, blk)) for j in range(i)]
        pieces.append((colb <= row).astype(F32))
        allowed = jnp.concatenate(pieces, axis=1) if len(pieces) > 1 else pieces[0]
        logits = jnp.where(allowed > 0.5, logits, NEG)
        m = jnp.max(logits, axis=-1, keepdims=True)
        p = jnp.exp(logits - m)
        denom = jnp.sum(p, axis=-1, keepdims=True)
        out = jnp.dot(p.astype(BF16), vb[:nk], preferred_element_type=F32)
        o_ref[i * blk:(i + 1) * blk, :] = out / denom


def _moba_prompt(q, k, v, *, n_heads):
    b, s, w = q.shape
    dh = w // n_heads
    spec = pl.BlockSpec((None, s, dh), lambda bi, hi: (bi, 0, hi))
    return pl.pallas_call(
        functools.partial(_moba_prompt_kernel, nb=s // MOBA_BLOCK, scale=dh ** -0.5),
        grid=(b, n_heads),
        in_specs=[spec, spec, spec],
        out_specs=spec,
        out_shape=jax.ShapeDtypeStruct((b, s, w), F32),
        compiler_params=_params(("parallel", "parallel")),
        name="moba_prompt",
    )(q, k, v)


def _gated_norm(y, z, g):
    return _rms(y * _silu(z), g).astype(BF16)


def _ssd_prompt_kernel(xbc_ref, dtr_ref, z_ref, cw_ref, cb_ref, dtb_ref, alog_ref, dskip_ref, sg_ref,
                       expand_ref, s_ref, convp_ref, h_ref, xp_ref, *, cw, ssm_w, groups, n_state):
    c = pl.program_id(1)
    nc = pl.num_programs(1)
    lc = SSD_CHUNK
    p_dim = SSM_HEAD_DIM
    heads = ssm_w // p_dim
    hpg = heads // groups
    gw = hpg * p_dim
    tail = cw - 1

    @pl.when(c == 0)
    def _():
        xp_ref[0:SUBLANES, :] = jnp.zeros((SUBLANES, xp_ref.shape[1]), F32)
        h_ref[...] = jnp.zeros(h_ref.shape, F32)

    xc = xbc_ref[...]
    xp_ref[SUBLANES:SUBLANES + lc, :] = xc
    conv = cb_ref[...]
    for i in range(cw):
        conv = conv + xp_ref[SUBLANES - tail + i:SUBLANES - tail + i + lc, :] * cw_ref[i:i + 1, :]
    act = _silu(conv)
    xp_ref[SUBLANES - tail:SUBLANES, :] = xc[lc - tail:lc, :]

    @pl.when(c == nc - 1)
    def _():
        convp_ref[...] = xc[lc - tail:lc, :]

    dt = _softplus(dtr_ref[...] + dtb_ref[...])
    a = -jnp.exp(alog_ref[...])
    row = lax.broadcasted_iota(jnp.int32, (lc, lc), 0)
    colm = lax.broadcasted_iota(jnp.int32, (lc, lc), 1)
    causal = row >= colm
    tri = causal.astype(F32)
    cum = jnp.dot(tri, dt * a, precision=lax.Precision.HIGHEST, preferred_element_type=F32)
    cum_t = cum.T
    dt_t = dt.T
    cum_last = cum[lc - 1:lc, :]
    dec_end = jnp.exp(cum_last - cum) * dt
    ecum = jnp.exp(cum)
    expand = expand_ref[...]
    ecum_x = jnp.dot(ecum, expand, precision=lax.Precision.HIGHEST, preferred_element_type=F32)
    dec_end_x = jnp.dot(dec_end, expand, precision=lax.Precision.HIGHEST,
                        preferred_element_type=F32)

    xs = act[:, :ssm_w]
    xs_b = xs.astype(BF16)
    y = dskip_ref[...] * xs
    lane = lax.broadcasted_iota(jnp.int32, (lc, 2 * p_dim), 1)
    for g in range(groups):
        bg = act[:, ssm_w + g * n_state:ssm_w + (g + 1) * n_state].astype(BF16)
        cg = act[:, ssm_w + (groups + g) * n_state:ssm_w + (groups + g + 1) * n_state].astype(BF16)
        cb = lax.dot_general(cg, bg, _NT, preferred_element_type=F32)
        hg = h_ref[g * hpg:(g + 1) * hpg].reshape(gw, n_state)
        y_off = lax.dot_general(cg, hg.astype(BF16), _NT, preferred_element_type=F32)
        y_off = y_off * ecum_x[:, g * gw:(g + 1) * gw]
        y_diag = []
        for pr in range(hpg // 2):
            ws = []
            for e in (g * hpg + 2 * pr, g * hpg + 2 * pr + 1):
                diff = cum[:, e:e + 1] - cum_t[e:e + 1, :]
                decay = jnp.where(causal, jnp.exp(diff), 0.0)
                ws.append((cb * decay * dt_t[e:e + 1, :]).astype(BF16))
            lo = (g * hpg + 2 * pr) * p_dim
            xpair = xs_b[:, lo:lo + 2 * p_dim]
            zero = jnp.zeros_like(xpair)
            rhs = jnp.concatenate([jnp.where(lane < p_dim, xpair, zero),
                                   jnp.where(lane >= p_dim, xpair, zero)], axis=0)
            y_diag.append(jnp.dot(jnp.concatenate(ws, axis=1), rhs, preferred_element_type=F32))
        y_g = jnp.concatenate(y_diag, axis=1) + y_off
        xw = (xs[:, g * gw:(g + 1) * gw] * dec_end_x[:, g * gw:(g + 1) * gw]).astype(BF16)
        st = lax.dot_general(xw, bg, _TN, preferred_element_type=F32)
        for ee in range(hpg):
            e = g * hpg + ee
            cd = jnp.exp(cum_t[e:e + 1, lc - 1:lc])
            h_ref[e] = h_ref[e] * cd + st[ee * p_dim:(ee + 1) * p_dim, :]
        gl = slice(g * gw, (g + 1) * gw)
        s_ref[:, gl] = _gated_norm(y[:, gl] + y_g, z_ref[:, gl], sg_ref[:, gl])


def _ssd_prompt(xbc, dtr, z, conv_w, conv_b, dtb, alog, dskip_x, ssm_g, expand, *, ssm_w, groups, n_state):
    b, s, cc = xbc.shape
    cw = conv_w.shape[0]
    heads = ssm_w // SSM_HEAD_DIM
    lc = SSD_CHUNK
    const = lambda shape: pl.BlockSpec(shape, lambda bi, ci: (0,) * len(shape))
    return pl.pallas_call(
        functools.partial(_ssd_prompt_kernel, cw=cw, ssm_w=ssm_w, groups=groups, n_state=n_state),
        grid=(b, s // lc),
        in_specs=[
            pl.BlockSpec((None, lc, cc), lambda bi, ci: (bi, ci, 0)),
            pl.BlockSpec((None, lc, LANES), lambda bi, ci: (bi, ci, 0)),
            pl.BlockSpec((None, lc, ssm_w), lambda bi, ci: (bi, ci, 0)),
            const((cw, cc)), const((1, cc)), const((1, LANES)), const((1, LANES)),
            const((1, ssm_w)), const((1, ssm_w)), const((LANES, ssm_w)),
        ],
        out_specs=(
            pl.BlockSpec((None, lc, ssm_w), lambda bi, ci: (bi, ci, 0)),
            pl.BlockSpec((None, cw - 1, cc), lambda bi, ci: (bi, 0, 0)),
            pl.BlockSpec((None, heads, SSM_HEAD_DIM, n_state), lambda bi, ci: (bi, 0, 0, 0)),
        ),
        out_shape=(
            jax.ShapeDtypeStruct((b, s, ssm_w), BF16),
            jax.ShapeDtypeStruct((b, cw - 1, cc), F32),
            jax.ShapeDtypeStruct((b, heads, SSM_HEAD_DIM, n_state), F32),
        ),
        scratch_shapes=[pltpu.VMEM((SUBLANES + lc, cc), F32)],
        compiler_params=_params(("parallel", "arbitrary")),
        name="ssd_prompt",
    )(xbc, dtr, z, conv_w, conv_b, dtb, alog, dskip_x, ssm_g, expand)


def _outproj_kernel(attn_ref, s_ref, x_ref, ag_ref, w_ref, o_ref, *, rc):
    for r in range(o_ref.shape[0] // rc):
        rows = slice(r * rc, (r + 1) * rc)
        a = _rms(attn_ref[rows, :], ag_ref[...]).astype(BF16)
        mix = jnp.concatenate([a, s_ref[rows, :]], axis=1)
        o_ref[rows, :] = x_ref[rows, :] + jnp.dot(mix, w_ref[...], preferred_element_type=F32)


def _outproj(attn, s_norm, x2, attn_g, w_out, *, tm, rc=256):
    m, attn_w = attn.shape
    ssm_w = s_norm.shape[1]
    d = x2.shape[1]
    return pl.pallas_call(
        functools.partial(_outproj_kernel, rc=min(rc, tm)),
        grid=(m // tm,),
        in_specs=[
            pl.BlockSpec((tm, attn_w), lambda i: (i, 0)),
            pl.BlockSpec((tm, ssm_w), lambda i: (i, 0)),
            pl.BlockSpec((tm, d), lambda i: (i, 0)),
            pl.BlockSpec((1, attn_w), lambda i: (0, 0)),
            pl.BlockSpec((attn_w + ssm_w, d), lambda i: (0, 0)),
        ],
        out_specs=pl.BlockSpec((tm, d), lambda i: (i, 0)),
        out_shape=jax.ShapeDtypeStruct((m, d), F32),
        compiler_params=_params(("parallel",)),
        name="outproj",
    )(attn, s_norm, x2, attn_g, w_out)


def _ffn_kernel(h_ref, g_ref, wg_ref, wu_ref, wd_ref, o_ref, hn_ref, *, rc):
    f = pl.program_id(1)

    @pl.when(f == 0)
    def _():
        hn_ref[...] = _rms(h_ref[...], g_ref[...]).astype(BF16)
        o_ref[...] = h_ref[...]

    for r in range(o_ref.shape[0] // rc):
        rows = slice(r * rc, (r + 1) * rc)
        hn = hn_ref[rows, :]
        gate = jnp.dot(hn, wg_ref[...], preferred_element_type=F32)
        up = jnp.dot(hn, wu_ref[...], preferred_element_type=F32)
        o_ref[rows, :] += jnp.dot((_silu(gate) * up).astype(BF16), wd_ref[...],
                                  preferred_element_type=F32)


def _ffn(h, g, w_gate, w_up, w_down, *, tm, tf=512, rc=512):
    m, d = h.shape
    dff = w_gate.shape[1]
    return pl.pallas_call(
        functools.partial(_ffn_kernel, rc=min(rc, tm)),
        grid=(m // tm, dff // tf),
        in_specs=[
            pl.BlockSpec((tm, d), lambda i, f: (i, 0)),
            pl.BlockSpec((1, d), lambda i, f: (0, 0)),
            pl.BlockSpec((d, tf), lambda i, f: (0, f)),
            pl.BlockSpec((d, tf), lambda i, f: (0, f)),
            pl.BlockSpec((tf, d), lambda i, f: (f, 0)),
        ],
        out_specs=pl.BlockSpec((tm, d), lambda i, f: (i, 0)),
        out_shape=jax.ShapeDtypeStruct((m, d), F32),
        scratch_shapes=[pltpu.VMEM((tm, d), BF16)],
        compiler_params=_params(("parallel", "arbitrary"), vmem=FFN_VMEM_LIMIT),
        name="ffn",
    )(h, g, w_gate, w_up, w_down)


def _page_sum_kernel(ck_ref, o_ref):
    o_ref[...] = jnp.sum(ck_ref[...], axis=1)


def _page_sums(cache_k, *, pages_per_step=16):
    n_phys, page, h, dh = cache_k.shape
    return pl.pallas_call(
        _page_sum_kernel,
        grid=(n_phys // pages_per_step,),
        in_specs=[pl.BlockSpec((pages_per_step, page, h, dh), lambda i: (i, 0, 0, 0))],
        out_specs=pl.BlockSpec((pages_per_step, h, dh), lambda i: (i, 0, 0)),
        out_shape=jax.ShapeDtypeStruct((n_phys, h, dh), F32),
        compiler_params=_params(("parallel",)),
        name="page_sums",
    )(cache_k)


def _sample_select_kernel(pt_ref, ps_ref, q_ref, sel_ref, blk_ref, *, nb, ppb, n_heads, t_len):
    b = pl.program_id(0)
    for n in range(nb):
        acc = ps_ref[pt_ref[b, n * ppb]]
        for pp in range(1, ppb):
            acc = acc + ps_ref[pt_ref[b, n * ppb + pp]]
        blk_ref[n] = acc / MOBA_BLOCK
    lane = lax.broadcasted_iota(jnp.int32, (SUBLANES, LANES), 1)
    for h in range(n_heads):
        qh = q_ref[:, h * LANES:(h + 1) * LANES]
        q8 = jnp.concatenate([qh, jnp.zeros((SUBLANES - t_len, LANES), F32)], axis=0)
        bm = blk_ref[:, h, :]
        gate = lax.dot_general(q8, bm, _NT, precision=lax.Precision.HIGHEST,
                               preferred_element_type=F32)
        lane_n = lax.broadcasted_iota(jnp.int32, gate.shape, 1).astype(F32)
        out = jnp.zeros((SUBLANES, LANES), jnp.int32)
        for kk in range(MOBA_TOP_K):
            mx = jnp.max(gate, axis=-1, keepdims=True)
            idx_f = jnp.min(jnp.where(gate == mx, lane_n, float(nb)), axis=-1, keepdims=True)
            idx = idx_f.astype(jnp.int32)
            ok = (mx > 0.5 * NEG).astype(jnp.int32)
            out = jnp.where(lane == kk, idx, out)
            out = jnp.where(lane == MOBA_TOP_K + kk, ok, out)
            gate = jnp.where(lane_n == idx_f, -jnp.inf, gate)
        sel_ref[h] = out


def _sample_select(page_table, page_sum, q_s, *, n_heads, page):
    db, t_len, w = q_s.shape
    n_pages = page_table.shape[1]
    n_phys, h, dh = page_sum.shape
    ppb = MOBA_BLOCK // page
    nb = n_pages // ppb
    grid_spec = pltpu.PrefetchScalarGridSpec(
        num_scalar_prefetch=1,
        grid=(db,),
        in_specs=[
            pl.BlockSpec((n_phys, h, dh), lambda bi, pt: (0, 0, 0), pipeline_mode=pl.Buffered(1)),
            pl.BlockSpec((None, t_len, w), lambda bi, pt: (bi, 0, 0)),
        ],
        out_specs=pl.BlockSpec((None, n_heads, SUBLANES, LANES), lambda bi, pt: (bi, 0, 0, 0)),
        scratch_shapes=[pltpu.VMEM((nb, h, dh), F32)],
    )
    return pl.pallas_call(
        functools.partial(_sample_select_kernel, nb=nb, ppb=ppb, n_heads=n_heads, t_len=t_len),
        grid_spec=grid_spec,
        out_shape=jax.ShapeDtypeStruct((db, n_heads, SUBLANES, LANES), jnp.int32),
        compiler_params=_params(("arbitrary",)),
        name="sample_select",
    )(page_table, page_sum, q_s)


def _sample_attend_kernel(pt_ref, idx_ref, ok_ref, q_ref, kn_ref, vn_ref, ck_hbm, cv_hbm, o_ref,
                          kbuf, vbuf, sem, *, t_len, page, ppb, past_len, scale):
    b = pl.program_id(0)
    h = pl.program_id(1)
    n_heads = pl.num_programs(1)
    step = b * n_heads + h
    cur = step % 2
    n_sel = t_len * MOBA_TOP_K * ppb
    n_pages = past_len // page

    def copies(st, hh, slot, phys):
        return (pltpu.make_async_copy(ck_hbm.at[phys, :, hh, :], kbuf.at[st, slot], sem.at[st, 0]),
                pltpu.make_async_copy(cv_hbm.at[phys, :, hh, :], vbuf.at[st, slot], sem.at[st, 1]))

    def slot_pages(bb, hh):
        out = []
        for t in range(t_len):
            for kk in range(MOBA_TOP_K):
                n = idx_ref[bb, (hh * t_len + t) * MOBA_TOP_K + kk]
                for pp in range(ppb):
                    lp = jnp.clip(n * ppb + pp, 0, n_pages - 1)
                    out.append(((t * MOBA_TOP_K + kk) * ppb + pp, pt_ref[bb, lp], n * ppb + pp))
        out.append((n_sel, pt_ref[bb, n_pages - 1], None))
        return out

    def start_all(st, bb, hh):
        for slot, phys, _ in slot_pages(bb, hh):
            for cp in copies(st, hh, slot, phys):
                cp.start()

    @pl.when(step == 0)
    def _():
        start_all(0, b, h)

    wrap = h + 1 == n_heads
    b_next = b + wrap.astype(jnp.int32)
    h_next = jnp.where(wrap, 0, h + 1)

    @pl.when(step + 1 < pl.num_programs(0) * n_heads)
    def _():
        start_all(1 - cur, b_next, h_next)

    pages = slot_pages(b, h)

    pad = jnp.zeros((page - t_len, LANES), F32)
    kbuf[cur, n_sel + 1] = jnp.concatenate([kn_ref[...], pad], axis=0)
    vbuf[cur, n_sel + 1] = jnp.concatenate([vn_ref[...], pad], axis=0)

    lane = lax.broadcasted_iota(jnp.int32, (1, page), 1)
    pos, okv = [], []
    for slot, _, lp in pages[:-1]:
        n_ok = ok_ref[b, (h * t_len + slot // (MOBA_TOP_K * ppb)) * MOBA_TOP_K + (slot // ppb) % MOBA_TOP_K]
        pos.append(lane + jnp.clip(lp, 0, n_pages - 1) * page)
        okv.append(jnp.broadcast_to(n_ok, (1, page)))
    pos.append(lane + (past_len - page))
    pos.append(lane + past_len)
    pos = jnp.concatenate(pos, axis=1)
    okv = jnp.concatenate(okv, axis=1)

    for slot, phys, _ in pages:
        for cp in copies(cur, h, slot, phys):
            cp.wait()

    n_keys = (n_sel + 2) * page
    kall = kbuf[cur].reshape(n_keys, LANES).astype(BF16)
    vall = vbuf[cur].reshape(n_keys, LANES).astype(BF16)
    q8 = jnp.concatenate([q_ref[...], jnp.zeros((SUBLANES - t_len, LANES), F32)], axis=0)
    s = lax.dot_general(q8.astype(BF16), kall, _NT, preferred_element_type=F32)

    slope = _alibi_slope(h)
    trow = lax.broadcasted_iota(jnp.int32, (SUBLANES, n_keys), 0)
    kcol = lax.broadcasted_iota(jnp.int32, (SUBLANES, n_keys), 1)
    q_pos = past_len + trow
    dist = (q_pos - pos).astype(F32)
    logits = s * scale - slope * dist
    per_t = MOBA_TOP_K * ppb * page
    sel_cols = n_sel * page
    sel_ok = (kcol // per_t == trow) & (jnp.concatenate(
        [okv, jnp.zeros((1, n_keys - sel_cols), jnp.int32)], axis=1) > 0)
    blk_start = (q_pos // MOBA_BLOCK) * MOBA_BLOCK
    own_ok = (kcol >= sel_cols) & (pos <= q_pos) & (pos >= blk_start) & (kcol < sel_cols + page + t_len)
    logits = jnp.where(sel_ok | own_ok, logits, NEG)
    m = jnp.max(logits, axis=-1, keepdims=True)
    p = jnp.exp(logits - m)
    denom = jnp.sum(p, axis=-1, keepdims=True)
    out = jnp.dot(p.astype(BF16), vall, preferred_element_type=F32) / denom
    o_ref[...] = out[:t_len, :]


def _sample_attend(page_table, idx, ok, q_s, k_s, v_s, cache_k, cache_v, *, n_heads):
    db, t_len, w = q_s.shape
    page = cache_k.shape[1]
    n_pages = page_table.shape[1]
    ppb = MOBA_BLOCK // page
    n_slots = t_len * MOBA_TOP_K * ppb + 2
    tok = pl.BlockSpec((None, t_len, LANES), lambda bi, hi, *_: (bi, 0, hi))
    grid_spec = pltpu.PrefetchScalarGridSpec(
        num_scalar_prefetch=3,
        grid=(db, n_heads),
        in_specs=[tok, tok, tok,
                  pl.BlockSpec(memory_space=pl.ANY), pl.BlockSpec(memory_space=pl.ANY)],
        out_specs=tok,
        scratch_shapes=[pltpu.VMEM((2, n_slots, page, LANES), F32),
                        pltpu.VMEM((2, n_slots, page, LANES), F32),
                        pltpu.SemaphoreType.DMA((2, 2))],
    )
    return pl.pallas_call(
        functools.partial(_sample_attend_kernel, t_len=t_len, page=page, ppb=ppb,
                          past_len=n_pages * page, scale=LANES ** -0.5),
        grid_spec=grid_spec,
        out_shape=jax.ShapeDtypeStruct((db, t_len, w), F32),
        compiler_params=_params(("arbitrary", "arbitrary")),
        name="sample_attend",
    )(page_table, idx, ok, q_s, k_s, v_s, cache_k, cache_v)


def _ssd_sample_kernel(xbc_ref, sc_ref, dtr_ref, z_ref, h0_ref, cw_ref, cb_ref, dtb_ref, alog_ref,
                       dskip_ref, sg_ref, expand_ref, s_ref, convs_ref, h_ref, xp_ref,
                       *, cw, t_len, ssm_w, groups, n_state):
    p_dim = SSM_HEAD_DIM
    heads = ssm_w // p_dim
    hpg = heads // groups
    gw = hpg * p_dim
    tail = cw - 1
    xp_ref[0:tail, :] = sc_ref[...]
    xp_ref[tail:tail + t_len, :] = xbc_ref[...]
    conv = cb_ref[...]
    for i in range(cw):
        conv = conv + xp_ref[i:i + t_len, :] * cw_ref[i:i + 1, :]
    act = _silu(conv)
    convs_ref[...] = xp_ref[t_len:t_len + tail, :]

    dt = _softplus(dtr_ref[...] + dtb_ref[...])
    da = dt * (-jnp.exp(alog_ref[...]))
    cum = [da[0:1]]
    for t in range(1, t_len):
        cum.append(cum[-1] + da[t:t + 1])
    lane = lax.broadcasted_iota(jnp.int32, (1, LANES), 1)
    xs = act[:, :ssm_w]
    zpad = jnp.zeros((SUBLANES - t_len, n_state), F32)

    b8, c8, cbs = [], [], []
    for g in range(groups):
        bg = act[:, ssm_w + g * n_state:ssm_w + (g + 1) * n_state]
        cg = act[:, ssm_w + (groups + g) * n_state:ssm_w + (groups + g + 1) * n_state]
        b8.append(jnp.concatenate([bg, zpad], axis=0).astype(BF16))
        c8.append(jnp.concatenate([cg, zpad], axis=0).astype(BF16))
        cbs.append(lax.dot_general(c8[g], b8[g], _NT, preferred_element_type=F32))

    pairs = [(l, s) for l in range(t_len) for s in range(l + 1)]
    rows = []
    for l, s in pairs:
        cb_ls = jnp.zeros((1, LANES), F32)
        for g in range(groups):
            in_g = (lane >= g * hpg) & (lane < (g + 1) * hpg)
            cb_ls = jnp.where(in_g, cbs[g][l:l + 1, s:s + 1], cb_ls)
        rows.append(cb_ls * jnp.exp(cum[l] - cum[s]) * dt[s:s + 1])
    rows += [jnp.exp(cum[l]) for l in range(t_len)]
    rows += [jnp.exp(cum[-1] - cum[s]) * dt[s:s + 1] for s in range(t_len)]
    n_rows = -(-len(rows) // SUBLANES) * SUBLANES
    rows.append(jnp.zeros((n_rows - len(rows), LANES), F32))
    rx = jnp.dot(jnp.concatenate(rows, axis=0), expand_ref[...], precision=lax.Precision.HIGHEST,
                 preferred_element_type=F32)
    r_ecum = len(pairs)
    r_coef = r_ecum + t_len

    y_diag = []
    for l in range(t_len):
        acc = jnp.zeros((1, ssm_w), F32)
        for s in range(l + 1):
            i = pairs.index((l, s))
            acc = acc + rx[i:i + 1, :] * xs[s:s + 1, :]
        y_diag.append(acc)
    y = dskip_ref[...] * xs + jnp.concatenate(y_diag, axis=0)

    y_off = []
    for g in range(groups):
        hg = h0_ref[g * hpg:(g + 1) * hpg].reshape(gw, n_state)
        y_off.append(lax.dot_general(c8[g], hg.astype(BF16), _NT, preferred_element_type=F32)[:t_len])
    y = y + jnp.concatenate(y_off, axis=1) * rx[r_ecum:r_ecum + t_len, :]
    for g in range(groups):
        gl = slice(g * gw, (g + 1) * gw)
        s_ref[:, gl] = _gated_norm(y[:, gl], z_ref[:, gl], sg_ref[:, gl])

    e_last = jnp.exp(cum[-1])
    xw = xs * rx[r_coef:r_coef + t_len, :]
    for g in range(groups):
        xw8 = jnp.concatenate([xw[:, g * gw:(g + 1) * gw], jnp.zeros((SUBLANES - t_len, gw), F32)], axis=0)
        st = lax.dot_general(xw8.astype(BF16), b8[g], _TN, preferred_element_type=F32)
        for ee in range(hpg):
            e = g * hpg + ee
            h_ref[e] = h0_ref[e] * e_last[:, e:e + 1] + st[ee * p_dim:(ee + 1) * p_dim, :]


def _ssd_sample(xbc, state_conv, dtr, z, state_ssm, conv_w, conv_b, dtb, alog, dskip_x, ssm_g, expand,
                *, ssm_w, groups, n_state):
    db, t_len, cc = xbc.shape
    cw = conv_w.shape[0]
    heads = ssm_w // SSM_HEAD_DIM
    const = lambda shape: pl.BlockSpec(shape, lambda bi: (0,) * len(shape))
    return pl.pallas_call(
        functools.partial(_ssd_sample_kernel, cw=cw, t_len=t_len, ssm_w=ssm_w, groups=groups,
                          n_state=n_state),
        grid=(db,),
        in_specs=[
            pl.BlockSpec((None, t_len, cc), lambda bi: (bi, 0, 0)),
            pl.BlockSpec((None, cw - 1, cc), lambda bi: (bi, 0, 0)),
            pl.BlockSpec((None, t_len, LANES), lambda bi: (bi, 0, 0)),
            pl.BlockSpec((None, t_len, ssm_w), lambda bi: (bi, 0, 0)),
            pl.BlockSpec((None, heads, SSM_HEAD_DIM, n_state), lambda bi: (bi, 0, 0, 0)),
            const((cw, cc)), const((1, cc)), const((1, LANES)), const((1, LANES)), const((1, ssm_w)),
            const((1, ssm_w)), const((LANES, ssm_w)),
        ],
        out_specs=(
            pl.BlockSpec((None, t_len, ssm_w), lambda bi: (bi, 0, 0)),
            pl.BlockSpec((None, cw - 1, cc), lambda bi: (bi, 0, 0)),
            pl.BlockSpec((None, heads, SSM_HEAD_DIM, n_state), lambda bi: (bi, 0, 0, 0)),
        ),
        out_shape=(
            jax.ShapeDtypeStruct((db, t_len, ssm_w), BF16),
            jax.ShapeDtypeStruct((db, cw - 1, cc), F32),
            jax.ShapeDtypeStruct((db, heads, SSM_HEAD_DIM, n_state), F32),
        ),
        scratch_shapes=[pltpu.VMEM((2 * SUBLANES, cc), F32)],
        compiler_params=_params(("parallel",)),
        name="ssd_sample",
    )(xbc, state_conv, dtr, z, state_ssm, conv_w, conv_b, dtb, alog, dskip_x, ssm_g, expand)


def _row_tile(m, target):
    tm = min(m, target)
    assert m % tm == 0, (m, tm)
    return tm


def kernel(x_prompt, x_sample, cache_k, cache_v, page_table, state_conv, state_ssm, norm_mix_g, w_in, q_norm_g, k_norm_g, conv_w, conv_b, dt_bias, a_log, d_skip, attn_out_g, ssm_out_g, w_out, norm_ffn_g, w_gate, w_up, w_down):
    b, s, d = x_prompt.shape
    db, t_len, _ = x_sample.shape
    n_phys, page, n_heads, dh = cache_k.shape
    attn_w = n_heads * dh
    heads, p_dim, n_state = state_ssm.shape[1:]
    assert p_dim == SSM_HEAD_DIM and dh == LANES and heads <= LANES
    ssm_w = heads * p_dim
    conv_ch = state_conv.shape[2]
    groups = (conv_ch - ssm_w) // (2 * n_state)
    n_main = 3 * attn_w + ssm_w + conv_ch

    w_in_b = w_in.astype(BF16)
    w_dt = jnp.pad(w_in[:, n_main:].astype(BF16), ((0, 0), (0, LANES - heads)))
    w_out_b, w_gate_b, w_up_b, w_down_b = (w.astype(BF16) for w in (w_out, w_gate, w_up, w_down))
    row = lambda v: v.reshape(1, -1).astype(F32)
    padl = lambda v: jnp.pad(v.astype(F32), (0, LANES - v.shape[0])).reshape(1, LANES)
    dskip_x = jnp.repeat(d_skip.astype(F32), p_dim).reshape(1, ssm_w)
    expand = (jnp.arange(LANES)[:, None] == (jnp.arange(ssm_w)[None, :] // p_dim)).astype(F32)
    in_args = (row(norm_mix_g), w_in_b, w_dt, row(q_norm_g), row(k_norm_g))
    in_kw = dict(attn_w=attn_w, ssm_w=ssm_w, conv_ch=conv_ch)
    ssd_kw = dict(ssm_w=ssm_w, groups=groups, n_state=n_state)
    ssd_par = (conv_w, row(conv_b), padl(dt_bias), padl(a_log), dskip_x)

    xp2 = x_prompt.reshape(b * s, d)
    q, k, v, z, xbc, dtr = _inproj(xp2, *in_args, tm=_row_tile(b * s, 1024), **in_kw)
    attn = _moba_prompt(q.reshape(b, s, attn_w), k.reshape(b, s, attn_w), v.reshape(b, s, attn_w),
                        n_heads=n_heads)
    sn, conv_p, ssm_p = _ssd_prompt(xbc.reshape(b, s, conv_ch), dtr.reshape(b, s, LANES),
                                    z.reshape(b, s, ssm_w), *ssd_par, row(ssm_out_g), expand, **ssd_kw)
    h_p = _outproj(attn.reshape(b * s, attn_w), sn.reshape(b * s, ssm_w), xp2, row(attn_out_g), w_out_b,
                   tm=_row_tile(b * s, 512))
    y_p = _ffn(h_p, row(norm_ffn_g), w_gate_b, w_up_b, w_down_b,
               tm=_row_tile(b * s, 1024)).reshape(b, s, d)
    page_shape = (b, s // page, page, n_heads, dh)
    k_p, v_p = k.reshape(page_shape), v.reshape(page_shape)

    ms = db * t_len
    xs2 = x_sample.reshape(ms, d)
    qs, k_s, v_s, zs, xbcs, dtrs = _inproj(xs2, *in_args, tm=_row_tile(ms, 1024), **in_kw)
    qs3, ks3, vs3 = (a.reshape(db, t_len, attn_w) for a in (qs, k_s, v_s))
    page_sum = _page_sums(cache_k)
    sel = _sample_select(page_table, page_sum, qs3, n_heads=n_heads, page=page)
    idx = sel[:, :, :t_len, :MOBA_TOP_K].reshape(db, n_heads * t_len * MOBA_TOP_K)
    ok = sel[:, :, :t_len, MOBA_TOP_K:2 * MOBA_TOP_K].reshape(db, n_heads * t_len * MOBA_TOP_K)
    attn_s = _sample_attend(page_table, idx, ok, qs3, ks3, vs3, cache_k, cache_v, n_heads=n_heads)
    sns, conv_s, ssm_s = _ssd_sample(xbcs.reshape(db, t_len, conv_ch), state_conv,
                                     dtrs.reshape(db, t_len, LANES), zs.reshape(db, t_len, ssm_w), state_ssm,
                                     *ssd_par, row(ssm_out_g), expand, **ssd_kw)
    tms = _row_tile(ms, 512)
    h_s = _outproj(attn_s.reshape(ms, attn_w), sns.reshape(ms, ssm_w), xs2, row(attn_out_g), w_out_b, tm=tms)
    y_s = _ffn(h_s, row(norm_ffn_g), w_gate_b, w_up_b, w_down_b, tm=tms).reshape(db, t_len, d)
    tok_shape = (db, t_len, n_heads, dh)
    return (y_p, y_s, k_p, v_p, conv_p, ssm_p, k_s.reshape(tok_shape), v_s.reshape(tok_shape),
            conv_s, ssm_s)
```

```python
import functools

import jax
import jax.numpy as jnp
from jax import lax
from jax.experimental import pallas as pl
from jax.experimental.pallas import tpu as pltpu

F32 = jnp.float32
BF16 = jnp.bfloat16
EPS = 1e-6
NEG = -1e30

MOBA_BLOCK = 256
MOBA_TOP_K = 3
SSD_CHUNK = 128
SSM_HEAD_DIM = 64
LANES = 128
SUBLANES = 8
VMEM_LIMIT = 56 * 1024 * 1024
FFN_VMEM_LIMIT = 60 * 1024 * 1024

_NT = (((1,), (1,)), ((), ()))
_TN = (((0,), (0,)), ((), ()))


def _params(sem, vmem=VMEM_LIMIT):
    return pltpu.CompilerParams(dimension_semantics=sem, vmem_limit_bytes=vmem)


def _rms(x, g):
    return x * lax.rsqrt(jnp.mean(x * x, axis=-1, keepdims=True) + EPS) * g


def _silu(x):
    return x * (1.0 / (1.0 + jnp.exp(-x)))


def _softplus(x):
    return jnp.maximum(x, 0.0) + jnp.log1p(jnp.exp(-jnp.abs(x)))


def _alibi_slope(h):
    bits = (jnp.full((1, 1), 126, jnp.int32) - h) << 23
    return lax.bitcast_convert_type(bits, F32)


def _inproj_kernel(x_ref, g_ref, w_ref, wdt_ref, qg_ref, kg_ref,
                   q_ref, k_ref, v_ref, z_ref, xbc_ref, dt_ref, xn_ref, *, tn, seg):
    j = pl.program_id(1)

    @pl.when(j == 0)
    def _():
        xn_ref[...] = _rms(x_ref[...], g_ref[...]).astype(BF16)
        dt_ref[...] = jnp.dot(xn_ref[...], wdt_ref[...], preferred_element_type=F32)

    acc = jnp.dot(xn_ref[...], w_ref[...].astype(BF16), preferred_element_type=F32)

    def headnorm(out_ref, g):
        for hh in range(tn // LANES):
            sl = slice(hh * LANES, (hh + 1) * LANES)
            out_ref[:, sl] = _rms(acc[:, sl], g)

    @pl.when(j < seg[0])
    def _():
        headnorm(q_ref, qg_ref[...])

    @pl.when((j >= seg[0]) & (j < seg[1]))
    def _():
        headnorm(k_ref, kg_ref[...])

    @pl.when((j >= seg[1]) & (j < seg[2]))
    def _():
        v_ref[...] = acc

    @pl.when((j >= seg[2]) & (j < seg[3]))
    def _():
        z_ref[...] = acc

    @pl.when(j >= seg[3])
    def _():
        xbc_ref[...] = acc


def _inproj(x2, norm_g, w_all, w_dt, q_g, k_g, *, attn_w, ssm_w, conv_ch, tm, tn=512):
    m, d = x2.shape
    n_main = 3 * attn_w + ssm_w + conv_ch
    assert n_main % tn == 0
    seg = (attn_w // tn, 2 * attn_w // tn, 3 * attn_w // tn, (3 * attn_w + ssm_w) // tn)
    nj = n_main // tn

    def col(lo, hi):
        return lambda i, j: (i, jnp.clip(j - lo, 0, hi - lo - 1))

    out_shape = (
        jax.ShapeDtypeStruct((m, attn_w), F32), jax.ShapeDtypeStruct((m, attn_w), F32),
        jax.ShapeDtypeStruct((m, attn_w), F32), jax.ShapeDtypeStruct((m, ssm_w), F32),
        jax.ShapeDtypeStruct((m, conv_ch), F32), jax.ShapeDtypeStruct((m, LANES), F32))
    return pl.pallas_call(
        functools.partial(_inproj_kernel, tn=tn, seg=seg),
        grid=(m // tm, nj),
        in_specs=[
            pl.BlockSpec((tm, d), lambda i, j: (i, 0)),
            pl.BlockSpec((1, d), lambda i, j: (0, 0)),
            pl.BlockSpec((d, tn), lambda i, j: (0, j)),
            pl.BlockSpec((d, LANES), lambda i, j: (0, 0)),
            pl.BlockSpec((1, LANES), lambda i, j: (0, 0)),
            pl.BlockSpec((1, LANES), lambda i, j: (0, 0)),
        ],
        out_specs=(
            pl.BlockSpec((tm, tn), col(0, seg[0])),
            pl.BlockSpec((tm, tn), col(seg[0], seg[1])),
            pl.BlockSpec((tm, tn), col(seg[1], seg[2])),
            pl.BlockSpec((tm, tn), col(seg[2], seg[3])),
            pl.BlockSpec((tm, tn), col(seg[3], nj)),
            pl.BlockSpec((tm, LANES), lambda i, j: (i, 0)),
        ),
        out_shape=out_shape,
        scratch_shapes=[pltpu.VMEM((tm, d), BF16)],
        compiler_params=_params(("parallel", "arbitrary")),
        name="inproj",
    )(x2, norm_g, w_all, w_dt, q_g, k_g)


def _block_select(gate, n_past):
    lane = lax.broadcasted_iota(jnp.int32, gate.shape, 1)
    valid = lane < n_past
    if n_past <= MOBA_TOP_K:
        return valid & (gate > 0.5 * NEG)
    gm = jnp.where(valid, gate, -jnp.inf)
    rank = jnp.zeros(gate.shape, jnp.int32)
    for d in range(1, n_past):
        lower = pltpu.roll(gm, d, axis=1)
        upper = pltpu.roll(gm, LANES - d, axis=1)
        rank = rank + (lower >= gm).astype(jnp.int32) + (upper > gm).astype(jnp.int32)
    return valid & (rank < MOBA_TOP_K) & (gate > 0.5 * NEG)


def _page_chunk_copy(ck_hbm, buf, sem, first_page, n_pages):
    return pltpu.make_async_copy(ck_hbm.at[pl.ds(first_page, n_pages)], buf, sem)


def _page_key_sums(pages, lanes_of_adds=4):
    n, page, heads, dh = pages.shape
    part = pages.reshape(n * lanes_of_adds, page // lanes_of_adds, heads, dh).sum(axis=1)
    return part.reshape(n, lanes_of_adds, heads, dh).sum(axis=1)


def _causal_chunks(total, nb):
    tri = nb * (nb + 1) // 2
    cum = [0] + [(total * (i + 1) * (i + 2) // 2 + tri // 2) // tri for i in range(nb)]
    return [hi - lo for lo, hi in zip(cum[:-1], cum[1:])], cum[:-1]


def _moba_prompt_kernel(q_ref, k_ref, v_ref, ck_hbm, o_ref, ps_ref, pbuf, psem, *, nb, scale, pages):
    blk = MOBA_BLOCK
    h = pl.program_id(1)
    step = pl.program_id(0) * pl.num_programs(1) + h
    page0 = step * pages
    sizes, offs = _causal_chunks(pages, nb)
    chunks = [i for i in range(nb) if sizes[i] > 0]
    n_slots = pbuf.shape[0]

    def page_copy(c):
        i, slot = chunks[c], c % n_slots
        return _page_chunk_copy(ck_hbm, pbuf.at[slot, pl.ds(0, sizes[i])], psem.at[slot],
                                page0 + offs[i], sizes[i])

    for c in range(min(n_slots - 1, len(chunks))):
        page_copy(c).start()
    slope = _alibi_slope(h)
    k = k_ref[...]
    kb = k.astype(BF16)
    vb = v_ref[...].astype(BF16)
    means = [jnp.mean(k[j * blk:(j + 1) * blk], axis=0, keepdims=True) for j in range(nb)]
    means = jnp.concatenate(means + [jnp.zeros((LANES - nb, LANES), F32)], axis=0)
    for i in range(nb):
        qi = q_ref[i * blk:(i + 1) * blk, :]
        nk = (i + 1) * blk
        s = lax.dot_general(qi.astype(BF16), kb[:nk], _NT, preferred_element_type=F32)
        col = lax.broadcasted_iota(jnp.int32, (blk, nk), 1)
        logits = s * scale + slope * col.astype(F32)
        row = lax.broadcasted_iota(jnp.int32, (blk, blk), 0)
        colb = lax.broadcasted_iota(jnp.int32, (blk, blk), 1)
        pieces = []
        if i > 0:
            gate = lax.dot_general(qi, means, _NT, precision=lax.Precision.HIGHEST,
                                   preferred_element_type=F32)
            sel = _block_select(gate, i).astype(F32)
            pieces = [jnp.broadcast_to(sel[:, j:j + 1], (blk, blk)) for j in range(i)]
        pieces.append((colb <= row).astype(F32))
        allowed = jnp.concatenate(pieces, axis=1) if len(pieces) > 1 else pieces[0]
        logits = jnp.where(allowed > 0.5, logits, NEG)
        m = jnp.max(logits, axis=-1, keepdims=True)
        p = jnp.exp(logits - m)
        denom = jnp.sum(p, axis=-1, keepdims=True)
        out = jnp.dot(p.astype(BF16), vb[:nk], preferred_element_type=F32)
        o_ref[i * blk:(i + 1) * blk, :] = out / denom
        if i in chunks:
            c = chunks.index(i)
            page_copy(c).wait()
            if c + n_slots - 1 < len(chunks):
                page_copy(c + n_slots - 1).start()
            ps_ref[offs[i]:offs[i] + sizes[i]] = _page_key_sums(pbuf[c % n_slots, 0:sizes[i]])


def _moba_prompt(q, k, v, cache_k, *, n_heads, pages_per_step):
    b, s, w = q.shape
    dh = w // n_heads
    nb = s // MOBA_BLOCK
    npc = max(_causal_chunks(pages_per_step, nb)[0])
    page, ch, cdh = cache_k.shape[1:]
    spec = pl.BlockSpec((None, s, dh), lambda bi, hi: (bi, 0, hi))
    return pl.pallas_call(
        functools.partial(_moba_prompt_kernel, nb=nb, scale=dh ** -0.5, pages=pages_per_step),
        grid=(b, n_heads),
        in_specs=[spec, spec, spec, pl.BlockSpec(memory_space=pl.ANY)],
        out_specs=(spec, pl.BlockSpec((pages_per_step, ch, cdh), lambda bi, hi: (bi * n_heads + hi, 0, 0))),
        out_shape=(jax.ShapeDtypeStruct((b, s, w), F32),
                   jax.ShapeDtypeStruct((b * n_heads * pages_per_step, ch, cdh), F32)),
        scratch_shapes=[pltpu.VMEM((3, npc, page, ch, cdh), F32), pltpu.SemaphoreType.DMA((3,))],
        compiler_params=_params(("arbitrary", "arbitrary")),
        name="moba_prompt",
    )(q, k, v, cache_k)


def _gated_norm(y, z, g):
    return _rms(y * _silu(z), g).astype(BF16)


def _ssd_prompt_kernel(xbc_ref, dtr_ref, z_ref, cw_ref, cb_ref, dtb_ref, alog_ref, dskip_ref, sg_ref,
                       expand_ref, ck_hbm, s_ref, convp_ref, h_ref, ps_ref, xp_ref, pbuf, psem,
                       *, cw, ssm_w, groups, n_state, page0):
    c = pl.program_id(1)
    nc = pl.num_programs(1)
    n_pg = pbuf.shape[1]
    step = pl.program_id(0) * nc + c
    slot = step % 2

    def page_copy(st, sl):
        return _page_chunk_copy(ck_hbm, pbuf.at[sl], psem.at[sl], page0 + st * n_pg, n_pg)

    @pl.when(step == 0)
    def _():
        page_copy(0, 0).start()

    page_copy(step, slot).wait()

    @pl.when(step + 1 < pl.num_programs(0) * nc)
    def _():
        page_copy(step + 1, 1 - slot).start()

    ps_ref[...] = _page_key_sums(pbuf[slot])
    lc = SSD_CHUNK
    p_dim = SSM_HEAD_DIM
    heads = ssm_w // p_dim
    hpg = heads // groups
    gw = hpg * p_dim
    tail = cw - 1

    @pl.when(c == 0)
    def _():
        xp_ref[0:SUBLANES, :] = jnp.zeros((SUBLANES, xp_ref.shape[1]), F32)
        h_ref[...] = jnp.zeros(h_ref.shape, F32)

    xc = xbc_ref[...]
    xp_ref[SUBLANES:SUBLANES + lc, :] = xc
    conv = cb_ref[...]
    for i in range(cw):
        conv = conv + xp_ref[SUBLANES - tail + i:SUBLANES - tail + i + lc, :] * cw_ref[i:i + 1, :]
    act = _silu(conv)
    xp_ref[SUBLANES - tail:SUBLANES, :] = xc[lc - tail:lc, :]

    @pl.when(c == nc - 1)
    def _():
        convp_ref[...] = xc[lc - tail:lc, :]

    dt = _softplus(dtr_ref[...] + dtb_ref[...])
    a = -jnp.exp(alog_ref[...])
    row = lax.broadcasted_iota(jnp.int32, (lc, lc), 0)
    colm = lax.broadcasted_iota(jnp.int32, (lc, lc), 1)
    causal = row >= colm
    tri = causal.astype(F32)
    cum = jnp.dot(tri, dt * a, precision=lax.Precision.HIGHEST, preferred_element_type=F32)
    cum_t = cum.T
    dt_t = dt.T
    cum_last = cum[lc - 1:lc, :]
    dec_end = jnp.exp(cum_last - cum) * dt
    ecum = jnp.exp(cum)
    expand = expand_ref[...]
    ecum_x = jnp.dot(ecum, expand, precision=lax.Precision.HIGHEST, preferred_element_type=F32)
    dec_end_x = jnp.dot(dec_end, expand, precision=lax.Precision.HIGHEST,
                        preferred_element_type=F32)

    xs = act[:, :ssm_w]
    xs_b = xs.astype(BF16)
    y = dskip_ref[...] * xs
    lane = lax.broadcasted_iota(jnp.int32, (lc, 2 * p_dim), 1)
    for g in range(groups):
        bg = act[:, ssm_w + g * n_state:ssm_w + (g + 1) * n_state].astype(BF16)
        cg = act[:, ssm_w + (groups + g) * n_state:ssm_w + (groups + g + 1) * n_state].astype(BF16)
        cb = lax.dot_general(cg, bg, _NT, preferred_element_type=F32)
        hg = h_ref[g * hpg:(g + 1) * hpg].reshape(gw, n_state)
        y_off = lax.dot_general(cg, hg.astype(BF16), _NT, preferred_element_type=F32)
        y_off = y_off * ecum_x[:, g * gw:(g + 1) * gw]
        y_diag = []
        for pr in range(hpg // 2):
            ws = []
            for e in (g * hpg + 2 * pr, g * hpg + 2 * pr + 1):
                diff = cum[:, e:e + 1] - cum_t[e:e + 1, :]
                decay = jnp.where(causal, jnp.exp(diff), 0.0)
                ws.append((cb * decay * dt_t[e:e + 1, :]).astype(BF16))
            lo = (g * hpg + 2 * pr) * p_dim
            xpair = xs_b[:, lo:lo + 2 * p_dim]
            zero = jnp.zeros_like(xpair)
            rhs = jnp.concatenate([jnp.where(lane < p_dim, xpair, zero),
                                   jnp.where(lane >= p_dim, xpair, zero)], axis=0)
            y_diag.append(jnp.dot(jnp.concatenate(ws, axis=1), rhs, preferred_element_type=F32))
        y_g = jnp.concatenate(y_diag, axis=1) + y_off
        xw = (xs[:, g * gw:(g + 1) * gw] * dec_end_x[:, g * gw:(g + 1) * gw]).astype(BF16)
        st = lax.dot_general(xw, bg, _TN, preferred_element_type=F32)
        for ee in range(hpg):
            e = g * hpg + ee
            cd = jnp.exp(cum_t[e:e + 1, lc - 1:lc])
            h_ref[e] = h_ref[e] * cd + st[ee * p_dim:(ee + 1) * p_dim, :]
        gl = slice(g * gw, (g + 1) * gw)
        s_ref[:, gl] = _gated_norm(y[:, gl] + y_g, z_ref[:, gl], sg_ref[:, gl])


def _ssd_prompt(xbc, dtr, z, conv_w, conv_b, dtb, alog, dskip_x, ssm_g, expand, cache_k,
                *, ssm_w, groups, n_state, page0, pages_per_step):
    b, s, cc = xbc.shape
    cw = conv_w.shape[0]
    heads = ssm_w // SSM_HEAD_DIM
    lc = SSD_CHUNK
    nc = s // lc
    page, ch, cdh = cache_k.shape[1:]
    const = lambda shape: pl.BlockSpec(shape, lambda bi, ci: (0,) * len(shape))
    return pl.pallas_call(
        functools.partial(_ssd_prompt_kernel, cw=cw, ssm_w=ssm_w, groups=groups, n_state=n_state,
                          page0=page0),
        grid=(b, nc),
        in_specs=[
            pl.BlockSpec((None, lc, cc), lambda bi, ci: (bi, ci, 0)),
            pl.BlockSpec((None, lc, LANES), lambda bi, ci: (bi, ci, 0)),
            pl.BlockSpec((None, lc, ssm_w), lambda bi, ci: (bi, ci, 0)),
            const((cw, cc)), const((1, cc)), const((1, LANES)), const((1, LANES)),
            const((1, ssm_w)), const((1, ssm_w)), const((LANES, ssm_w)),
            pl.BlockSpec(memory_space=pl.ANY),
        ],
        out_specs=(
            pl.BlockSpec((None, lc, ssm_w), lambda bi, ci: (bi, ci, 0)),
            pl.BlockSpec((None, cw - 1, cc), lambda bi, ci: (bi, 0, 0)),
            pl.BlockSpec((None, heads, SSM_HEAD_DIM, n_state), lambda bi, ci: (bi, 0, 0, 0)),
            pl.BlockSpec((pages_per_step, ch, cdh), lambda bi, ci: (bi * nc + ci, 0, 0)),
        ),
        out_shape=(
            jax.ShapeDtypeStruct((b, s, ssm_w), BF16),
            jax.ShapeDtypeStruct((b, cw - 1, cc), F32),
            jax.ShapeDtypeStruct((b, heads, SSM_HEAD_DIM, n_state), F32),
            jax.ShapeDtypeStruct((b * nc * pages_per_step, ch, cdh), F32),
        ),
        scratch_shapes=[pltpu.VMEM((SUBLANES + lc, cc), F32),
                        pltpu.VMEM((2, pages_per_step, page, ch, cdh), F32),
                        pltpu.SemaphoreType.DMA((2,))],
        compiler_params=_params(("arbitrary", "arbitrary")),
        name="ssd_prompt",
    )(xbc, dtr, z, conv_w, conv_b, dtb, alog, dskip_x, ssm_g, expand, cache_k)


def _outproj_kernel(attn_ref, s_ref, x_ref, ag_ref, w_ref, o_ref, *, rc):
    for r in range(o_ref.shape[0] // rc):
        rows = slice(r * rc, (r + 1) * rc)
        a = _rms(attn_ref[rows, :], ag_ref[...]).astype(BF16)
        mix = jnp.concatenate([a, s_ref[rows, :]], axis=1)
        o_ref[rows, :] = x_ref[rows, :] + jnp.dot(mix, w_ref[...], preferred_element_type=F32)


def _outproj(attn, s_norm, x2, attn_g, w_out, *, tm, rc=256):
    m, attn_w = attn.shape
    ssm_w = s_norm.shape[1]
    d = x2.shape[1]
    return pl.pallas_call(
        functools.partial(_outproj_kernel, rc=min(rc, tm)),
        grid=(m // tm,),
        in_specs=[
            pl.BlockSpec((tm, attn_w), lambda i: (i, 0)),
            pl.BlockSpec((tm, ssm_w), lambda i: (i, 0)),
            pl.BlockSpec((tm, d), lambda i: (i, 0)),
            pl.BlockSpec((1, attn_w), lambda i: (0, 0)),
            pl.BlockSpec((attn_w + ssm_w, d), lambda i: (0, 0)),
        ],
        out_specs=pl.BlockSpec((tm, d), lambda i: (i, 0)),
        out_shape=jax.ShapeDtypeStruct((m, d), F32),
        compiler_params=_params(("parallel",)),
        name="outproj",
    )(attn, s_norm, x2, attn_g, w_out)


def _ffn_kernel(h_ref, g_ref, wg_ref, wu_ref, wd_ref, o_ref, hn_ref, *, rc):
    f = pl.program_id(1)

    @pl.when(f == 0)
    def _():
        hn_ref[...] = _rms(h_ref[...], g_ref[...]).astype(BF16)
        o_ref[...] = h_ref[...]

    for r in range(o_ref.shape[0] // rc):
        rows = slice(r * rc, (r + 1) * rc)
        hn = hn_ref[rows, :]
        gate = jnp.dot(hn, wg_ref[...], preferred_element_type=F32)
        up = jnp.dot(hn, wu_ref[...], preferred_element_type=F32)
        o_ref[rows, :] += jnp.dot((_silu(gate) * up).astype(BF16), wd_ref[...],
                                  preferred_element_type=F32)


def _ffn(h, g, w_gate, w_up, w_down, *, tm, tf=512, rc=512):
    m, d = h.shape
    dff = w_gate.shape[1]
    return pl.pallas_call(
        functools.partial(_ffn_kernel, rc=min(rc, tm)),
        grid=(m // tm, dff // tf),
        in_specs=[
            pl.BlockSpec((tm, d), lambda i, f: (i, 0)),
            pl.BlockSpec((1, d), lambda i, f: (0, 0)),
            pl.BlockSpec((d, tf), lambda i, f: (0, f)),
            pl.BlockSpec((d, tf), lambda i, f: (0, f)),
            pl.BlockSpec((tf, d), lambda i, f: (f, 0)),
        ],
        out_specs=pl.BlockSpec((tm, d), lambda i, f: (i, 0)),
        out_shape=jax.ShapeDtypeStruct((m, d), F32),
        scratch_shapes=[pltpu.VMEM((tm, d), BF16)],
        compiler_params=_params(("parallel", "arbitrary"), vmem=FFN_VMEM_LIMIT),
        name="ffn",
    )(h, g, w_gate, w_up, w_down)


def _sample_select_kernel(pt_ref, ps_ref, q_ref, sel_ref, blk_ref, *, nb, ppb, n_heads, t_len):
    b = pl.program_id(0)
    for n in range(nb):
        acc = ps_ref[pt_ref[b, n * ppb]]
        for pp in range(1, ppb):
            acc = acc + ps_ref[pt_ref[b, n * ppb + pp]]
        blk_ref[n] = acc / MOBA_BLOCK
    lane = lax.broadcasted_iota(jnp.int32, (SUBLANES, LANES), 1)
    for h in range(n_heads):
        qh = q_ref[:, h * LANES:(h + 1) * LANES]
        q8 = jnp.concatenate([qh, jnp.zeros((SUBLANES - t_len, LANES), F32)], axis=0)
        bm = blk_ref[:, h, :]
        gate = lax.dot_general(q8, bm, _NT, precision=lax.Precision.HIGHEST,
                               preferred_element_type=F32)
        lane_n = lax.broadcasted_iota(jnp.int32, gate.shape, 1).astype(F32)
        out = jnp.zeros((SUBLANES, LANES), jnp.int32)
        for kk in range(MOBA_TOP_K):
            mx = jnp.max(gate, axis=-1, keepdims=True)
            idx_f = jnp.min(jnp.where(gate == mx, lane_n, float(nb)), axis=-1, keepdims=True)
            idx = idx_f.astype(jnp.int32)
            ok = (mx > 0.5 * NEG).astype(jnp.int32)
            out = jnp.where(lane == kk, idx, out)
            out = jnp.where(lane == MOBA_TOP_K + kk, ok, out)
            gate = jnp.where(lane_n == idx_f, -jnp.inf, gate)
        sel_ref[h] = out


def _sample_select(page_table, page_sum, q_s, *, n_heads, page):
    db, t_len, w = q_s.shape
    n_pages = page_table.shape[1]
    n_phys, h, dh = page_sum.shape
    ppb = MOBA_BLOCK // page
    nb = n_pages // ppb
    grid_spec = pltpu.PrefetchScalarGridSpec(
        num_scalar_prefetch=1,
        grid=(db,),
        in_specs=[
            pl.BlockSpec((n_phys, h, dh), lambda bi, pt: (0, 0, 0), pipeline_mode=pl.Buffered(1)),
            pl.BlockSpec((None, t_len, w), lambda bi, pt: (bi, 0, 0)),
        ],
        out_specs=pl.BlockSpec((None, n_heads, SUBLANES, LANES), lambda bi, pt: (bi, 0, 0, 0)),
        scratch_shapes=[pltpu.VMEM((nb, h, dh), F32)],
    )
    return pl.pallas_call(
        functools.partial(_sample_select_kernel, nb=nb, ppb=ppb, n_heads=n_heads, t_len=t_len),
        grid_spec=grid_spec,
        out_shape=jax.ShapeDtypeStruct((db, n_heads, SUBLANES, LANES), jnp.int32),
        compiler_params=_params(("arbitrary",)),
        name="sample_select",
    )(page_table, page_sum, q_s)


def _sample_attend_kernel(pt_ref, idx_ref, ok_ref, q_ref, kn_ref, vn_ref, ck_hbm, cv_hbm, o_ref,
                          kbuf, vbuf, sem, *, t_len, page, ppb, past_len, scale):
    b = pl.program_id(0)
    h = pl.program_id(1)
    n_heads = pl.num_programs(1)
    step = b * n_heads + h
    cur = step % 2
    n_sel = t_len * MOBA_TOP_K * ppb
    n_pages = past_len // page

    def copies(st, hh, slot, phys):
        return (pltpu.make_async_copy(ck_hbm.at[phys, :, hh, :], kbuf.at[st, slot], sem.at[st, 0]),
                pltpu.make_async_copy(cv_hbm.at[phys, :, hh, :], vbuf.at[st, slot], sem.at[st, 1]))

    def slot_pages(bb, hh):
        out = []
        for t in range(t_len):
            for kk in range(MOBA_TOP_K):
                n = idx_ref[bb, (hh * t_len + t) * MOBA_TOP_K + kk]
                for pp in range(ppb):
                    lp = jnp.clip(n * ppb + pp, 0, n_pages - 1)
                    out.append(((t * MOBA_TOP_K + kk) * ppb + pp, pt_ref[bb, lp], n * ppb + pp))
        out.append((n_sel, pt_ref[bb, n_pages - 1], None))
        return out

    def start_all(st, bb, hh):
        for slot, phys, _ in slot_pages(bb, hh):
            for cp in copies(st, hh, slot, phys):
                cp.start()

    @pl.when(step == 0)
    def _():
        start_all(0, b, h)

    wrap = h + 1 == n_heads
    b_next = b + wrap.astype(jnp.int32)
    h_next = jnp.where(wrap, 0, h + 1)

    @pl.when(step + 1 < pl.num_programs(0) * n_heads)
    def _():
        start_all(1 - cur, b_next, h_next)

    pages = slot_pages(b, h)

    pad = jnp.zeros((page - t_len, LANES), F32)
    kbuf[cur, n_sel + 1] = jnp.concatenate([kn_ref[...], pad], axis=0)
    vbuf[cur, n_sel + 1] = jnp.concatenate([vn_ref[...], pad], axis=0)

    lane = lax.broadcasted_iota(jnp.int32, (1, page), 1)
    pos, okv = [], []
    for slot, _, lp in pages[:-1]:
        n_ok = ok_ref[b, (h * t_len + slot // (MOBA_TOP_K * ppb)) * MOBA_TOP_K + (slot // ppb) % MOBA_TOP_K]
        pos.append(lane + jnp.clip(lp, 0, n_pages - 1) * page)
        okv.append(jnp.broadcast_to(n_ok, (1, page)))
    pos.append(lane + (past_len - page))
    pos.append(lane + past_len)
    pos = jnp.concatenate(pos, axis=1)
    okv = jnp.concatenate(okv, axis=1)

    for slot, phys, _ in pages:
        for cp in copies(cur, h, slot, phys):
            cp.wait()

    n_keys = (n_sel + 2) * page
    kall = kbuf[cur].reshape(n_keys, LANES).astype(BF16)
    vall = vbuf[cur].reshape(n_keys, LANES).astype(BF16)
    q8 = jnp.concatenate([q_ref[...], jnp.zeros((SUBLANES - t_len, LANES), F32)], axis=0)
    s = lax.dot_general(q8.astype(BF16), kall, _NT, preferred_element_type=F32)

    slope = _alibi_slope(h)
    trow = lax.broadcasted_iota(jnp.int32, (SUBLANES, n_keys), 0)
    kcol = lax.broadcasted_iota(jnp.int32, (SUBLANES, n_keys), 1)
    q_pos = past_len + trow
    dist = (q_pos - pos).astype(F32)
    logits = s * scale - slope * dist
    per_t = MOBA_TOP_K * ppb * page
    sel_cols = n_sel * page
    sel_ok = (kcol // per_t == trow) & (jnp.concatenate(
        [okv, jnp.zeros((1, n_keys - sel_cols), jnp.int32)], axis=1) > 0)
    blk_start = (q_pos // MOBA_BLOCK) * MOBA_BLOCK
    own_ok = (kcol >= sel_cols) & (pos <= q_pos) & (pos >= blk_start) & (kcol < sel_cols + page + t_len)
    logits = jnp.where(sel_ok | own_ok, logits, NEG)
    m = jnp.max(logits, axis=-1, keepdims=True)
    p = jnp.exp(logits - m)
    denom = jnp.sum(p, axis=-1, keepdims=True)
    out = jnp.dot(p.astype(BF16), vall, preferred_element_type=F32) / denom
    o_ref[...] = out[:t_len, :]


def _sample_attend(page_table, idx, ok, q_s, k_s, v_s, cache_k, cache_v, *, n_heads):
    db, t_len, w = q_s.shape
    page = cache_k.shape[1]
    n_pages = page_table.shape[1]
    ppb = MOBA_BLOCK // page
    n_slots = t_len * MOBA_TOP_K * ppb + 2
    tok = pl.BlockSpec((None, t_len, LANES), lambda bi, hi, *_: (bi, 0, hi))
    grid_spec = pltpu.PrefetchScalarGridSpec(
        num_scalar_prefetch=3,
        grid=(db, n_heads),
        in_specs=[tok, tok, tok,
                  pl.BlockSpec(memory_space=pl.ANY), pl.BlockSpec(memory_space=pl.ANY)],
        out_specs=tok,
        scratch_shapes=[pltpu.VMEM((2, n_slots, page, LANES), F32),
                        pltpu.VMEM((2, n_slots, page, LANES), F32),
                        pltpu.SemaphoreType.DMA((2, 2))],
    )
    return pl.pallas_call(
        functools.partial(_sample_attend_kernel, t_len=t_len, page=page, ppb=ppb,
                          past_len=n_pages * page, scale=LANES ** -0.5),
        grid_spec=grid_spec,
        out_shape=jax.ShapeDtypeStruct((db, t_len, w), F32),
        compiler_params=_params(("arbitrary", "arbitrary")),
        name="sample_attend",
    )(page_table, idx, ok, q_s, k_s, v_s, cache_k, cache_v)


def _ssd_sample_kernel(xbc_ref, sc_ref, dtr_ref, z_ref, h0_ref, cw_ref, cb_ref, dtb_ref, alog_ref,
                       dskip_ref, sg_ref, expand_ref, s_ref, convs_ref, h_ref, xp_ref,
                       *, cw, t_len, ssm_w, groups, n_state):
    p_dim = SSM_HEAD_DIM
    heads = ssm_w // p_dim
    hpg = heads // groups
    gw = hpg * p_dim
    tail = cw - 1
    xp_ref[0:tail, :] = sc_ref[...]
    xp_ref[tail:tail + t_len, :] = xbc_ref[...]
    conv = cb_ref[...]
    for i in range(cw):
        conv = conv + xp_ref[i:i + t_len, :] * cw_ref[i:i + 1, :]
    act = _silu(conv)
    convs_ref[...] = xp_ref[t_len:t_len + tail, :]

    dt = _softplus(dtr_ref[...] + dtb_ref[...])
    da = dt * (-jnp.exp(alog_ref[...]))
    cum = [da[0:1]]
    for t in range(1, t_len):
        cum.append(cum[-1] + da[t:t + 1])
    lane = lax.broadcasted_iota(jnp.int32, (1, LANES), 1)
    xs = act[:, :ssm_w]
    zpad = jnp.zeros((SUBLANES - t_len, n_state), F32)

    b8, c8, cbs = [], [], []
    for g in range(groups):
        bg = act[:, ssm_w + g * n_state:ssm_w + (g + 1) * n_state]
        cg = act[:, ssm_w + (groups + g) * n_state:ssm_w + (groups + g + 1) * n_state]
        b8.append(jnp.concatenate([bg, zpad], axis=0).astype(BF16))
        c8.append(jnp.concatenate([cg, zpad], axis=0).astype(BF16))
        cbs.append(lax.dot_general(c8[g], b8[g], _NT, preferred_element_type=F32))

    pairs = [(l, s) for l in range(t_len) for s in range(l + 1)]
    rows = []
    for l, s in pairs:
        cb_ls = jnp.zeros((1, LANES), F32)
        for g in range(groups):
            in_g = (lane >= g * hpg) & (lane < (g + 1) * hpg)
            cb_ls = jnp.where(in_g, cbs[g][l:l + 1, s:s + 1], cb_ls)
        rows.append(cb_ls * jnp.exp(cum[l] - cum[s]) * dt[s:s + 1])
    rows += [jnp.exp(cum[l]) for l in range(t_len)]
    rows += [jnp.exp(cum[-1] - cum[s]) * dt[s:s + 1] for s in range(t_len)]
    n_rows = -(-len(rows) // SUBLANES) * SUBLANES
    rows.append(jnp.zeros((n_rows - len(rows), LANES), F32))
    rx = jnp.dot(jnp.concatenate(rows, axis=0), expand_ref[...], precision=lax.Precision.HIGHEST,
                 preferred_element_type=F32)
    r_ecum = len(pairs)
    r_coef = r_ecum + t_len

    y_diag = []
    for l in range(t_len):
        acc = jnp.zeros((1, ssm_w), F32)
        for s in range(l + 1):
            i = pairs.index((l, s))
            acc = acc + rx[i:i + 1, :] * xs[s:s + 1, :]
        y_diag.append(acc)
    y = dskip_ref[...] * xs + jnp.concatenate(y_diag, axis=0)

    y_off = []
    for g in range(groups):
        hg = h0_ref[g * hpg:(g + 1) * hpg].reshape(gw, n_state)
        y_off.append(lax.dot_general(c8[g], hg.astype(BF16), _NT, preferred_element_type=F32)[:t_len])
    y = y + jnp.concatenate(y_off, axis=1) * rx[r_ecum:r_ecum + t_len, :]
    for g in range(groups):
        gl = slice(g * gw, (g + 1) * gw)
        s_ref[:, gl] = _gated_norm(y[:, gl], z_ref[:, gl], sg_ref[:, gl])

    e_last = jnp.exp(cum[-1])
    xw = xs * rx[r_coef:r_coef + t_len, :]
    for g in range(groups):
        xw8 = jnp.concatenate([xw[:, g * gw:(g + 1) * gw], jnp.zeros((SUBLANES - t_len, gw), F32)], axis=0)
        st = lax.dot_general(xw8.astype(BF16), b8[g], _TN, preferred_element_type=F32)
        for ee in range(hpg):
            e = g * hpg + ee
            h_ref[e] = h0_ref[e] * e_last[:, e:e + 1] + st[ee * p_dim:(ee + 1) * p_dim, :]


def _ssd_sample(xbc, state_conv, dtr, z, state_ssm, conv_w, conv_b, dtb, alog, dskip_x, ssm_g, expand,
                *, ssm_w, groups, n_state):
    db, t_len, cc = xbc.shape
    cw = conv_w.shape[0]
    heads = ssm_w // SSM_HEAD_DIM
    const = lambda shape: pl.BlockSpec(shape, lambda bi: (0,) * len(shape))
    return pl.pallas_call(
        functools.partial(_ssd_sample_kernel, cw=cw, t_len=t_len, ssm_w=ssm_w, groups=groups,
                          n_state=n_state),
        grid=(db,),
        in_specs=[
            pl.BlockSpec((None, t_len, cc), lambda bi: (bi, 0, 0)),
            pl.BlockSpec((None, cw - 1, cc), lambda bi: (bi, 0, 0)),
            pl.BlockSpec((None, t_len, LANES), lambda bi: (bi, 0, 0)),
            pl.BlockSpec((None, t_len, ssm_w), lambda bi: (bi, 0, 0)),
            pl.BlockSpec((None, heads, SSM_HEAD_DIM, n_state), lambda bi: (bi, 0, 0, 0)),
            const((cw, cc)), const((1, cc)), const((1, LANES)), const((1, LANES)), const((1, ssm_w)),
            const((1, ssm_w)), const((LANES, ssm_w)),
        ],
        out_specs=(
            pl.BlockSpec((None, t_len, ssm_w), lambda bi: (bi, 0, 0)),
            pl.BlockSpec((None, cw - 1, cc), lambda bi: (bi, 0, 0)),
            pl.BlockSpec((None, heads, SSM_HEAD_DIM, n_state), lambda bi: (bi, 0, 0, 0)),
        ),
        out_shape=(
            jax.ShapeDtypeStruct((db, t_len, ssm_w), BF16),
            jax.ShapeDtypeStruct((db, cw - 1, cc), F32),
            jax.ShapeDtypeStruct((db, heads, SSM_HEAD_DIM, n_state), F32),
        ),
        scratch_shapes=[pltpu.VMEM((2 * SUBLANES, cc), F32)],
        compiler_params=_params(("parallel",)),
        name="ssd_sample",
    )(xbc, state_conv, dtr, z, state_ssm, conv_w, conv_b, dtb, alog, dskip_x, ssm_g, expand)


def _row_tile(m, target):
    tm = min(m, target)
    assert m % tm == 0, (m, tm)
    return tm


def _page_split(n_phys, moba_steps, nb, ssd_steps):
    per_moba = max(1, n_phys * 7 // 10 // moba_steps)
    rest = max(n_phys - moba_steps * per_moba, 0)
    per_ssd = max(1, -(-rest // ssd_steps))
    return per_moba, per_ssd


def kernel(x_prompt, x_sample, cache_k, cache_v, page_table, state_conv, state_ssm, norm_mix_g, w_in, q_norm_g, k_norm_g, conv_w, conv_b, dt_bias, a_log, d_skip, attn_out_g, ssm_out_g, w_out, norm_ffn_g, w_gate, w_up, w_down):
    b, s, d = x_prompt.shape
    db, t_len, _ = x_sample.shape
    n_phys, page, n_heads, dh = cache_k.shape
    attn_w = n_heads * dh
    heads, p_dim, n_state = state_ssm.shape[1:]
    assert p_dim == SSM_HEAD_DIM and dh == LANES and heads <= LANES
    ssm_w = heads * p_dim
    conv_ch = state_conv.shape[2]
    groups = (conv_ch - ssm_w) // (2 * n_state)
    n_main = 3 * attn_w + ssm_w + conv_ch

    w_dt = jnp.pad(w_in[:, n_main:].astype(BF16), ((0, 0), (0, LANES - heads)))
    w_out_b, w_gate_b, w_up_b, w_down_b = (w.astype(BF16) for w in (w_out, w_gate, w_up, w_down))
    row = lambda v: v.reshape(1, -1).astype(F32)
    padl = lambda v: jnp.pad(v.astype(F32), (0, LANES - v.shape[0])).reshape(1, LANES)
    dskip_x = jnp.repeat(d_skip.astype(F32), p_dim).reshape(1, ssm_w)
    expand = (jnp.arange(LANES)[:, None] == (jnp.arange(ssm_w)[None, :] // p_dim)).astype(F32)
    in_args = (row(norm_mix_g), w_in, w_dt, row(q_norm_g), row(k_norm_g))
    in_kw = dict(attn_w=attn_w, ssm_w=ssm_w, conv_ch=conv_ch)
    ssd_kw = dict(ssm_w=ssm_w, groups=groups, n_state=n_state)
    ssd_par = (conv_w, row(conv_b), padl(dt_bias), padl(a_log), dskip_x)

    xp2 = x_prompt.reshape(b * s, d)
    q, k, v, z, xbc, dtr = _inproj(xp2, *in_args, tm=_row_tile(b * s, 1024), **in_kw)
    moba_steps, ssd_steps = b * n_heads, b * (s // SSD_CHUNK)
    pg_moba, pg_ssd = _page_split(n_phys, moba_steps, s // MOBA_BLOCK, ssd_steps)
    ps_ssd0 = n_phys - ssd_steps * pg_ssd
    assert 0 <= ps_ssd0 <= moba_steps * pg_moba <= n_phys
    attn, ps_a = _moba_prompt(q.reshape(b, s, attn_w), k.reshape(b, s, attn_w), v.reshape(b, s, attn_w),
                              cache_k, n_heads=n_heads, pages_per_step=pg_moba)
    sn, conv_p, ssm_p, ps_b = _ssd_prompt(xbc.reshape(b, s, conv_ch), dtr.reshape(b, s, LANES),
                                          z.reshape(b, s, ssm_w), *ssd_par, row(ssm_out_g), expand, cache_k,
                                          page0=ps_ssd0, pages_per_step=pg_ssd, **ssd_kw)
    page_sum = jnp.concatenate([ps_a[:ps_ssd0], ps_b], axis=0)
    h_p = _outproj(attn.reshape(b * s, attn_w), sn.reshape(b * s, ssm_w), xp2, row(attn_out_g), w_out_b,
                   tm=_row_tile(b * s, 512))
    y_p = _ffn(h_p, row(norm_ffn_g), w_gate_b, w_up_b, w_down_b,
               tm=_row_tile(b * s, 1024)).reshape(b, s, d)
    page_shape = (b, s // page, page, n_heads, dh)
    k_p, v_p = k.reshape(page_shape), v.reshape(page_shape)

    ms = db * t_len
    xs2 = x_sample.reshape(ms, d)
    qs, k_s, v_s, zs, xbcs, dtrs = _inproj(xs2, *in_args, tm=_row_tile(ms, 1024), **in_kw)
    qs3, ks3, vs3 = (a.reshape(db, t_len, attn_w) for a in (qs, k_s, v_s))
    sel = _sample_select(page_table, page_sum, qs3, n_heads=n_heads, page=page)
    idx = sel[:, :, :t_len, :MOBA_TOP_K].reshape(db, n_heads * t_len * MOBA_TOP_K)
    ok = sel[:, :, :t_len, MOBA_TOP_K:2 * MOBA_TOP_K].reshape(db, n_heads * t_len * MOBA_TOP_K)
    attn_s = _sample_attend(page_table, idx, ok, qs3, ks3, vs3, cache_k, cache_v, n_heads=n_heads)
    sns, conv_s, ssm_s = _ssd_sample(xbcs.reshape(db, t_len, conv_ch), state_conv,
                                     dtrs.reshape(db, t_len, LANES), zs.reshape(db, t_len, ssm_w), state_ssm,
                                     *ssd_par, row(ssm_out_g), expand, **ssd_kw)
    tms = _row_tile(ms, 512)
    h_s = _outproj(attn_s.reshape(ms, attn_w), sns.reshape(ms, ssm_w), xs2, row(attn_out_g), w_out_b, tm=tms)
    y_s = _ffn(h_s, row(norm_ffn_g), w_gate_b, w_up_b, w_down_b, tm=tms).reshape(db, t_len, d)
    tok_shape = (db, t_len, n_heads, dh)
    return (y_p, y_s, k_p, v_p, conv_p, ssm_p, k_s.reshape(tok_shape), v_s.reshape(tok_shape),
            conv_s, ssm_s)
```

```python
import functools

import jax
import jax.numpy as jnp
from jax import lax
from jax.experimental import pallas as pl
from jax.experimental.pallas import tpu as pltpu

F32 = jnp.float32
BF16 = jnp.bfloat16
EPS = 1e-6
NEG = -1e30

MOBA_BLOCK = 256
MOBA_TOP_K = 3
SSD_CHUNK = 128
SSM_HEAD_DIM = 64
LANES = 128
SUBLANES = 8
VMEM_LIMIT = 56 * 1024 * 1024
BIG_VMEM_LIMIT = 60 * 1024 * 1024

_NT = (((1,), (1,)), ((), ()))
_TN = (((0,), (0,)), ((), ()))


def _params(sem, vmem=VMEM_LIMIT):
    return pltpu.CompilerParams(dimension_semantics=sem, vmem_limit_bytes=vmem)


def _rms(x, g):
    return x * lax.rsqrt(jnp.mean(x * x, axis=-1, keepdims=True) + EPS) * g


def _silu(x):
    return x * (1.0 / (1.0 + jnp.exp(-x)))


def _softplus(x):
    return jnp.maximum(x, 0.0) + jnp.log1p(jnp.exp(-jnp.abs(x)))


def _alibi_slope(h):
    bits = (jnp.full((1, 1), 126, jnp.int32) - h) << 23
    return lax.bitcast_convert_type(bits, F32)


def _inproj_kernel(x_ref, g_ref, w_ref, wdt_ref, qg_ref, kg_ref,
                   q_ref, k_ref, v_ref, z_ref, xbc_ref, dt_ref, xn_ref, *, tn, seg):
    j = pl.program_id(1)

    @pl.when(j == 0)
    def _():
        xn_ref[...] = _rms(x_ref[...], g_ref[...]).astype(BF16)
        dt_ref[...] = jnp.dot(xn_ref[...], wdt_ref[...], preferred_element_type=F32)

    acc = jnp.dot(xn_ref[...], w_ref[...].astype(BF16), preferred_element_type=F32)

    def headnorm(out_ref, g):
        for hh in range(tn // LANES):
            sl = slice(hh * LANES, (hh + 1) * LANES)
            out_ref[:, sl] = _rms(acc[:, sl], g)

    @pl.when(j < seg[0])
    def _():
        headnorm(q_ref, qg_ref[...])

    @pl.when((j >= seg[0]) & (j < seg[1]))
    def _():
        headnorm(k_ref, kg_ref[...])

    @pl.when((j >= seg[1]) & (j < seg[2]))
    def _():
        v_ref[...] = acc

    @pl.when((j >= seg[2]) & (j < seg[3]))
    def _():
        z_ref[...] = acc

    @pl.when(j >= seg[3])
    def _():
        xbc_ref[...] = acc


def _inproj(x2, norm_g, w_all, w_dt, q_g, k_g, *, attn_w, ssm_w, conv_ch, tm, tn=512):
    m, d = x2.shape
    n_main = 3 * attn_w + ssm_w + conv_ch
    assert n_main % tn == 0
    seg = (attn_w // tn, 2 * attn_w // tn, 3 * attn_w // tn, (3 * attn_w + ssm_w) // tn)
    nj = n_main // tn

    def col(lo, hi):
        return lambda i, j: (i, jnp.clip(j - lo, 0, hi - lo - 1))

    out_shape = (
        jax.ShapeDtypeStruct((m, attn_w), F32), jax.ShapeDtypeStruct((m, attn_w), F32),
        jax.ShapeDtypeStruct((m, attn_w), F32), jax.ShapeDtypeStruct((m, ssm_w), F32),
        jax.ShapeDtypeStruct((m, conv_ch), F32), jax.ShapeDtypeStruct((m, LANES), F32))
    return pl.pallas_call(
        functools.partial(_inproj_kernel, tn=tn, seg=seg),
        grid=(m // tm, nj),
        in_specs=[
            pl.BlockSpec((tm, d), lambda i, j: (i, 0)),
            pl.BlockSpec((1, d), lambda i, j: (0, 0)),
            pl.BlockSpec((d, tn), lambda i, j: (0, j)),
            pl.BlockSpec((d, LANES), lambda i, j: (0, 0)),
            pl.BlockSpec((1, LANES), lambda i, j: (0, 0)),
            pl.BlockSpec((1, LANES), lambda i, j: (0, 0)),
        ],
        out_specs=(
            pl.BlockSpec((tm, tn), col(0, seg[0])),
            pl.BlockSpec((tm, tn), col(seg[0], seg[1])),
            pl.BlockSpec((tm, tn), col(seg[1], seg[2])),
            pl.BlockSpec((tm, tn), col(seg[2], seg[3])),
            pl.BlockSpec((tm, tn), col(seg[3], nj)),
            pl.BlockSpec((tm, LANES), lambda i, j: (i, 0)),
        ),
        out_shape=out_shape,
        scratch_shapes=[pltpu.VMEM((tm, d), BF16)],
        compiler_params=_params(("parallel", "arbitrary")),
        name="inproj",
    )(x2, norm_g, w_all, w_dt, q_g, k_g)


def _block_select(gate, n_past):
    lane = lax.broadcasted_iota(jnp.int32, gate.shape, 1)
    valid = lane < n_past
    if n_past <= MOBA_TOP_K:
        return valid & (gate > 0.5 * NEG)
    gm = jnp.where(valid, gate, -jnp.inf)
    rank = jnp.zeros(gate.shape, jnp.int32)
    for d in range(1, n_past):
        lower = pltpu.roll(gm, d, axis=1)
        upper = pltpu.roll(gm, LANES - d, axis=1)
        rank = rank + (lower >= gm).astype(jnp.int32) + (upper > gm).astype(jnp.int32)
    return valid & (rank < MOBA_TOP_K) & (gate > 0.5 * NEG)


def _page_chunk_copy(ck_hbm, buf, sem, first_page, n_pages):
    return pltpu.make_async_copy(ck_hbm.at[pl.ds(first_page, n_pages)], buf, sem)


def _page_key_sums(pages, lanes_of_adds=4):
    n, page, heads, dh = pages.shape
    part = pages.reshape(n * lanes_of_adds, page // lanes_of_adds, heads, dh).sum(axis=1)
    return part.reshape(n, lanes_of_adds, heads, dh).sum(axis=1)


def _stream_page_sums(ck_hbm, ps_ref, pbuf, psem, step, n_steps, page0):
    n_pg = pbuf.shape[1]
    slot = step % 2

    def page_copy(st, sl):
        return _page_chunk_copy(ck_hbm, pbuf.at[sl], psem.at[sl], page0 + st * n_pg, n_pg)

    @pl.when(step == 0)
    def _():
        page_copy(0, 0).start()

    page_copy(step, slot).wait()

    @pl.when(step + 1 < n_steps)
    def _():
        page_copy(step + 1, 1 - slot).start()

    ps_ref[...] = _page_key_sums(pbuf[slot])


def _moba_prompt_kernel(q_ref, k_ref, v_ref, ck_hbm, o_ref, ps_ref, pbuf, psem, *, nb, scale):
    blk = MOBA_BLOCK
    h = pl.program_id(1)
    _stream_page_sums(ck_hbm, ps_ref, pbuf, psem, pl.program_id(0) * pl.num_programs(1) + h,
                      pl.num_programs(0) * pl.num_programs(1), 0)
    slope = _alibi_slope(h)
    k = k_ref[...]
    kb = k.astype(BF16)
    vb = v_ref[...].astype(BF16)
    means = [jnp.mean(k[j * blk:(j + 1) * blk], axis=0, keepdims=True) for j in range(nb)]
    means = jnp.concatenate(means + [jnp.zeros((LANES - nb, LANES), F32)], axis=0)
    for i in range(nb):
        qi = q_ref[i * blk:(i + 1) * blk, :]
        nk = (i + 1) * blk
        s = lax.dot_general(qi.astype(BF16), kb[:nk], _NT, preferred_element_type=F32)
        col = lax.broadcasted_iota(jnp.int32, (blk, nk), 1)
        logits = s * scale + slope * col.astype(F32)
        row = lax.broadcasted_iota(jnp.int32, (blk, blk), 0)
        colb = lax.broadcasted_iota(jnp.int32, (blk, blk), 1)
        pieces = []
        if i > 0:
            gate = lax.dot_general(qi, means, _NT, precision=lax.Precision.HIGHEST,
                                   preferred_element_type=F32)
            sel = _block_select(gate, i).astype(F32)
            pieces = [jnp.broadcast_to(sel[:, j:j + 1], (blk, blk)) for j in range(i)]
        pieces.append((colb <= row).astype(F32))
        allowed = jnp.concatenate(pieces, axis=1) if len(pieces) > 1 else pieces[0]
        logits = jnp.where(allowed > 0.5, logits, NEG)
        m = jnp.max(logits, axis=-1, keepdims=True)
        p = jnp.exp(logits - m)
        denom = jnp.sum(p, axis=-1, keepdims=True)
        out = jnp.dot(p.astype(BF16), vb[:nk], preferred_element_type=F32)
        o_ref[i * blk:(i + 1) * blk, :] = out / denom


def _moba_prompt(q, k, v, cache_k, *, n_heads, pages_per_step):
    b, s, w = q.shape
    dh = w // n_heads
    nb = s // MOBA_BLOCK
    page, ch, cdh = cache_k.shape[1:]
    spec = pl.BlockSpec((None, s, dh), lambda bi, hi: (bi, 0, hi))
    return pl.pallas_call(
        functools.partial(_moba_prompt_kernel, nb=nb, scale=dh ** -0.5),
        grid=(b, n_heads),
        in_specs=[spec, spec, spec, pl.BlockSpec(memory_space=pl.ANY)],
        out_specs=(spec, pl.BlockSpec((pages_per_step, ch, cdh), lambda bi, hi: (bi * n_heads + hi, 0, 0))),
        out_shape=(jax.ShapeDtypeStruct((b, s, w), F32),
                   jax.ShapeDtypeStruct((b * n_heads * pages_per_step, ch, cdh), F32)),
        scratch_shapes=[pltpu.VMEM((2, pages_per_step, page, ch, cdh), F32),
                        pltpu.SemaphoreType.DMA((2,))],
        compiler_params=_params(("arbitrary", "arbitrary"), vmem=BIG_VMEM_LIMIT),
        name="moba_prompt",
    )(q, k, v, cache_k)


def _gated_norm(y, z, g):
    return _rms(y * _silu(z), g).astype(BF16)


def _ssd_prompt_kernel(xbc_ref, dtr_ref, z_ref, cw_ref, cb_ref, dtb_ref, alog_ref, dskip_ref, sg_ref,
                       expand_ref, ck_hbm, s_ref, convp_ref, h_ref, ps_ref, xp_ref, pbuf, psem,
                       *, cw, ssm_w, groups, n_state, page0):
    c = pl.program_id(1)
    nc = pl.num_programs(1)
    _stream_page_sums(ck_hbm, ps_ref, pbuf, psem, pl.program_id(0) * nc + c, pl.num_programs(0) * nc,
                      page0)
    lc = SSD_CHUNK
    p_dim = SSM_HEAD_DIM
    heads = ssm_w // p_dim
    hpg = heads // groups
    gw = hpg * p_dim
    tail = cw - 1

    @pl.when(c == 0)
    def _():
        xp_ref[0:SUBLANES, :] = jnp.zeros((SUBLANES, xp_ref.shape[1]), F32)
        h_ref[...] = jnp.zeros(h_ref.shape, F32)

    xc = xbc_ref[...]
    xp_ref[SUBLANES:SUBLANES + lc, :] = xc
    conv = cb_ref[...]
    for i in range(cw):
        conv = conv + xp_ref[SUBLANES - tail + i:SUBLANES - tail + i + lc, :] * cw_ref[i:i + 1, :]
    act = _silu(conv)
    xp_ref[SUBLANES - tail:SUBLANES, :] = xc[lc - tail:lc, :]

    @pl.when(c == nc - 1)
    def _():
        convp_ref[...] = xc[lc - tail:lc, :]

    dt = _softplus(dtr_ref[...] + dtb_ref[...])
    a = -jnp.exp(alog_ref[...])
    row = lax.broadcasted_iota(jnp.int32, (lc, lc), 0)
    colm = lax.broadcasted_iota(jnp.int32, (lc, lc), 1)
    causal = row >= colm
    tri = causal.astype(F32)
    cum = jnp.dot(tri, dt * a, precision=lax.Precision.HIGHEST, preferred_element_type=F32)
    cum_t = cum.T
    dt_t = dt.T
    cum_last = cum[lc - 1:lc, :]
    dec_end = jnp.exp(cum_last - cum) * dt
    ecum = jnp.exp(cum)
    expand = expand_ref[...]
    ecum_x = jnp.dot(ecum, expand, precision=lax.Precision.HIGHEST, preferred_element_type=F32)
    dec_end_x = jnp.dot(dec_end, expand, precision=lax.Precision.HIGHEST,
                        preferred_element_type=F32)

    xs = act[:, :ssm_w]
    xs_b = xs.astype(BF16)
    y = dskip_ref[...] * xs
    lane = lax.broadcasted_iota(jnp.int32, (lc, 2 * p_dim), 1)
    for g in range(groups):
        bg = act[:, ssm_w + g * n_state:ssm_w + (g + 1) * n_state].astype(BF16)
        cg = act[:, ssm_w + (groups + g) * n_state:ssm_w + (groups + g + 1) * n_state].astype(BF16)
        cb = lax.dot_general(cg, bg, _NT, preferred_element_type=F32)
        hg = h_ref[g * hpg:(g + 1) * hpg].reshape(gw, n_state)
        y_off = lax.dot_general(cg, hg.astype(BF16), _NT, preferred_element_type=F32)
        y_off = y_off * ecum_x[:, g * gw:(g + 1) * gw]
        y_diag = []
        for pr in range(hpg // 2):
            ws = []
            for e in (g * hpg + 2 * pr, g * hpg + 2 * pr + 1):
                diff = cum[:, e:e + 1] - cum_t[e:e + 1, :]
                decay = jnp.where(causal, jnp.exp(diff), 0.0)
                ws.append((cb * decay * dt_t[e:e + 1, :]).astype(BF16))
            lo = (g * hpg + 2 * pr) * p_dim
            xpair = xs_b[:, lo:lo + 2 * p_dim]
            zero = jnp.zeros_like(xpair)
            rhs = jnp.concatenate([jnp.where(lane < p_dim, xpair, zero),
                                   jnp.where(lane >= p_dim, xpair, zero)], axis=0)
            y_diag.append(jnp.dot(jnp.concatenate(ws, axis=1), rhs, preferred_element_type=F32))
        y_g = jnp.concatenate(y_diag, axis=1) + y_off
        xw = (xs[:, g * gw:(g + 1) * gw] * dec_end_x[:, g * gw:(g + 1) * gw]).astype(BF16)
        st = lax.dot_general(xw, bg, _TN, preferred_element_type=F32)
        for ee in range(hpg):
            e = g * hpg + ee
            cd = jnp.exp(cum_t[e:e + 1, lc - 1:lc])
            h_ref[e] = h_ref[e] * cd + st[ee * p_dim:(ee + 1) * p_dim, :]
        gl = slice(g * gw, (g + 1) * gw)
        s_ref[:, gl] = _gated_norm(y[:, gl] + y_g, z_ref[:, gl], sg_ref[:, gl])


def _ssd_prompt(xbc, dtr, z, conv_w, conv_b, dtb, alog, dskip_x, ssm_g, expand, cache_k,
                *, ssm_w, groups, n_state, page0, pages_per_step):
    b, s, cc = xbc.shape
    cw = conv_w.shape[0]
    heads = ssm_w // SSM_HEAD_DIM
    lc = SSD_CHUNK
    nc = s // lc
    page, ch, cdh = cache_k.shape[1:]
    const = lambda shape: pl.BlockSpec(shape, lambda bi, ci: (0,) * len(shape))
    return pl.pallas_call(
        functools.partial(_ssd_prompt_kernel, cw=cw, ssm_w=ssm_w, groups=groups, n_state=n_state,
                          page0=page0),
        grid=(b, nc),
        in_specs=[
            pl.BlockSpec((None, lc, cc), lambda bi, ci: (bi, ci, 0)),
            pl.BlockSpec((None, lc, LANES), lambda bi, ci: (bi, ci, 0)),
            pl.BlockSpec((None, lc, ssm_w), lambda bi, ci: (bi, ci, 0)),
            const((cw, cc)), const((1, cc)), const((1, LANES)), const((1, LANES)),
            const((1, ssm_w)), const((1, ssm_w)), const((LANES, ssm_w)),
            pl.BlockSpec(memory_space=pl.ANY),
        ],
        out_specs=(
            pl.BlockSpec((None, lc, ssm_w), lambda bi, ci: (bi, ci, 0)),
            pl.BlockSpec((None, cw - 1, cc), lambda bi, ci: (bi, 0, 0)),
            pl.BlockSpec((None, heads, SSM_HEAD_DIM, n_state), lambda bi, ci: (bi, 0, 0, 0)),
            pl.BlockSpec((pages_per_step, ch, cdh), lambda bi, ci: (bi * nc + ci, 0, 0)),
        ),
        out_shape=(
            jax.ShapeDtypeStruct((b, s, ssm_w), BF16),
            jax.ShapeDtypeStruct((b, cw - 1, cc), F32),
            jax.ShapeDtypeStruct((b, heads, SSM_HEAD_DIM, n_state), F32),
            jax.ShapeDtypeStruct((b * nc * pages_per_step, ch, cdh), F32),
        ),
        scratch_shapes=[pltpu.VMEM((SUBLANES + lc, cc), F32),
                        pltpu.VMEM((2, pages_per_step, page, ch, cdh), F32),
                        pltpu.SemaphoreType.DMA((2,))],
        compiler_params=_params(("arbitrary", "arbitrary")),
        name="ssd_prompt",
    )(xbc, dtr, z, conv_w, conv_b, dtb, alog, dskip_x, ssm_g, expand, cache_k)


def _outproj_kernel(attn_ref, s_ref, x_ref, ag_ref, w_ref, o_ref, *, rc):
    for r in range(o_ref.shape[0] // rc):
        rows = slice(r * rc, (r + 1) * rc)
        a = _rms(attn_ref[rows, :], ag_ref[...]).astype(BF16)
        mix = jnp.concatenate([a, s_ref[rows, :]], axis=1)
        o_ref[rows, :] = x_ref[rows, :] + jnp.dot(mix, w_ref[...], preferred_element_type=F32)


def _outproj(attn, s_norm, x2, attn_g, w_out, *, tm, rc=256):
    m, attn_w = attn.shape
    ssm_w = s_norm.shape[1]
    d = x2.shape[1]
    return pl.pallas_call(
        functools.partial(_outproj_kernel, rc=min(rc, tm)),
        grid=(m // tm,),
        in_specs=[
            pl.BlockSpec((tm, attn_w), lambda i: (i, 0)),
            pl.BlockSpec((tm, ssm_w), lambda i: (i, 0)),
            pl.BlockSpec((tm, d), lambda i: (i, 0)),
            pl.BlockSpec((1, attn_w), lambda i: (0, 0)),
            pl.BlockSpec((attn_w + ssm_w, d), lambda i: (0, 0)),
        ],
        out_specs=pl.BlockSpec((tm, d), lambda i: (i, 0)),
        out_shape=jax.ShapeDtypeStruct((m, d), F32),
        compiler_params=_params(("parallel",)),
        name="outproj",
    )(attn, s_norm, x2, attn_g, w_out)


def _ffn_kernel(h_ref, g_ref, wg_ref, wu_ref, wd_ref, o_ref, hn_ref, *, rc):
    f = pl.program_id(1)

    @pl.when(f == 0)
    def _():
        hn_ref[...] = _rms(h_ref[...], g_ref[...]).astype(BF16)
        o_ref[...] = h_ref[...]

    for r in range(o_ref.shape[0] // rc):
        rows = slice(r * rc, (r + 1) * rc)
        hn = hn_ref[rows, :]
        gate = jnp.dot(hn, wg_ref[...], preferred_element_type=F32)
        up = jnp.dot(hn, wu_ref[...], preferred_element_type=F32)
        o_ref[rows, :] += jnp.dot((_silu(gate) * up).astype(BF16), wd_ref[...],
                                  preferred_element_type=F32)


def _ffn(h, g, w_gate, w_up, w_down, *, tm, tf=512, rc=512):
    m, d = h.shape
    dff = w_gate.shape[1]
    return pl.pallas_call(
        functools.partial(_ffn_kernel, rc=min(rc, tm)),
        grid=(m // tm, dff // tf),
        in_specs=[
            pl.BlockSpec((tm, d), lambda i, f: (i, 0)),
            pl.BlockSpec((1, d), lambda i, f: (0, 0)),
            pl.BlockSpec((d, tf), lambda i, f: (0, f)),
            pl.BlockSpec((d, tf), lambda i, f: (0, f)),
            pl.BlockSpec((tf, d), lambda i, f: (f, 0)),
        ],
        out_specs=pl.BlockSpec((tm, d), lambda i, f: (i, 0)),
        out_shape=jax.ShapeDtypeStruct((m, d), F32),
        scratch_shapes=[pltpu.VMEM((tm, d), BF16)],
        compiler_params=_params(("parallel", "arbitrary"), vmem=BIG_VMEM_LIMIT),
        name="ffn",
    )(h, g, w_gate, w_up, w_down)


def _sample_select_kernel(pt_ref, psa_ref, psb_ref, q_ref, sel_ref, blk_ref,
                          *, nb, ppb, n_heads, t_len, split):
    b = pl.program_id(0)

    def page_sum(pg):
        lo = psa_ref[jnp.minimum(pg, split - 1)]
        hi = psb_ref[jnp.maximum(pg - split, 0)]
        return jnp.where(pg < split, lo, hi)

    for n in range(nb):
        acc = page_sum(pt_ref[b, n * ppb])
        for pp in range(1, ppb):
            acc = acc + page_sum(pt_ref[b, n * ppb + pp])
        blk_ref[n] = acc / MOBA_BLOCK
    lane = lax.broadcasted_iota(jnp.int32, (SUBLANES, LANES), 1)
    for h in range(n_heads):
        qh = q_ref[:, h * LANES:(h + 1) * LANES]
        q8 = jnp.concatenate([qh, jnp.zeros((SUBLANES - t_len, LANES), F32)], axis=0)
        bm = blk_ref[:, h, :]
        gate = lax.dot_general(q8, bm, _NT, precision=lax.Precision.HIGHEST,
                               preferred_element_type=F32)
        lane_n = lax.broadcasted_iota(jnp.int32, gate.shape, 1).astype(F32)
        out = jnp.zeros((SUBLANES, LANES), jnp.int32)
        for kk in range(MOBA_TOP_K):
            mx = jnp.max(gate, axis=-1, keepdims=True)
            idx_f = jnp.min(jnp.where(gate == mx, lane_n, float(nb)), axis=-1, keepdims=True)
            idx = idx_f.astype(jnp.int32)
            ok = (mx > 0.5 * NEG).astype(jnp.int32)
            out = jnp.where(lane == kk, idx, out)
            out = jnp.where(lane == MOBA_TOP_K + kk, ok, out)
            gate = jnp.where(lane_n == idx_f, -jnp.inf, gate)
        sel_ref[h] = out


def _sample_select(page_table, ps_lo, ps_hi, q_s, *, n_heads, page, split):
    db, t_len, w = q_s.shape
    n_pages = page_table.shape[1]
    _, h, dh = ps_lo.shape
    ppb = MOBA_BLOCK // page
    nb = n_pages // ppb
    whole = lambda a: pl.BlockSpec(a.shape, lambda bi, pt: (0, 0, 0), pipeline_mode=pl.Buffered(1))
    grid_spec = pltpu.PrefetchScalarGridSpec(
        num_scalar_prefetch=1,
        grid=(db,),
        in_specs=[whole(ps_lo), whole(ps_hi),
                  pl.BlockSpec((None, t_len, w), lambda bi, pt: (bi, 0, 0))],
        out_specs=pl.BlockSpec((None, n_heads, SUBLANES, LANES), lambda bi, pt: (bi, 0, 0, 0)),
        scratch_shapes=[pltpu.VMEM((nb, h, dh), F32)],
    )
    return pl.pallas_call(
        functools.partial(_sample_select_kernel, nb=nb, ppb=ppb, n_heads=n_heads, t_len=t_len,
                          split=split),
        grid_spec=grid_spec,
        out_shape=jax.ShapeDtypeStruct((db, n_heads, SUBLANES, LANES), jnp.int32),
        compiler_params=_params(("arbitrary",)),
        name="sample_select",
    )(page_table, ps_lo, ps_hi, q_s)


def _sample_attend_kernel(pt_ref, idx_ref, ok_ref, q_ref, kn_ref, vn_ref, ck_hbm, cv_hbm, o_ref,
                          kbuf, vbuf, sem, *, t_len, page, ppb, past_len, scale, hps):
    b = pl.program_id(0)
    hg = pl.program_id(1)
    n_hg = pl.num_programs(1)
    step = b * n_hg + hg
    cur = step % 2
    n_sel = t_len * MOBA_TOP_K * ppb
    n_pages = past_len // page

    def copies(st, hl, hh, slot, phys):
        return (pltpu.make_async_copy(ck_hbm.at[phys, :, hh, :], kbuf.at[st, hl, slot], sem.at[st, 0]),
                pltpu.make_async_copy(cv_hbm.at[phys, :, hh, :], vbuf.at[st, hl, slot], sem.at[st, 1]))

    def slot_pages(bb, hh):
        out = []
        for t in range(t_len):
            for kk in range(MOBA_TOP_K):
                n = idx_ref[bb, (hh * t_len + t) * MOBA_TOP_K + kk]
                for pp in range(ppb):
                    lp = jnp.clip(n * ppb + pp, 0, n_pages - 1)
                    out.append(((t * MOBA_TOP_K + kk) * ppb + pp, pt_ref[bb, lp], n * ppb + pp))
        out.append((n_sel, pt_ref[bb, n_pages - 1], None))
        return out

    def start_all(st, bb, hgg):
        for hl in range(hps):
            hh = hgg * hps + hl
            for slot, phys, _ in slot_pages(bb, hh):
                for cp in copies(st, hl, hh, slot, phys):
                    cp.start()

    @pl.when(step == 0)
    def _():
        start_all(0, b, hg)

    wrap = hg + 1 == n_hg
    b_next = b + wrap.astype(jnp.int32)
    hg_next = jnp.where(wrap, 0, hg + 1)

    @pl.when(step + 1 < pl.num_programs(0) * n_hg)
    def _():
        start_all(1 - cur, b_next, hg_next)

    n_keys = (n_sel + 2) * page
    lane = lax.broadcasted_iota(jnp.int32, (1, page), 1)
    trow = lax.broadcasted_iota(jnp.int32, (SUBLANES, n_keys), 0)
    kcol = lax.broadcasted_iota(jnp.int32, (SUBLANES, n_keys), 1)
    q_pos = past_len + trow
    blk_start = (q_pos // MOBA_BLOCK) * MOBA_BLOCK
    per_t = MOBA_TOP_K * ppb * page
    sel_cols = n_sel * page
    pad = jnp.zeros((page - t_len, LANES), F32)

    head_pages = [slot_pages(b, hg * hps + hl) for hl in range(hps)]
    for hl in range(hps):
        hcols = slice(hl * LANES, (hl + 1) * LANES)
        kbuf[cur, hl, n_sel + 1] = jnp.concatenate([kn_ref[:, hcols], pad], axis=0)
        vbuf[cur, hl, n_sel + 1] = jnp.concatenate([vn_ref[:, hcols], pad], axis=0)

    for hl in range(hps):
        for slot, phys, _ in head_pages[hl]:
            for cp in copies(cur, hl, hg * hps + hl, slot, phys):
                cp.wait()

    for hl in range(hps):
        h = hg * hps + hl
        hcols = slice(hl * LANES, (hl + 1) * LANES)
        pos, okv = [], []
        for slot, _, lp in head_pages[hl][:-1]:
            n_ok = ok_ref[b, (h * t_len + slot // (MOBA_TOP_K * ppb)) * MOBA_TOP_K
                          + (slot // ppb) % MOBA_TOP_K]
            pos.append(lane + jnp.clip(lp, 0, n_pages - 1) * page)
            okv.append(jnp.broadcast_to(n_ok, (1, page)))
        pos.append(lane + (past_len - page))
        pos.append(lane + past_len)
        pos = jnp.concatenate(pos, axis=1)
        okv = jnp.concatenate(okv + [jnp.zeros((1, n_keys - sel_cols), jnp.int32)], axis=1)

        kall = kbuf[cur, hl].reshape(n_keys, LANES).astype(BF16)
        vall = vbuf[cur, hl].reshape(n_keys, LANES).astype(BF16)
        q8 = jnp.concatenate([q_ref[:, hcols], jnp.zeros((SUBLANES - t_len, LANES), F32)], axis=0)
        s = lax.dot_general(q8.astype(BF16), kall, _NT, preferred_element_type=F32)

        dist = (q_pos - pos).astype(F32)
        logits = s * scale - _alibi_slope(h) * dist
        sel_ok = (kcol // per_t == trow) & (okv > 0)
        own_ok = ((kcol >= sel_cols) & (pos <= q_pos) & (pos >= blk_start)
                  & (kcol < sel_cols + page + t_len))
        logits = jnp.where(sel_ok | own_ok, logits, NEG)
        m = jnp.max(logits, axis=-1, keepdims=True)
        p = jnp.exp(logits - m)
        denom = jnp.sum(p, axis=-1, keepdims=True)
        out = jnp.dot(p.astype(BF16), vall, preferred_element_type=F32) / denom
        o_ref[:, hcols] = out[:t_len, :]


def _sample_attend(page_table, idx, ok, q_s, k_s, v_s, cache_k, cache_v, *, n_heads, hps=4):
    db, t_len, w = q_s.shape
    page = cache_k.shape[1]
    n_pages = page_table.shape[1]
    ppb = MOBA_BLOCK // page
    n_slots = t_len * MOBA_TOP_K * ppb + 2
    assert n_heads % hps == 0
    tok = pl.BlockSpec((None, t_len, hps * LANES), lambda bi, hi, *_: (bi, 0, hi))
    grid_spec = pltpu.PrefetchScalarGridSpec(
        num_scalar_prefetch=3,
        grid=(db, n_heads // hps),
        in_specs=[tok, tok, tok,
                  pl.BlockSpec(memory_space=pl.ANY), pl.BlockSpec(memory_space=pl.ANY)],
        out_specs=tok,
        scratch_shapes=[pltpu.VMEM((2, hps, n_slots, page, LANES), F32),
                        pltpu.VMEM((2, hps, n_slots, page, LANES), F32),
                        pltpu.SemaphoreType.DMA((2, 2))],
    )
    return pl.pallas_call(
        functools.partial(_sample_attend_kernel, t_len=t_len, page=page, ppb=ppb,
                          past_len=n_pages * page, scale=LANES ** -0.5, hps=hps),
        grid_spec=grid_spec,
        out_shape=jax.ShapeDtypeStruct((db, t_len, w), F32),
        compiler_params=_params(("arbitrary", "arbitrary")),
        name="sample_attend",
    )(page_table, idx, ok, q_s, k_s, v_s, cache_k, cache_v)


def _ssd_sample_kernel(xbc_ref, sc_ref, dtr_ref, z_ref, h0_ref, cw_ref, cb_ref, dtb_ref, alog_ref,
                       dskip_ref, sg_ref, expand_ref, s_ref, convs_ref, h_ref, xp_ref,
                       *, cw, t_len, ssm_w, groups, n_state):
    p_dim = SSM_HEAD_DIM
    heads = ssm_w // p_dim
    hpg = heads // groups
    gw = hpg * p_dim
    tail = cw - 1
    xp_ref[0:tail, :] = sc_ref[...]
    xp_ref[tail:tail + t_len, :] = xbc_ref[...]
    conv = cb_ref[...]
    for i in range(cw):
        conv = conv + xp_ref[i:i + t_len, :] * cw_ref[i:i + 1, :]
    act = _silu(conv)
    convs_ref[...] = xp_ref[t_len:t_len + tail, :]

    dt = _softplus(dtr_ref[...] + dtb_ref[...])
    da = dt * (-jnp.exp(alog_ref[...]))
    cum = [da[0:1]]
    for t in range(1, t_len):
        cum.append(cum[-1] + da[t:t + 1])
    lane = lax.broadcasted_iota(jnp.int32, (1, LANES), 1)
    xs = act[:, :ssm_w]
    zpad = jnp.zeros((SUBLANES - t_len, n_state), F32)

    b8, c8, cbs = [], [], []
    for g in range(groups):
        bg = act[:, ssm_w + g * n_state:ssm_w + (g + 1) * n_state]
        cg = act[:, ssm_w + (groups + g) * n_state:ssm_w + (groups + g + 1) * n_state]
        b8.append(jnp.concatenate([bg, zpad], axis=0).astype(BF16))
        c8.append(jnp.concatenate([cg, zpad], axis=0).astype(BF16))
        cbs.append(lax.dot_general(c8[g], b8[g], _NT, preferred_element_type=F32))

    pairs = [(l, s) for l in range(t_len) for s in range(l + 1)]
    rows = []
    for l, s in pairs:
        cb_ls = jnp.zeros((1, LANES), F32)
        for g in range(groups):
            in_g = (lane >= g * hpg) & (lane < (g + 1) * hpg)
            cb_ls = jnp.where(in_g, cbs[g][l:l + 1, s:s + 1], cb_ls)
        rows.append(cb_ls * jnp.exp(cum[l] - cum[s]) * dt[s:s + 1])
    rows += [jnp.exp(cum[l]) for l in range(t_len)]
    rows += [jnp.exp(cum[-1] - cum[s]) * dt[s:s + 1] for s in range(t_len)]
    n_rows = -(-len(rows) // SUBLANES) * SUBLANES
    rows.append(jnp.zeros((n_rows - len(rows), LANES), F32))
    rx = jnp.dot(jnp.concatenate(rows, axis=0), expand_ref[...], precision=lax.Precision.HIGHEST,
                 preferred_element_type=F32)
    r_ecum = len(pairs)
    r_coef = r_ecum + t_len

    y_diag = []
    for l in range(t_len):
        acc = jnp.zeros((1, ssm_w), F32)
        for s in range(l + 1):
            i = pairs.index((l, s))
            acc = acc + rx[i:i + 1, :] * xs[s:s + 1, :]
        y_diag.append(acc)
    y = dskip_ref[...] * xs + jnp.concatenate(y_diag, axis=0)

    y_off = []
    for g in range(groups):
        hg = h0_ref[g * hpg:(g + 1) * hpg].reshape(gw, n_state)
        y_off.append(lax.dot_general(c8[g], hg.astype(BF16), _NT, preferred_element_type=F32)[:t_len])
    y = y + jnp.concatenate(y_off, axis=1) * rx[r_ecum:r_ecum + t_len, :]
    for g in range(groups):
        gl = slice(g * gw, (g + 1) * gw)
        s_ref[:, gl] = _gated_norm(y[:, gl], z_ref[:, gl], sg_ref[:, gl])

    e_last = jnp.exp(cum[-1])
    xw = xs * rx[r_coef:r_coef + t_len, :]
    for g in range(groups):
        xw8 = jnp.concatenate([xw[:, g * gw:(g + 1) * gw], jnp.zeros((SUBLANES - t_len, gw), F32)], axis=0)
        st = lax.dot_general(xw8.astype(BF16), b8[g], _TN, preferred_element_type=F32)
        for ee in range(hpg):
            e = g * hpg + ee
            h_ref[e] = h0_ref[e] * e_last[:, e:e + 1] + st[ee * p_dim:(ee + 1) * p_dim, :]


def _ssd_sample(xbc, state_conv, dtr, z, state_ssm, conv_w, conv_b, dtb, alog, dskip_x, ssm_g, expand,
                *, ssm_w, groups, n_state):
    db, t_len, cc = xbc.shape
    cw = conv_w.shape[0]
    heads = ssm_w // SSM_HEAD_DIM
    const = lambda shape: pl.BlockSpec(shape, lambda bi: (0,) * len(shape))
    return pl.pallas_call(
        functools.partial(_ssd_sample_kernel, cw=cw, t_len=t_len, ssm_w=ssm_w, groups=groups,
                          n_state=n_state),
        grid=(db,),
        in_specs=[
            pl.BlockSpec((None, t_len, cc), lambda bi: (bi, 0, 0)),
            pl.BlockSpec((None, cw - 1, cc), lambda bi: (bi, 0, 0)),
            pl.BlockSpec((None, t_len, LANES), lambda bi: (bi, 0, 0)),
            pl.BlockSpec((None, t_len, ssm_w), lambda bi: (bi, 0, 0)),
            pl.BlockSpec((None, heads, SSM_HEAD_DIM, n_state), lambda bi: (bi, 0, 0, 0)),
            const((cw, cc)), const((1, cc)), const((1, LANES)), const((1, LANES)), const((1, ssm_w)),
            const((1, ssm_w)), const((LANES, ssm_w)),
        ],
        out_specs=(
            pl.BlockSpec((None, t_len, ssm_w), lambda bi: (bi, 0, 0)),
            pl.BlockSpec((None, cw - 1, cc), lambda bi: (bi, 0, 0)),
            pl.BlockSpec((None, heads, SSM_HEAD_DIM, n_state), lambda bi: (bi, 0, 0, 0)),
        ),
        out_shape=(
            jax.ShapeDtypeStruct((db, t_len, ssm_w), BF16),
            jax.ShapeDtypeStruct((db, cw - 1, cc), F32),
            jax.ShapeDtypeStruct((db, heads, SSM_HEAD_DIM, n_state), F32),
        ),
        scratch_shapes=[pltpu.VMEM((2 * SUBLANES, cc), F32)],
        compiler_params=_params(("parallel",)),
        name="ssd_sample",
    )(xbc, state_conv, dtr, z, state_ssm, conv_w, conv_b, dtb, alog, dskip_x, ssm_g, expand)


def _row_tile(m, target):
    tm = min(m, target)
    assert m % tm == 0, (m, tm)
    return tm


def _page_split(n_phys, moba_steps, nb, ssd_steps):
    per_moba = max(1, n_phys // 2 // moba_steps)
    rest = max(n_phys - moba_steps * per_moba, 0)
    per_ssd = max(1, -(-rest // ssd_steps))
    return per_moba, per_ssd


def kernel(x_prompt, x_sample, cache_k, cache_v, page_table, state_conv, state_ssm, norm_mix_g, w_in, q_norm_g, k_norm_g, conv_w, conv_b, dt_bias, a_log, d_skip, attn_out_g, ssm_out_g, w_out, norm_ffn_g, w_gate, w_up, w_down):
    b, s, d = x_prompt.shape
    db, t_len, _ = x_sample.shape
    n_phys, page, n_heads, dh = cache_k.shape
    attn_w = n_heads * dh
    heads, p_dim, n_state = state_ssm.shape[1:]
    assert p_dim == SSM_HEAD_DIM and dh == LANES and heads <= LANES
    ssm_w = heads * p_dim
    conv_ch = state_conv.shape[2]
    groups = (conv_ch - ssm_w) // (2 * n_state)
    n_main = 3 * attn_w + ssm_w + conv_ch

    w_dt = jnp.pad(w_in[:, n_main:].astype(BF16), ((0, 0), (0, LANES - heads)))
    w_out_b, w_gate_b, w_up_b, w_down_b = (w.astype(BF16) for w in (w_out, w_gate, w_up, w_down))
    row = lambda v: v.reshape(1, -1).astype(F32)
    padl = lambda v: jnp.pad(v.astype(F32), (0, LANES - v.shape[0])).reshape(1, LANES)
    dskip_x = jnp.repeat(d_skip.astype(F32), p_dim).reshape(1, ssm_w)
    expand = (jnp.arange(LANES)[:, None] == (jnp.arange(ssm_w)[None, :] // p_dim)).astype(F32)
    in_args = (row(norm_mix_g), w_in, w_dt, row(q_norm_g), row(k_norm_g))
    in_kw = dict(attn_w=attn_w, ssm_w=ssm_w, conv_ch=conv_ch)
    ssd_kw = dict(ssm_w=ssm_w, groups=groups, n_state=n_state)
    ssd_par = (conv_w, row(conv_b), padl(dt_bias), padl(a_log), dskip_x)

    xp2 = x_prompt.reshape(b * s, d)
    q, k, v, z, xbc, dtr = _inproj(xp2, *in_args, tm=_row_tile(b * s, 1024), **in_kw)
    moba_steps, ssd_steps = b * n_heads, b * (s // SSD_CHUNK)
    pg_moba, pg_ssd = _page_split(n_phys, moba_steps, s // MOBA_BLOCK, ssd_steps)
    ps_ssd0 = n_phys - ssd_steps * pg_ssd
    assert 0 <= ps_ssd0 <= moba_steps * pg_moba <= n_phys
    attn, ps_a = _moba_prompt(q.reshape(b, s, attn_w), k.reshape(b, s, attn_w), v.reshape(b, s, attn_w),
                              cache_k, n_heads=n_heads, pages_per_step=pg_moba)
    sn, conv_p, ssm_p, ps_b = _ssd_prompt(xbc.reshape(b, s, conv_ch), dtr.reshape(b, s, LANES),
                                          z.reshape(b, s, ssm_w), *ssd_par, row(ssm_out_g), expand, cache_k,
                                          page0=ps_ssd0, pages_per_step=pg_ssd, **ssd_kw)
    h_p = _outproj(attn.reshape(b * s, attn_w), sn.reshape(b * s, ssm_w), xp2, row(attn_out_g), w_out_b,
                   tm=_row_tile(b * s, 512))
    y_p = _ffn(h_p, row(norm_ffn_g), w_gate_b, w_up_b, w_down_b,
               tm=_row_tile(b * s, 1024)).reshape(b, s, d)
    page_shape = (b, s // page, page, n_heads, dh)
    k_p, v_p = k.reshape(page_shape), v.reshape(page_shape)

    ms = db * t_len
    xs2 = x_sample.reshape(ms, d)
    qs, k_s, v_s, zs, xbcs, dtrs = _inproj(xs2, *in_args, tm=_row_tile(ms, 1024), **in_kw)
    qs3, ks3, vs3 = (a.reshape(db, t_len, attn_w) for a in (qs, k_s, v_s))
    sel = _sample_select(page_table, ps_a, ps_b, qs3, n_heads=n_heads, page=page, split=ps_ssd0)
    idx = sel[:, :, :t_len, :MOBA_TOP_K].reshape(db, n_heads * t_len * MOBA_TOP_K)
    ok = sel[:, :, :t_len, MOBA_TOP_K:2 * MOBA_TOP_K].reshape(db, n_heads * t_len * MOBA_TOP_K)
    attn_s = _sample_attend(page_table, idx, ok, qs3, ks3, vs3, cache_k, cache_v, n_heads=n_heads)
    sns, conv_s, ssm_s = _ssd_sample(xbcs.reshape(db, t_len, conv_ch), state_conv,
                                     dtrs.reshape(db, t_len, LANES), zs.reshape(db, t_len, ssm_w), state_ssm,
                                     *ssd_par, row(ssm_out_g), expand, **ssd_kw)
    tms = _row_tile(ms, 512)
    h_s = _outproj(attn_s.reshape(ms, attn_w), sns.reshape(ms, ssm_w), xs2, row(attn_out_g), w_out_b, tm=tms)
    y_s = _ffn(h_s, row(norm_ffn_g), w_gate_b, w_up_b, w_down_b, tm=tms).reshape(db, t_len, d)
    tok_shape = (db, t_len, n_heads, dh)
    return (y_p, y_s, k_p, v_p, conv_p, ssm_p, k_s.reshape(tok_shape), v_s.reshape(tok_shape),
            conv_s, ssm_s)
```

```python
import functools

import jax
import jax.numpy as jnp
from jax import lax
from jax.experimental import pallas as pl
from jax.experimental.pallas import tpu as pltpu

F32 = jnp.float32
BF16 = jnp.bfloat16
EPS = 1e-6
NEG = -1e30

MOBA_BLOCK = 256
MOBA_TOP_K = 3
SSD_CHUNK = 128
SSM_HEAD_DIM = 64
LANES = 128
SUBLANES = 8
VMEM_LIMIT = 56 * 1024 * 1024
BIG_VMEM_LIMIT = 60 * 1024 * 1024

_NT = (((1,), (1,)), ((), ()))
_TN = (((0,), (0,)), ((), ()))


def _params(sem, vmem=VMEM_LIMIT):
    return pltpu.CompilerParams(dimension_semantics=sem, vmem_limit_bytes=vmem)


def _rms(x, g):
    return x * lax.rsqrt(jnp.mean(x * x, axis=-1, keepdims=True) + EPS) * g


def _silu(x):
    return x * (1.0 / (1.0 + jnp.exp(-x)))


def _softplus(x):
    return jnp.maximum(x, 0.0) + jnp.log1p(jnp.exp(-jnp.abs(x)))


def _alibi_slope(h):
    bits = (jnp.full((1, 1), 126, jnp.int32) - h) << 23
    return lax.bitcast_convert_type(bits, F32)


def _inproj_kernel(x_ref, g_ref, w_ref, wdt_ref, qg_ref, kg_ref,
                   q_ref, k_ref, v_ref, z_ref, xbc_ref, dt_ref, xn_ref, *, tn, seg):
    j = pl.program_id(1)

    @pl.when(j == 0)
    def _():
        xn_ref[...] = _rms(x_ref[...], g_ref[...]).astype(BF16)
        dt_ref[...] = jnp.dot(xn_ref[...], wdt_ref[...], preferred_element_type=F32)

    acc = jnp.dot(xn_ref[...], w_ref[...].astype(BF16), preferred_element_type=F32)

    def headnorm(out_ref, g):
        for hh in range(tn // LANES):
            sl = slice(hh * LANES, (hh + 1) * LANES)
            out_ref[:, sl] = _rms(acc[:, sl], g)

    @pl.when(j < seg[0])
    def _():
        headnorm(q_ref, qg_ref[...])

    @pl.when((j >= seg[0]) & (j < seg[1]))
    def _():
        headnorm(k_ref, kg_ref[...])

    @pl.when((j >= seg[1]) & (j < seg[2]))
    def _():
        v_ref[...] = acc

    @pl.when((j >= seg[2]) & (j < seg[3]))
    def _():
        z_ref[...] = acc

    @pl.when(j >= seg[3])
    def _():
        xbc_ref[...] = acc


def _inproj(x2, norm_g, w_all, w_dt, q_g, k_g, *, attn_w, ssm_w, conv_ch, tm, tn=512):
    m, d = x2.shape
    n_main = 3 * attn_w + ssm_w + conv_ch
    assert n_main % tn == 0
    seg = (attn_w // tn, 2 * attn_w // tn, 3 * attn_w // tn, (3 * attn_w + ssm_w) // tn)
    nj = n_main // tn

    def col(lo, hi):
        return lambda i, j: (i, jnp.clip(j - lo, 0, hi - lo - 1))

    out_shape = (
        jax.ShapeDtypeStruct((m, attn_w), F32), jax.ShapeDtypeStruct((m, attn_w), F32),
        jax.ShapeDtypeStruct((m, attn_w), F32), jax.ShapeDtypeStruct((m, ssm_w), F32),
        jax.ShapeDtypeStruct((m, conv_ch), F32), jax.ShapeDtypeStruct((m, LANES), F32))
    return pl.pallas_call(
        functools.partial(_inproj_kernel, tn=tn, seg=seg),
        grid=(m // tm, nj),
        in_specs=[
            pl.BlockSpec((tm, d), lambda i, j: (i, 0)),
            pl.BlockSpec((1, d), lambda i, j: (0, 0)),
            pl.BlockSpec((d, tn), lambda i, j: (0, j)),
            pl.BlockSpec((d, LANES), lambda i, j: (0, 0)),
            pl.BlockSpec((1, LANES), lambda i, j: (0, 0)),
            pl.BlockSpec((1, LANES), lambda i, j: (0, 0)),
        ],
        out_specs=(
            pl.BlockSpec((tm, tn), col(0, seg[0])),
            pl.BlockSpec((tm, tn), col(seg[0], seg[1])),
            pl.BlockSpec((tm, tn), col(seg[1], seg[2])),
            pl.BlockSpec((tm, tn), col(seg[2], seg[3])),
            pl.BlockSpec((tm, tn), col(seg[3], nj)),
            pl.BlockSpec((tm, LANES), lambda i, j: (i, 0)),
        ),
        out_shape=out_shape,
        scratch_shapes=[pltpu.VMEM((tm, d), BF16)],
        compiler_params=_params(("parallel", "arbitrary")),
        name="inproj",
    )(x2, norm_g, w_all, w_dt, q_g, k_g)


def _block_select(gate_t, n_past):
    row = lax.broadcasted_iota(jnp.int32, gate_t.shape, 0)
    valid = row < n_past
    if n_past <= MOBA_TOP_K:
        return valid & (gate_t > 0.5 * NEG)
    rank = jnp.zeros(gate_t.shape, jnp.int32)
    for j in range(n_past):
        other = gate_t[j:j + 1, :]
        rank = rank + ((other > gate_t) | ((other == gate_t) & (row > j))).astype(jnp.int32)
    return valid & (rank < MOBA_TOP_K) & (gate_t > 0.5 * NEG)


def _page_chunk_copy(ck_hbm, buf, sem, first_page, n_pages):
    return pltpu.make_async_copy(ck_hbm.at[pl.ds(first_page, n_pages)], buf, sem)


def _page_key_sums(pages, lanes_of_adds=4):
    n, page, heads, dh = pages.shape
    part = pages.reshape(n * lanes_of_adds, page // lanes_of_adds, heads, dh).sum(axis=1)
    return part.reshape(n, lanes_of_adds, heads, dh).sum(axis=1)


def _stream_page_sums(ck_hbm, ps_ref, pbuf, psem, step, n_steps, page0):
    n_pg = pbuf.shape[1]
    slot = step % 2

    def page_copy(st, sl):
        return _page_chunk_copy(ck_hbm, pbuf.at[sl], psem.at[sl], page0 + st * n_pg, n_pg)

    @pl.when(step == 0)
    def _():
        page_copy(0, 0).start()

    page_copy(step, slot).wait()

    @pl.when(step + 1 < n_steps)
    def _():
        page_copy(step + 1, 1 - slot).start()

    ps_ref[...] = _page_key_sums(pbuf[slot])


def _causal_chunks(total, nb):
    tri = nb * (nb + 1) // 2
    cum = [0] + [(total * (i + 1) * (i + 2) // 2 + tri // 2) // tri for i in range(nb)]
    return [hi - lo for lo, hi in zip(cum[:-1], cum[1:])], cum[:-1]


def _moba_prompt_kernel(q_ref, k_ref, v_ref, ck_hbm, o_ref, ps_ref, pbuf, psem, *, nb, scale, pages):
    blk = MOBA_BLOCK
    h = pl.program_id(1)
    step = pl.program_id(0) * pl.num_programs(1) + h
    page0 = step * pages
    sizes, offs = _causal_chunks(pages, nb)
    chunks = [i for i in range(nb) if sizes[i] > 0]
    n_slots = pbuf.shape[0]

    def page_copy(c):
        i, slot = chunks[c], c % n_slots
        return _page_chunk_copy(ck_hbm, pbuf.at[slot, pl.ds(0, sizes[i])], psem.at[slot],
                                page0 + offs[i], sizes[i])

    for c in range(min(n_slots - 1, len(chunks))):
        page_copy(c).start()
    slope = _alibi_slope(h)
    k = k_ref[...]
    kb = k.astype(BF16)
    vb = v_ref[...].astype(BF16)
    nb_pad = -(-nb // SUBLANES) * SUBLANES
    means = [jnp.mean(k[j * blk:(j + 1) * blk], axis=0, keepdims=True) for j in range(nb)]
    means = jnp.concatenate(means + [jnp.zeros((nb_pad - nb, LANES), F32)] * (nb_pad > nb), axis=0)
    for i in range(nb):
        qi = q_ref[i * blk:(i + 1) * blk, :]
        nk = (i + 1) * blk
        s = lax.dot_general(qi.astype(BF16), kb[:nk], _NT, preferred_element_type=F32)
        col = lax.broadcasted_iota(jnp.int32, (blk, nk), 1)
        logits = s * scale + slope * col.astype(F32)
        row = lax.broadcasted_iota(jnp.int32, (blk, blk), 0)
        colb = lax.broadcasted_iota(jnp.int32, (blk, blk), 1)
        pieces = []
        if i > 0:
            gate_t = lax.dot_general(means, qi, _NT, precision=lax.Precision.HIGHEST,
                                     preferred_element_type=F32)
            sel_t = _block_select(gate_t, i).astype(F32)
            sel_t = jnp.concatenate([sel_t, jnp.zeros((LANES - nb_pad, blk), F32)], axis=0)
            sel = jnp.concatenate([sel_t[:, r * LANES:(r + 1) * LANES].T
                                   for r in range(blk // LANES)], axis=0)
            pieces = [jnp.broadcast_to(sel[:, j:j + 1], (blk, blk)) for j in range(i)]
        pieces.append((colb <= row).astype(F32))
        allowed = jnp.concatenate(pieces, axis=1) if len(pieces) > 1 else pieces[0]
        logits = jnp.where(allowed > 0.5, logits, NEG)
        m = jnp.max(logits, axis=-1, keepdims=True)
        p = jnp.exp(logits - m)
        denom = jnp.sum(p, axis=-1, keepdims=True)
        out = jnp.dot(p.astype(BF16), vb[:nk], preferred_element_type=F32)
        o_ref[i * blk:(i + 1) * blk, :] = out / denom
        if i in chunks:
            c = chunks.index(i)
            page_copy(c).wait()
            if c + n_slots - 1 < len(chunks):
                page_copy(c + n_slots - 1).start()
            ps_ref[offs[i]:offs[i] + sizes[i]] = _page_key_sums(pbuf[c % n_slots, 0:sizes[i]])


def _moba_prompt(q, k, v, cache_k, *, n_heads, pages_per_step):
    b, s, w = q.shape
    dh = w // n_heads
    nb = s // MOBA_BLOCK
    npc = max(_causal_chunks(pages_per_step, nb)[0])
    page, ch, cdh = cache_k.shape[1:]
    spec = pl.BlockSpec((None, s, dh), lambda bi, hi: (bi, 0, hi))
    return pl.pallas_call(
        functools.partial(_moba_prompt_kernel, nb=nb, scale=dh ** -0.5, pages=pages_per_step),
        grid=(b, n_heads),
        in_specs=[spec, spec, spec, pl.BlockSpec(memory_space=pl.ANY)],
        out_specs=(spec, pl.BlockSpec((pages_per_step, ch, cdh), lambda bi, hi: (bi * n_heads + hi, 0, 0))),
        out_shape=(jax.ShapeDtypeStruct((b, s, w), F32),
                   jax.ShapeDtypeStruct((b * n_heads * pages_per_step, ch, cdh), F32)),
        scratch_shapes=[pltpu.VMEM((3, npc, page, ch, cdh), F32), pltpu.SemaphoreType.DMA((3,))],
        compiler_params=_params(("arbitrary", "arbitrary")),
        name="moba_prompt",
    )(q, k, v, cache_k)


def _gated_norm(y, z, g):
    return _rms(y * _silu(z), g).astype(BF16)


def _ssd_prompt_kernel(xbc_ref, dtr_ref, z_ref, cw_ref, cb_ref, dtb_ref, alog_ref, dskip_ref, sg_ref,
                       expand_ref, ck_hbm, s_ref, convp_ref, h_ref, ps_ref, xp_ref, pbuf, psem,
                       *, cw, ssm_w, groups, n_state, page0):
    c = pl.program_id(1)
    nc = pl.num_programs(1)
    _stream_page_sums(ck_hbm, ps_ref, pbuf, psem, pl.program_id(0) * nc + c, pl.num_programs(0) * nc,
                      page0)
    lc = SSD_CHUNK
    p_dim = SSM_HEAD_DIM
    heads = ssm_w // p_dim
    hpg = heads // groups
    gw = hpg * p_dim
    tail = cw - 1

    @pl.when(c == 0)
    def _():
        xp_ref[0:SUBLANES, :] = jnp.zeros((SUBLANES, xp_ref.shape[1]), F32)
        h_ref[...] = jnp.zeros(h_ref.shape, F32)

    xc = xbc_ref[...]
    xp_ref[SUBLANES:SUBLANES + lc, :] = xc
    conv = cb_ref[...]
    for i in range(cw):
        conv = conv + xp_ref[SUBLANES - tail + i:SUBLANES - tail + i + lc, :] * cw_ref[i:i + 1, :]
    act = _silu(conv)
    xp_ref[SUBLANES - tail:SUBLANES, :] = xc[lc - tail:lc, :]

    @pl.when(c == nc - 1)
    def _():
        convp_ref[...] = xc[lc - tail:lc, :]

    dt = _softplus(dtr_ref[...] + dtb_ref[...])
    a = -jnp.exp(alog_ref[...])
    row = lax.broadcasted_iota(jnp.int32, (lc, lc), 0)
    colm = lax.broadcasted_iota(jnp.int32, (lc, lc), 1)
    causal = row >= colm
    tri = causal.astype(F32)
    cum = jnp.dot(tri, dt * a, precision=lax.Precision.HIGHEST, preferred_element_type=F32)
    cum_t = cum.T
    dt_t = dt.T
    cum_last = cum[lc - 1:lc, :]
    dec_end = jnp.exp(cum_last - cum) * dt
    ecum = jnp.exp(cum)
    expand = expand_ref[...]
    ecum_x = jnp.dot(ecum, expand, precision=lax.Precision.HIGHEST, preferred_element_type=F32)
    dec_end_x = jnp.dot(dec_end, expand, precision=lax.Precision.HIGHEST,
                        preferred_element_type=F32)

    xs = act[:, :ssm_w]
    xs_b = xs.astype(BF16)
    y = dskip_ref[...] * xs
    lane = lax.broadcasted_iota(jnp.int32, (lc, 2 * p_dim), 1)
    for g in range(groups):
        bg = act[:, ssm_w + g * n_state:ssm_w + (g + 1) * n_state].astype(BF16)
        cg = act[:, ssm_w + (groups + g) * n_state:ssm_w + (groups + g + 1) * n_state].astype(BF16)
        cb = lax.dot_general(cg, bg, _NT, preferred_element_type=F32)
        hg = h_ref[g * hpg:(g + 1) * hpg].reshape(gw, n_state)
        y_off = lax.dot_general(cg, hg.astype(BF16), _NT, preferred_element_type=F32)
        y_off = y_off * ecum_x[:, g * gw:(g + 1) * gw]
        y_diag = []
        for pr in range(hpg // 2):
            ws = []
            for e in (g * hpg + 2 * pr, g * hpg + 2 * pr + 1):
                diff = cum[:, e:e + 1] - cum_t[e:e + 1, :]
                decay = jnp.where(causal, jnp.exp(diff), 0.0)
                ws.append((cb * decay * dt_t[e:e + 1, :]).astype(BF16))
            lo = (g * hpg + 2 * pr) * p_dim
            xpair = xs_b[:, lo:lo + 2 * p_dim]
            zero = jnp.zeros_like(xpair)
            rhs = jnp.concatenate([jnp.where(lane < p_dim, xpair, zero),
                                   jnp.where(lane >= p_dim, xpair, zero)], axis=0)
            y_diag.append(jnp.dot(jnp.concatenate(ws, axis=1), rhs, preferred_element_type=F32))
        y_g = jnp.concatenate(y_diag, axis=1) + y_off
        xw = (xs[:, g * gw:(g + 1) * gw] * dec_end_x[:, g * gw:(g + 1) * gw]).astype(BF16)
        st = lax.dot_general(xw, bg, _TN, preferred_element_type=F32)
        for ee in range(hpg):
            e = g * hpg + ee
            cd = jnp.exp(cum_t[e:e + 1, lc - 1:lc])
            h_ref[e] = h_ref[e] * cd + st[ee * p_dim:(ee + 1) * p_dim, :]
        gl = slice(g * gw, (g + 1) * gw)
        s_ref[:, gl] = _gated_norm(y[:, gl] + y_g, z_ref[:, gl], sg_ref[:, gl])


def _ssd_prompt(xbc, dtr, z, conv_w, conv_b, dtb, alog, dskip_x, ssm_g, expand, cache_k,
                *, ssm_w, groups, n_state, page0, pages_per_step):
    b, s, cc = xbc.shape
    cw = conv_w.shape[0]
    heads = ssm_w // SSM_HEAD_DIM
    lc = SSD_CHUNK
    nc = s // lc
    page, ch, cdh = cache_k.shape[1:]
    const = lambda shape: pl.BlockSpec(shape, lambda bi, ci: (0,) * len(shape))
    return pl.pallas_call(
        functools.partial(_ssd_prompt_kernel, cw=cw, ssm_w=ssm_w, groups=groups, n_state=n_state,
                          page0=page0),
        grid=(b, nc),
        in_specs=[
            pl.BlockSpec((None, lc, cc), lambda bi, ci: (bi, ci, 0)),
            pl.BlockSpec((None, lc, LANES), lambda bi, ci: (bi, ci, 0)),
            pl.BlockSpec((None, lc, ssm_w), lambda bi, ci: (bi, ci, 0)),
            const((cw, cc)), const((1, cc)), const((1, LANES)), const((1, LANES)),
            const((1, ssm_w)), const((1, ssm_w)), const((LANES, ssm_w)),
            pl.BlockSpec(memory_space=pl.ANY),
        ],
        out_specs=(
            pl.BlockSpec((None, lc, ssm_w), lambda bi, ci: (bi, ci, 0)),
            pl.BlockSpec((None, cw - 1, cc), lambda bi, ci: (bi, 0, 0)),
            pl.BlockSpec((None, heads, SSM_HEAD_DIM, n_state), lambda bi, ci: (bi, 0, 0, 0)),
            pl.BlockSpec((pages_per_step, ch, cdh), lambda bi, ci: (bi * nc + ci, 0, 0)),
        ),
        out_shape=(
            jax.ShapeDtypeStruct((b, s, ssm_w), BF16),
            jax.ShapeDtypeStruct((b, cw - 1, cc), F32),
            jax.ShapeDtypeStruct((b, heads, SSM_HEAD_DIM, n_state), F32),
            jax.ShapeDtypeStruct((b * nc * pages_per_step, ch, cdh), F32),
        ),
        scratch_shapes=[pltpu.VMEM((SUBLANES + lc, cc), F32),
                        pltpu.VMEM((2, pages_per_step, page, ch, cdh), F32),
                        pltpu.SemaphoreType.DMA((2,))],
        compiler_params=_params(("arbitrary", "arbitrary")),
        name="ssd_prompt",
    )(xbc, dtr, z, conv_w, conv_b, dtb, alog, dskip_x, ssm_g, expand, cache_k)


def _outproj_kernel(attn_ref, s_ref, x_ref, ag_ref, w_ref, o_ref, *, rc):
    for r in range(o_ref.shape[0] // rc):
        rows = slice(r * rc, (r + 1) * rc)
        a = _rms(attn_ref[rows, :], ag_ref[...]).astype(BF16)
        mix = jnp.concatenate([a, s_ref[rows, :]], axis=1)
        o_ref[rows, :] = x_ref[rows, :] + jnp.dot(mix, w_ref[...], preferred_element_type=F32)


def _outproj(attn, s_norm, x2, attn_g, w_out, *, tm, rc=256):
    m, attn_w = attn.shape
    ssm_w = s_norm.shape[1]
    d = x2.shape[1]
    return pl.pallas_call(
        functools.partial(_outproj_kernel, rc=min(rc, tm)),
        grid=(m // tm,),
        in_specs=[
            pl.BlockSpec((tm, attn_w), lambda i: (i, 0)),
            pl.BlockSpec((tm, ssm_w), lambda i: (i, 0)),
            pl.BlockSpec((tm, d), lambda i: (i, 0)),
            pl.BlockSpec((1, attn_w), lambda i: (0, 0)),
            pl.BlockSpec((attn_w + ssm_w, d), lambda i: (0, 0)),
        ],
        out_specs=pl.BlockSpec((tm, d), lambda i: (i, 0)),
        out_shape=jax.ShapeDtypeStruct((m, d), F32),
        compiler_params=_params(("parallel",)),
        name="outproj",
    )(attn, s_norm, x2, attn_g, w_out)


def _ffn_kernel(h_ref, g_ref, wg_ref, wu_ref, wd_ref, o_ref, hn_ref, *, rc):
    f = pl.program_id(1)

    @pl.when(f == 0)
    def _():
        hn_ref[...] = _rms(h_ref[...], g_ref[...]).astype(BF16)
        o_ref[...] = h_ref[...]

    for r in range(o_ref.shape[0] // rc):
        rows = slice(r * rc, (r + 1) * rc)
        hn = hn_ref[rows, :]
        gate = jnp.dot(hn, wg_ref[...], preferred_element_type=F32)
        up = jnp.dot(hn, wu_ref[...], preferred_element_type=F32)
        o_ref[rows, :] += jnp.dot((_silu(gate) * up).astype(BF16), wd_ref[...],
                                  preferred_element_type=F32)


def _ffn(h, g, w_gate, w_up, w_down, *, tm, tf=512, rc=512):
    m, d = h.shape
    dff = w_gate.shape[1]
    return pl.pallas_call(
        functools.partial(_ffn_kernel, rc=min(rc, tm)),
        grid=(m // tm, dff // tf),
        in_specs=[
            pl.BlockSpec((tm, d), lambda i, f: (i, 0)),
            pl.BlockSpec((1, d), lambda i, f: (0, 0)),
            pl.BlockSpec((d, tf), lambda i, f: (0, f)),
            pl.BlockSpec((d, tf), lambda i, f: (0, f)),
            pl.BlockSpec((tf, d), lambda i, f: (f, 0)),
        ],
        out_specs=pl.BlockSpec((tm, d), lambda i, f: (i, 0)),
        out_shape=jax.ShapeDtypeStruct((m, d), F32),
        scratch_shapes=[pltpu.VMEM((tm, d), BF16)],
        compiler_params=_params(("parallel", "arbitrary"), vmem=BIG_VMEM_LIMIT),
        name="ffn",
    )(h, g, w_gate, w_up, w_down)


def _sample_select_kernel(pt_ref, psa_ref, psb_ref, q_ref, sel_ref, blk_ref,
                          *, nb, ppb, n_heads, t_len, split):
    b = pl.program_id(0)

    def page_sum(pg):
        lo = psa_ref[jnp.minimum(pg, split - 1)]
        hi = psb_ref[jnp.maximum(pg - split, 0)]
        return jnp.where(pg < split, lo, hi)

    for n in range(nb):
        acc = page_sum(pt_ref[b, n * ppb])
        for pp in range(1, ppb):
            acc = acc + page_sum(pt_ref[b, n * ppb + pp])
        blk_ref[n] = acc / MOBA_BLOCK
    lane = lax.broadcasted_iota(jnp.int32, (SUBLANES, LANES), 1)
    for h in range(n_heads):
        qh = q_ref[:, h * LANES:(h + 1) * LANES]
        q8 = jnp.concatenate([qh, jnp.zeros((SUBLANES - t_len, LANES), F32)], axis=0)
        bm = blk_ref[:, h, :]
        gate = lax.dot_general(q8, bm, _NT, precision=lax.Precision.HIGHEST,
                               preferred_element_type=F32)
        lane_n = lax.broadcasted_iota(jnp.int32, gate.shape, 1).astype(F32)
        out = jnp.zeros((SUBLANES, LANES), jnp.int32)
        for kk in range(MOBA_TOP_K):
            mx = jnp.max(gate, axis=-1, keepdims=True)
            idx_f = jnp.min(jnp.where(gate == mx, lane_n, float(nb)), axis=-1, keepdims=True)
            idx = idx_f.astype(jnp.int32)
            ok = (mx > 0.5 * NEG).astype(jnp.int32)
            out = jnp.where(lane == kk, idx, out)
            out = jnp.where(lane == MOBA_TOP_K + kk, ok, out)
            gate = jnp.where(lane_n == idx_f, -jnp.inf, gate)
        sel_ref[h] = out


def _sample_select(page_table, ps_lo, ps_hi, q_s, *, n_heads, page, split):
    db, t_len, w = q_s.shape
    n_pages = page_table.shape[1]
    _, h, dh = ps_lo.shape
    ppb = MOBA_BLOCK // page
    nb = n_pages // ppb
    whole = lambda a: pl.BlockSpec(a.shape, lambda bi, pt: (0, 0, 0), pipeline_mode=pl.Buffered(1))
    grid_spec = pltpu.PrefetchScalarGridSpec(
        num_scalar_prefetch=1,
        grid=(db,),
        in_specs=[whole(ps_lo), whole(ps_hi),
                  pl.BlockSpec((None, t_len, w), lambda bi, pt: (bi, 0, 0))],
        out_specs=pl.BlockSpec((None, n_heads, SUBLANES, LANES), lambda bi, pt: (bi, 0, 0, 0)),
        scratch_shapes=[pltpu.VMEM((nb, h, dh), F32)],
    )
    return pl.pallas_call(
        functools.partial(_sample_select_kernel, nb=nb, ppb=ppb, n_heads=n_heads, t_len=t_len,
                          split=split),
        grid_spec=grid_spec,
        out_shape=jax.ShapeDtypeStruct((db, n_heads, SUBLANES, LANES), jnp.int32),
        compiler_params=_params(("arbitrary",)),
        name="sample_select",
    )(page_table, ps_lo, ps_hi, q_s)


def _sample_attend_kernel(pt_ref, idx_ref, ok_ref, q_ref, kn_ref, vn_ref, ck_hbm, cv_hbm, o_ref,
                          kbuf, vbuf, sem, *, t_len, page, ppb, past_len, scale, hps):
    b = pl.program_id(0)
    hg = pl.program_id(1)
    n_hg = pl.num_programs(1)
    step = b * n_hg + hg
    cur = step % 2
    n_sel = t_len * MOBA_TOP_K * ppb
    n_pages = past_len // page

    def copies(st, hl, hh, slot, phys):
        return (pltpu.make_async_copy(ck_hbm.at[phys, :, hh, :], kbuf.at[st, hl, slot], sem.at[st, 0]),
                pltpu.make_async_copy(cv_hbm.at[phys, :, hh, :], vbuf.at[st, hl, slot], sem.at[st, 1]))

    def slot_pages(bb, hh):
        out = []
        for t in range(t_len):
            for kk in range(MOBA_TOP_K):
                n = idx_ref[bb, (hh * t_len + t) * MOBA_TOP_K + kk]
                for pp in range(ppb):
                    lp = jnp.clip(n * ppb + pp, 0, n_pages - 1)
                    out.append(((t * MOBA_TOP_K + kk) * ppb + pp, pt_ref[bb, lp], n * ppb + pp))
        out.append((n_sel, pt_ref[bb, n_pages - 1], None))
        return out

    def start_all(st, bb, hgg):
        for hl in range(hps):
            hh = hgg * hps + hl
            for slot, phys, _ in slot_pages(bb, hh):
                for cp in copies(st, hl, hh, slot, phys):
                    cp.start()

    @pl.when(step == 0)
    def _():
        start_all(0, b, hg)

    wrap = hg + 1 == n_hg
    b_next = b + wrap.astype(jnp.int32)
    hg_next = jnp.where(wrap, 0, hg + 1)

    @pl.when(step + 1 < pl.num_programs(0) * n_hg)
    def _():
        start_all(1 - cur, b_next, hg_next)

    n_keys = (n_sel + 2) * page
    lane = lax.broadcasted_iota(jnp.int32, (1, page), 1)
    trow = lax.broadcasted_iota(jnp.int32, (SUBLANES, n_keys), 0)
    kcol = lax.broadcasted_iota(jnp.int32, (SUBLANES, n_keys), 1)
    q_pos = past_len + trow
    blk_start = (q_pos // MOBA_BLOCK) * MOBA_BLOCK
    per_t = MOBA_TOP_K * ppb * page
    sel_cols = n_sel * page
    pad = jnp.zeros((page - t_len, LANES), F32)

    head_pages = [slot_pages(b, hg * hps + hl) for hl in range(hps)]
    for hl in range(hps):
        hcols = slice(hl * LANES, (hl + 1) * LANES)
        kbuf[cur, hl, n_sel + 1] = jnp.concatenate([kn_ref[:, hcols], pad], axis=0)
        vbuf[cur, hl, n_sel + 1] = jnp.concatenate([vn_ref[:, hcols], pad], axis=0)

    for hl in range(hps):
        for slot, phys, _ in head_pages[hl]:
            for cp in copies(cur, hl, hg * hps + hl, slot, phys):
                cp.wait()

    for hl in range(hps):
        h = hg * hps + hl
        hcols = slice(hl * LANES, (hl + 1) * LANES)
        pos, okv = [], []
        for slot, _, lp in head_pages[hl][:-1]:
            n_ok = ok_ref[b, (h * t_len + slot // (MOBA_TOP_K * ppb)) * MOBA_TOP_K
                          + (slot // ppb) % MOBA_TOP_K]
            pos.append(lane + jnp.clip(lp, 0, n_pages - 1) * page)
            okv.append(jnp.broadcast_to(n_ok, (1, page)))
        pos.append(lane + (past_len - page))
        pos.append(lane + past_len)
        pos = jnp.concatenate(pos, axis=1)
        okv = jnp.concatenate(okv + [jnp.zeros((1, n_keys - sel_cols), jnp.int32)], axis=1)

        kall = kbuf[cur, hl].reshape(n_keys, LANES).astype(BF16)
        vall = vbuf[cur, hl].reshape(n_keys, LANES).astype(BF16)
        q8 = jnp.concatenate([q_ref[:, hcols], jnp.zeros((SUBLANES - t_len, LANES), F32)], axis=0)
        s = lax.dot_general(q8.astype(BF16), kall, _NT, preferred_element_type=F32)

        dist = (q_pos - pos).astype(F32)
        logits = s * scale - _alibi_slope(h) * dist
        sel_ok = (kcol // per_t == trow) & (okv > 0)
        own_ok = ((kcol >= sel_cols) & (pos <= q_pos) & (pos >= blk_start)
                  & (kcol < sel_cols + page + t_len))
        logits = jnp.where(sel_ok | own_ok, logits, NEG)
        m = jnp.max(logits, axis=-1, keepdims=True)
        p = jnp.exp(logits - m)
        denom = jnp.sum(p, axis=-1, keepdims=True)
        out = jnp.dot(p.astype(BF16), vall, preferred_element_type=F32) / denom
        o_ref[:, hcols] = out[:t_len, :]


def _sample_attend(page_table, idx, ok, q_s, k_s, v_s, cache_k, cache_v, *, n_heads, hps=4):
    db, t_len, w = q_s.shape
    page = cache_k.shape[1]
    n_pages = page_table.shape[1]
    ppb = MOBA_BLOCK // page
    n_slots = t_len * MOBA_TOP_K * ppb + 2
    assert n_heads % hps == 0
    tok = pl.BlockSpec((None, t_len, hps * LANES), lambda bi, hi, *_: (bi, 0, hi))
    grid_spec = pltpu.PrefetchScalarGridSpec(
        num_scalar_prefetch=3,
        grid=(db, n_heads // hps),
        in_specs=[tok, tok, tok,
                  pl.BlockSpec(memory_space=pl.ANY), pl.BlockSpec(memory_space=pl.ANY)],
        out_specs=tok,
        scratch_shapes=[pltpu.VMEM((2, hps, n_slots, page, LANES), F32),
                        pltpu.VMEM((2, hps, n_slots, page, LANES), F32),
                        pltpu.SemaphoreType.DMA((2, 2))],
    )
    return pl.pallas_call(
        functools.partial(_sample_attend_kernel, t_len=t_len, page=page, ppb=ppb,
                          past_len=n_pages * page, scale=LANES ** -0.5, hps=hps),
        grid_spec=grid_spec,
        out_shape=jax.ShapeDtypeStruct((db, t_len, w), F32),
        compiler_params=_params(("arbitrary", "arbitrary")),
        name="sample_attend",
    )(page_table, idx, ok, q_s, k_s, v_s, cache_k, cache_v)


def _ssd_sample_kernel(xbc_ref, sc_ref, dtr_ref, z_ref, h0_ref, cw_ref, cb_ref, dtb_ref, alog_ref,
                       dskip_ref, sg_ref, expand_ref, s_ref, convs_ref, h_ref, xp_ref,
                       *, cw, t_len, ssm_w, groups, n_state):
    p_dim = SSM_HEAD_DIM
    heads = ssm_w // p_dim
    hpg = heads // groups
    gw = hpg * p_dim
    tail = cw - 1
    xp_ref[0:tail, :] = sc_ref[...]
    xp_ref[tail:tail + t_len, :] = xbc_ref[...]
    conv = cb_ref[...]
    for i in range(cw):
        conv = conv + xp_ref[i:i + t_len, :] * cw_ref[i:i + 1, :]
    act = _silu(conv)
    convs_ref[...] = xp_ref[t_len:t_len + tail, :]

    dt = _softplus(dtr_ref[...] + dtb_ref[...])
    da = dt * (-jnp.exp(alog_ref[...]))
    cum = [da[0:1]]
    for t in range(1, t_len):
        cum.append(cum[-1] + da[t:t + 1])
    lane = lax.broadcasted_iota(jnp.int32, (1, LANES), 1)
    xs = act[:, :ssm_w]
    zpad = jnp.zeros((SUBLANES - t_len, n_state), F32)

    b8, c8, cbs = [], [], []
    for g in range(groups):
        bg = act[:, ssm_w + g * n_state:ssm_w + (g + 1) * n_state]
        cg = act[:, ssm_w + (groups + g) * n_state:ssm_w + (groups + g + 1) * n_state]
        b8.append(jnp.concatenate([bg, zpad], axis=0).astype(BF16))
        c8.append(jnp.concatenate([cg, zpad], axis=0).astype(BF16))
        cbs.append(lax.dot_general(c8[g], b8[g], _NT, preferred_element_type=F32))

    pairs = [(l, s) for l in range(t_len) for s in range(l + 1)]
    rows = []
    for l, s in pairs:
        cb_ls = jnp.zeros((1, LANES), F32)
        for g in range(groups):
            in_g = (lane >= g * hpg) & (lane < (g + 1) * hpg)
            cb_ls = jnp.where(in_g, cbs[g][l:l + 1, s:s + 1], cb_ls)
        rows.append(cb_ls * jnp.exp(cum[l] - cum[s]) * dt[s:s + 1])
    rows += [jnp.exp(cum[l]) for l in range(t_len)]
    rows += [jnp.exp(cum[-1] - cum[s]) * dt[s:s + 1] for s in range(t_len)]
    n_rows = -(-len(rows) // SUBLANES) * SUBLANES
    rows.append(jnp.zeros((n_rows - len(rows), LANES), F32))
    rx = jnp.dot(jnp.concatenate(rows, axis=0), expand_ref[...], precision=lax.Precision.HIGHEST,
                 preferred_element_type=F32)
    r_ecum = len(pairs)
    r_coef = r_ecum + t_len

    y_diag = []
    for l in range(t_len):
        acc = jnp.zeros((1, ssm_w), F32)
        for s in range(l + 1):
            i = pairs.index((l, s))
            acc = acc + rx[i:i + 1, :] * xs[s:s + 1, :]
        y_diag.append(acc)
    y = dskip_ref[...] * xs + jnp.concatenate(y_diag, axis=0)

    y_off = []
    for g in range(groups):
        hg = h0_ref[g * hpg:(g + 1) * hpg].reshape(gw, n_state)
        y_off.append(lax.dot_general(c8[g], hg.astype(BF16), _NT, preferred_element_type=F32)[:t_len])
    y = y + jnp.concatenate(y_off, axis=1) * rx[r_ecum:r_ecum + t_len, :]
    for g in range(groups):
        gl = slice(g * gw, (g + 1) * gw)
        s_ref[:, gl] = _gated_norm(y[:, gl], z_ref[:, gl], sg_ref[:, gl])

    e_last = jnp.exp(cum[-1])
    xw = xs * rx[r_coef:r_coef + t_len, :]
    for g in range(groups):
        xw8 = jnp.concatenate([xw[:, g * gw:(g + 1) * gw], jnp.zeros((SUBLANES - t_len, gw), F32)], axis=0)
        st = lax.dot_general(xw8.astype(BF16), b8[g], _TN, preferred_element_type=F32)
        for ee in range(hpg):
            e = g * hpg + ee
            h_ref[e] = h0_ref[e] * e_last[:, e:e + 1] + st[ee * p_dim:(ee + 1) * p_dim, :]


def _ssd_sample(xbc, state_conv, dtr, z, state_ssm, conv_w, conv_b, dtb, alog, dskip_x, ssm_g, expand,
                *, ssm_w, groups, n_state):
    db, t_len, cc = xbc.shape
    cw = conv_w.shape[0]
    heads = ssm_w // SSM_HEAD_DIM
    const = lambda shape: pl.BlockSpec(shape, lambda bi: (0,) * len(shape))
    return pl.pallas_call(
        functools.partial(_ssd_sample_kernel, cw=cw, t_len=t_len, ssm_w=ssm_w, groups=groups,
                          n_state=n_state),
        grid=(db,),
        in_specs=[
            pl.BlockSpec((None, t_len, cc), lambda bi: (bi, 0, 0)),
            pl.BlockSpec((None, cw - 1, cc), lambda bi: (bi, 0, 0)),
            pl.BlockSpec((None, t_len, LANES), lambda bi: (bi, 0, 0)),
            pl.BlockSpec((None, t_len, ssm_w), lambda bi: (bi, 0, 0)),
            pl.BlockSpec((None, heads, SSM_HEAD_DIM, n_state), lambda bi: (bi, 0, 0, 0)),
            const((cw, cc)), const((1, cc)), const((1, LANES)), const((1, LANES)), const((1, ssm_w)),
            const((1, ssm_w)), const((LANES, ssm_w)),
        ],
        out_specs=(
            pl.BlockSpec((None, t_len, ssm_w), lambda bi: (bi, 0, 0)),
            pl.BlockSpec((None, cw - 1, cc), lambda bi: (bi, 0, 0)),
            pl.BlockSpec((None, heads, SSM_HEAD_DIM, n_state), lambda bi: (bi, 0, 0, 0)),
        ),
        out_shape=(
            jax.ShapeDtypeStruct((db, t_len, ssm_w), BF16),
            jax.ShapeDtypeStruct((db, cw - 1, cc), F32),
            jax.ShapeDtypeStruct((db, heads, SSM_HEAD_DIM, n_state), F32),
        ),
        scratch_shapes=[pltpu.VMEM((2 * SUBLANES, cc), F32)],
        compiler_params=_params(("parallel",)),
        name="ssd_sample",
    )(xbc, state_conv, dtr, z, state_ssm, conv_w, conv_b, dtb, alog, dskip_x, ssm_g, expand)


def _row_tile(m, target):
    tm = min(m, target)
    assert m % tm == 0, (m, tm)
    return tm


def _page_split(n_phys, moba_steps, nb, ssd_steps):
    per_moba = max(1, n_phys * 6 // 10 // moba_steps)
    rest = max(n_phys - moba_steps * per_moba, 0)
    per_ssd = max(1, -(-rest // ssd_steps))
    return per_moba, per_ssd


def kernel(x_prompt, x_sample, cache_k, cache_v, page_table, state_conv, state_ssm, norm_mix_g, w_in, q_norm_g, k_norm_g, conv_w, conv_b, dt_bias, a_log, d_skip, attn_out_g, ssm_out_g, w_out, norm_ffn_g, w_gate, w_up, w_down):
    b, s, d = x_prompt.shape
    db, t_len, _ = x_sample.shape
    n_phys, page, n_heads, dh = cache_k.shape
    attn_w = n_heads * dh
    heads, p_dim, n_state = state_ssm.shape[1:]
    assert p_dim == SSM_HEAD_DIM and dh == LANES and heads <= LANES
    ssm_w = heads * p_dim
    conv_ch = state_conv.shape[2]
    groups = (conv_ch - ssm_w) // (2 * n_state)
    n_main = 3 * attn_w + ssm_w + conv_ch

    w_dt = jnp.pad(w_in[:, n_main:].astype(BF16), ((0, 0), (0, LANES - heads)))
    w_out_b, w_gate_b, w_up_b, w_down_b = (w.astype(BF16) for w in (w_out, w_gate, w_up, w_down))
    row = lambda v: v.reshape(1, -1).astype(F32)
    padl = lambda v: jnp.pad(v.astype(F32), (0, LANES - v.shape[0])).reshape(1, LANES)
    dskip_x = jnp.repeat(d_skip.astype(F32), p_dim).reshape(1, ssm_w)
    expand = (jnp.arange(LANES)[:, None] == (jnp.arange(ssm_w)[None, :] // p_dim)).astype(F32)
    in_args = (row(norm_mix_g), w_in, w_dt, row(q_norm_g), row(k_norm_g))
    in_kw = dict(attn_w=attn_w, ssm_w=ssm_w, conv_ch=conv_ch)
    ssd_kw = dict(ssm_w=ssm_w, groups=groups, n_state=n_state)
    ssd_par = (conv_w, row(conv_b), padl(dt_bias), padl(a_log), dskip_x)

    xp2 = x_prompt.reshape(b * s, d)
    q, k, v, z, xbc, dtr = _inproj(xp2, *in_args, tm=_row_tile(b * s, 1024), **in_kw)
    moba_steps, ssd_steps = b * n_heads, b * (s // SSD_CHUNK)
    pg_moba, pg_ssd = _page_split(n_phys, moba_steps, s // MOBA_BLOCK, ssd_steps)
    ps_ssd0 = n_phys - ssd_steps * pg_ssd
    assert 0 <= ps_ssd0 <= moba_steps * pg_moba <= n_phys
    attn, ps_a = _moba_prompt(q.reshape(b, s, attn_w), k.reshape(b, s, attn_w), v.reshape(b, s, attn_w),
                              cache_k, n_heads=n_heads, pages_per_step=pg_moba)
    sn, conv_p, ssm_p, ps_b = _ssd_prompt(xbc.reshape(b, s, conv_ch), dtr.reshape(b, s, LANES),
                                          z.reshape(b, s, ssm_w), *ssd_par, row(ssm_out_g), expand, cache_k,
                                          page0=ps_ssd0, pages_per_step=pg_ssd, **ssd_kw)
    h_p = _outproj(attn.reshape(b * s, attn_w), sn.reshape(b * s, ssm_w), xp2, row(attn_out_g), w_out_b,
                   tm=_row_tile(b * s, 512))
    y_p = _ffn(h_p, row(norm_ffn_g), w_gate_b, w_up_b, w_down_b,
               tm=_row_tile(b * s, 1024)).reshape(b, s, d)
    page_shape = (b, s // page, page, n_heads, dh)
    k_p, v_p = k.reshape(page_shape), v.reshape(page_shape)

    ms = db * t_len
    xs2 = x_sample.reshape(ms, d)
    qs, k_s, v_s, zs, xbcs, dtrs = _inproj(xs2, *in_args, tm=_row_tile(ms, 1024), **in_kw)
    qs3, ks3, vs3 = (a.reshape(db, t_len, attn_w) for a in (qs, k_s, v_s))
    sel = _sample_select(page_table, ps_a, ps_b, qs3, n_heads=n_heads, page=page, split=ps_ssd0)
    idx = sel[:, :, :t_len, :MOBA_TOP_K].reshape(db, n_heads * t_len * MOBA_TOP_K)
    ok = sel[:, :, :t_len, MOBA_TOP_K:2 * MOBA_TOP_K].reshape(db, n_heads * t_len * MOBA_TOP_K)
    attn_s = _sample_attend(page_table, idx, ok, qs3, ks3, vs3, cache_k, cache_v, n_heads=n_heads)
    sns, conv_s, ssm_s = _ssd_sample(xbcs.reshape(db, t_len, conv_ch), state_conv,
                                     dtrs.reshape(db, t_len, LANES), zs.reshape(db, t_len, ssm_w), state_ssm,
                                     *ssd_par, row(ssm_out_g), expand, **ssd_kw)
    tms = _row_tile(ms, 512)
    h_s = _outproj(attn_s.reshape(ms, attn_w), sns.reshape(ms, ssm_w), xs2, row(attn_out_g), w_out_b, tm=tms)
    y_s = _ffn(h_s, row(norm_ffn_g), w_gate_b, w_up_b, w_down_b, tm=tms).reshape(db, t_len, d)
    tok_shape = (db, t_len, n_heads, dh)
    return (y_p, y_s, k_p, v_p, conv_p, ssm_p, k_s.reshape(tok_shape), v_s.reshape(tok_shape),
            conv_s, ssm_s)
```

```python
import functools

import jax
import jax.numpy as jnp
from jax import lax
from jax.experimental import pallas as pl
from jax.experimental.pallas import tpu as pltpu

F32 = jnp.float32
BF16 = jnp.bfloat16
EPS = 1e-6
NEG = -1e30

MOBA_BLOCK = 256
MOBA_TOP_K = 3
SSD_CHUNK = 128
SSM_HEAD_DIM = 64
LANES = 128
SUBLANES = 8
VMEM_LIMIT = 56 * 1024 * 1024
BIG_VMEM_LIMIT = 60 * 1024 * 1024

_NT = (((1,), (1,)), ((), ()))
_TN = (((0,), (0,)), ((), ()))


def _params(sem, vmem=VMEM_LIMIT):
    return pltpu.CompilerParams(dimension_semantics=sem, vmem_limit_bytes=vmem)


def _rms(x, g):
    return x * lax.rsqrt(jnp.mean(x * x, axis=-1, keepdims=True) + EPS) * g


def _silu(x):
    return x * (1.0 / (1.0 + jnp.exp(-x)))


def _softplus(x):
    return jnp.maximum(x, 0.0) + jnp.log1p(jnp.exp(-jnp.abs(x)))


def _alibi_slope(h):
    bits = (jnp.full((1, 1), 126, jnp.int32) - h) << 23
    return lax.bitcast_convert_type(bits, F32)


def _inproj_kernel(x_ref, g_ref, w_ref, wdt_ref, qg_ref, kg_ref,
                   q_ref, k_ref, v_ref, z_ref, xbc_ref, dt_ref, xn_ref, *, tn, seg):
    j = pl.program_id(1)

    @pl.when(j == 0)
    def _():
        xn_ref[...] = _rms(x_ref[...], g_ref[...]).astype(BF16)
        dt_ref[...] = jnp.dot(xn_ref[...], wdt_ref[...], preferred_element_type=F32)

    acc = jnp.dot(xn_ref[...], w_ref[...].astype(BF16), preferred_element_type=F32)

    def headnorm(out_ref, g):
        for hh in range(tn // LANES):
            sl = slice(hh * LANES, (hh + 1) * LANES)
            out_ref[:, sl] = _rms(acc[:, sl], g)

    @pl.when(j < seg[0])
    def _():
        headnorm(q_ref, qg_ref[...])

    @pl.when((j >= seg[0]) & (j < seg[1]))
    def _():
        headnorm(k_ref, kg_ref[...])

    @pl.when((j >= seg[1]) & (j < seg[2]))
    def _():
        v_ref[...] = acc

    @pl.when((j >= seg[2]) & (j < seg[3]))
    def _():
        z_ref[...] = acc

    @pl.when(j >= seg[3])
    def _():
        xbc_ref[...] = acc


def _inproj(x2, norm_g, w_all, w_dt, q_g, k_g, *, attn_w, ssm_w, conv_ch, tm, tn=512):
    m, d = x2.shape
    n_main = 3 * attn_w + ssm_w + conv_ch
    assert n_main % tn == 0
    seg = (attn_w // tn, 2 * attn_w // tn, 3 * attn_w // tn, (3 * attn_w + ssm_w) // tn)
    nj = n_main // tn

    def col(lo, hi):
        return lambda i, j: (i, jnp.clip(j - lo, 0, hi - lo - 1))

    out_shape = (
        jax.ShapeDtypeStruct((m, attn_w), F32), jax.ShapeDtypeStruct((m, attn_w), F32),
        jax.ShapeDtypeStruct((m, attn_w), F32), jax.ShapeDtypeStruct((m, ssm_w), F32),
        jax.ShapeDtypeStruct((m, conv_ch), F32), jax.ShapeDtypeStruct((m, LANES), F32))
    return pl.pallas_call(
        functools.partial(_inproj_kernel, tn=tn, seg=seg),
        grid=(m // tm, nj),
        in_specs=[
            pl.BlockSpec((tm, d), lambda i, j: (i, 0)),
            pl.BlockSpec((1, d), lambda i, j: (0, 0)),
            pl.BlockSpec((d, tn), lambda i, j: (0, j)),
            pl.BlockSpec((d, LANES), lambda i, j: (0, 0)),
            pl.BlockSpec((1, LANES), lambda i, j: (0, 0)),
            pl.BlockSpec((1, LANES), lambda i, j: (0, 0)),
        ],
        out_specs=(
            pl.BlockSpec((tm, tn), col(0, seg[0])),
            pl.BlockSpec((tm, tn), col(seg[0], seg[1])),
            pl.BlockSpec((tm, tn), col(seg[1], seg[2])),
            pl.BlockSpec((tm, tn), col(seg[2], seg[3])),
            pl.BlockSpec((tm, tn), col(seg[3], nj)),
            pl.BlockSpec((tm, LANES), lambda i, j: (i, 0)),
        ),
        out_shape=out_shape,
        scratch_shapes=[pltpu.VMEM((tm, d), BF16)],
        compiler_params=_params(("parallel", "arbitrary")),
        name="inproj",
    )(x2, norm_g, w_all, w_dt, q_g, k_g)


def _block_select(gate_t, n_past):
    row = lax.broadcasted_iota(jnp.int32, gate_t.shape, 0)
    valid = row < n_past
    if n_past <= MOBA_TOP_K:
        return valid & (gate_t > 0.5 * NEG)
    rank = jnp.zeros(gate_t.shape, jnp.int32)
    for j in range(n_past):
        other = gate_t[j:j + 1, :]
        rank = rank + ((other > gate_t) | ((other == gate_t) & (row > j))).astype(jnp.int32)
    return valid & (rank < MOBA_TOP_K) & (gate_t > 0.5 * NEG)


def _page_chunk_copy(ck_hbm, buf, sem, first_page, n_pages):
    return pltpu.make_async_copy(ck_hbm.at[pl.ds(first_page, n_pages)], buf, sem)


def _page_key_sums(pages, lanes_of_adds=4):
    n, page, heads, dh = pages.shape
    part = pages.reshape(n * lanes_of_adds, page // lanes_of_adds, heads, dh).sum(axis=1)
    return part.reshape(n, lanes_of_adds, heads, dh).sum(axis=1)


def _stream_page_sums(ck_hbm, ps_ref, pbuf, psem, step, n_steps, page0):
    n_pg = pbuf.shape[1]
    slot = step % 2

    def page_copy(st, sl):
        return _page_chunk_copy(ck_hbm, pbuf.at[sl], psem.at[sl], page0 + st * n_pg, n_pg)

    @pl.when(step == 0)
    def _():
        page_copy(0, 0).start()

    page_copy(step, slot).wait()

    @pl.when(step + 1 < n_steps)
    def _():
        page_copy(step + 1, 1 - slot).start()

    ps_ref[...] = _page_key_sums(pbuf[slot])


def _causal_chunks(total, nb):
    weights = [i + 2 for i in range(nb - 1)] + [1]
    cum, acc = [0], 0
    for w in weights:
        acc += w
        cum.append((total * acc + sum(weights) // 2) // sum(weights))
    return [hi - lo for lo, hi in zip(cum[:-1], cum[1:])], cum[:-1]


def _moba_prompt_kernel(q_ref, k_ref, v_ref, ck_hbm, o_ref, ps_ref, pbuf, psem, *, nb, scale, pages):
    blk = MOBA_BLOCK
    h = pl.program_id(1)
    step = pl.program_id(0) * pl.num_programs(1) + h
    n_steps = pl.num_programs(0) * pl.num_programs(1)
    sizes, offs = _causal_chunks(pages, nb)
    chunks = [i for i in range(nb) if sizes[i] > 0]
    n_ch = len(chunks)
    n_slots = pbuf.shape[0]
    ahead = n_slots - 1
    assert n_ch >= ahead

    def page_copy(c):
        st, cc = divmod(c, n_ch)
        i = chunks[cc]
        slot = (step * n_ch + c) % n_slots
        return _page_chunk_copy(ck_hbm, pbuf.at[slot, pl.ds(0, sizes[i])], psem.at[slot],
                                (step + st) * pages + offs[i], sizes[i])

    @pl.when(step == 0)
    def _():
        for c in range(ahead):
            page_copy(c).start()

    slope = _alibi_slope(h)
    k = k_ref[...]
    kb = k.astype(BF16)
    vb = v_ref[...].astype(BF16)
    nb_pad = -(-nb // SUBLANES) * SUBLANES
    means = [jnp.mean(k[j * blk:(j + 1) * blk], axis=0, keepdims=True) for j in range(nb)]
    means = jnp.concatenate(means + [jnp.zeros((nb_pad - nb, LANES), F32)] * (nb_pad > nb), axis=0)
    for i in range(nb):
        qi = q_ref[i * blk:(i + 1) * blk, :]
        nk = (i + 1) * blk
        s = lax.dot_general(qi.astype(BF16), kb[:nk], _NT, preferred_element_type=F32)
        col = lax.broadcasted_iota(jnp.int32, (blk, nk), 1)
        logits = s * scale + slope * col.astype(F32)
        row = lax.broadcasted_iota(jnp.int32, (blk, blk), 0)
        colb = lax.broadcasted_iota(jnp.int32, (blk, blk), 1)
        pieces = []
        if i > 0:
            gate_t = lax.dot_general(means, qi, _NT, precision=lax.Precision.HIGHEST,
                                     preferred_element_type=F32)
            sel_t = _block_select(gate_t, i).astype(F32)
            sel_t = jnp.concatenate([sel_t, jnp.zeros((LANES - nb_pad, blk), F32)], axis=0)
            sel = jnp.concatenate([sel_t[:, r * LANES:(r + 1) * LANES].T
                                   for r in range(blk // LANES)], axis=0)
            pieces = [jnp.broadcast_to(sel[:, j:j + 1], (blk, blk)) for j in range(i)]
        pieces.append((colb <= row).astype(F32))
        allowed = jnp.concatenate(pieces, axis=1) if len(pieces) > 1 else pieces[0]
        logits = jnp.where(allowed > 0.5, logits, NEG)
        m = jnp.max(logits, axis=-1, keepdims=True)
        p = jnp.exp(logits - m)
        denom = jnp.sum(p, axis=-1, keepdims=True)
        out = jnp.dot(p.astype(BF16), vb[:nk], preferred_element_type=F32)
        o_ref[i * blk:(i + 1) * blk, :] = out / denom
        if i in chunks:
            c = chunks.index(i)
            page_copy(c).wait()
            if c + ahead < n_ch:
                page_copy(c + ahead).start()
            else:
                @pl.when(step + 1 < n_steps)
                def _():
                    page_copy(c + ahead).start()
            slot = (step * n_ch + c) % n_slots
            ps_ref[offs[i]:offs[i] + sizes[i]] = _page_key_sums(pbuf[slot, 0:sizes[i]])


def _moba_prompt(q, k, v, cache_k, *, n_heads, pages_per_step):
    b, s, w = q.shape
    dh = w // n_heads
    nb = s // MOBA_BLOCK
    npc = max(_causal_chunks(pages_per_step, nb)[0])
    page, ch, cdh = cache_k.shape[1:]
    spec = pl.BlockSpec((None, s, dh), lambda bi, hi: (bi, 0, hi))
    return pl.pallas_call(
        functools.partial(_moba_prompt_kernel, nb=nb, scale=dh ** -0.5, pages=pages_per_step),
        grid=(b, n_heads),
        in_specs=[spec, spec, spec, pl.BlockSpec(memory_space=pl.ANY)],
        out_specs=(spec, pl.BlockSpec((pages_per_step, ch, cdh), lambda bi, hi: (bi * n_heads + hi, 0, 0))),
        out_shape=(jax.ShapeDtypeStruct((b, s, w), F32),
                   jax.ShapeDtypeStruct((b * n_heads * pages_per_step, ch, cdh), F32)),
        scratch_shapes=[pltpu.VMEM((3, npc, page, ch, cdh), F32), pltpu.SemaphoreType.DMA((3,))],
        compiler_params=_params(("arbitrary", "arbitrary")),
        name="moba_prompt",
    )(q, k, v, cache_k)


def _gated_norm(y, z, g):
    return _rms(y * _silu(z), g).astype(BF16)


def _ssd_prompt_kernel(xbc_ref, dtr_ref, z_ref, cw_ref, cb_ref, dtb_ref, alog_ref, dskip_ref, sg_ref,
                       expand_ref, ck_hbm, s_ref, convp_ref, h_ref, ps_ref, xp_ref, pbuf, psem,
                       *, cw, ssm_w, groups, n_state, page0):
    c = pl.program_id(1)
    nc = pl.num_programs(1)
    _stream_page_sums(ck_hbm, ps_ref, pbuf, psem, pl.program_id(0) * nc + c, pl.num_programs(0) * nc,
                      page0)
    lc = SSD_CHUNK
    p_dim = SSM_HEAD_DIM
    heads = ssm_w // p_dim
    hpg = heads // groups
    gw = hpg * p_dim
    tail = cw - 1

    @pl.when(c == 0)
    def _():
        xp_ref[0:SUBLANES, :] = jnp.zeros((SUBLANES, xp_ref.shape[1]), F32)
        h_ref[...] = jnp.zeros(h_ref.shape, F32)

    xc = xbc_ref[...]
    xp_ref[SUBLANES:SUBLANES + lc, :] = xc
    conv = cb_ref[...]
    for i in range(cw):
        conv = conv + xp_ref[SUBLANES - tail + i:SUBLANES - tail + i + lc, :] * cw_ref[i:i + 1, :]
    act = _silu(conv)
    xp_ref[SUBLANES - tail:SUBLANES, :] = xc[lc - tail:lc, :]

    @pl.when(c == nc - 1)
    def _():
        convp_ref[...] = xc[lc - tail:lc, :]

    dt = _softplus(dtr_ref[...] + dtb_ref[...])
    a = -jnp.exp(alog_ref[...])
    row = lax.broadcasted_iota(jnp.int32, (lc, lc), 0)
    colm = lax.broadcasted_iota(jnp.int32, (lc, lc), 1)
    causal = row >= colm
    tri = causal.astype(F32)
    cum = jnp.dot(tri, dt * a, precision=lax.Precision.HIGHEST, preferred_element_type=F32)
    cum_t = cum.T
    dt_t = dt.T
    cum_last = cum[lc - 1:lc, :]
    dec_end = jnp.exp(cum_last - cum) * dt
    ecum = jnp.exp(cum)
    expand = expand_ref[...]
    ecum_x = jnp.dot(ecum, expand, precision=lax.Precision.HIGHEST, preferred_element_type=F32)
    dec_end_x = jnp.dot(dec_end, expand, precision=lax.Precision.HIGHEST,
                        preferred_element_type=F32)

    xs = act[:, :ssm_w]
    xs_b = xs.astype(BF16)
    y = dskip_ref[...] * xs
    lane = lax.broadcasted_iota(jnp.int32, (lc, 2 * p_dim), 1)
    for g in range(groups):
        bg = act[:, ssm_w + g * n_state:ssm_w + (g + 1) * n_state].astype(BF16)
        cg = act[:, ssm_w + (groups + g) * n_state:ssm_w + (groups + g + 1) * n_state].astype(BF16)
        cb = lax.dot_general(cg, bg, _NT, preferred_element_type=F32)
        hg = h_ref[g * hpg:(g + 1) * hpg].reshape(gw, n_state)
        y_off = lax.dot_general(cg, hg.astype(BF16), _NT, preferred_element_type=F32)
        y_off = y_off * ecum_x[:, g * gw:(g + 1) * gw]
        y_diag = []
        for pr in range(hpg // 2):
            ws = []
            for e in (g * hpg + 2 * pr, g * hpg + 2 * pr + 1):
                diff = cum[:, e:e + 1] - cum_t[e:e + 1, :]
                decay = jnp.where(causal, jnp.exp(diff), 0.0)
                ws.append((cb * decay * dt_t[e:e + 1, :]).astype(BF16))
            lo = (g * hpg + 2 * pr) * p_dim
            xpair = xs_b[:, lo:lo + 2 * p_dim]
            zero = jnp.zeros_like(xpair)
            rhs = jnp.concatenate([jnp.where(lane < p_dim, xpair, zero),
                                   jnp.where(lane >= p_dim, xpair, zero)], axis=0)
            y_diag.append(jnp.dot(jnp.concatenate(ws, axis=1), rhs, preferred_element_type=F32))
        y_g = jnp.concatenate(y_diag, axis=1) + y_off
        xw = (xs[:, g * gw:(g + 1) * gw] * dec_end_x[:, g * gw:(g + 1) * gw]).astype(BF16)
        st = lax.dot_general(xw, bg, _TN, preferred_element_type=F32)
        for ee in range(hpg):
            e = g * hpg + ee
            cd = jnp.exp(cum_t[e:e + 1, lc - 1:lc])
            h_ref[e] = h_ref[e] * cd + st[ee * p_dim:(ee + 1) * p_dim, :]
        gl = slice(g * gw, (g + 1) * gw)
        s_ref[:, gl] = _gated_norm(y[:, gl] + y_g, z_ref[:, gl], sg_ref[:, gl])


def _ssd_prompt(xbc, dtr, z, conv_w, conv_b, dtb, alog, dskip_x, ssm_g, expand, cache_k,
                *, ssm_w, groups, n_state, page0, pages_per_step):
    b, s, cc = xbc.shape
    cw = conv_w.shape[0]
    heads = ssm_w // SSM_HEAD_DIM
    lc = SSD_CHUNK
    nc = s // lc
    page, ch, cdh = cache_k.shape[1:]
    const = lambda shape: pl.BlockSpec(shape, lambda bi, ci: (0,) * len(shape))
    return pl.pallas_call(
        functools.partial(_ssd_prompt_kernel, cw=cw, ssm_w=ssm_w, groups=groups, n_state=n_state,
                          page0=page0),
        grid=(b, nc),
        in_specs=[
            pl.BlockSpec((None, lc, cc), lambda bi, ci: (bi, ci, 0)),
            pl.BlockSpec((None, lc, LANES), lambda bi, ci: (bi, ci, 0)),
            pl.BlockSpec((None, lc, ssm_w), lambda bi, ci: (bi, ci, 0)),
            const((cw, cc)), const((1, cc)), const((1, LANES)), const((1, LANES)),
            const((1, ssm_w)), const((1, ssm_w)), const((LANES, ssm_w)),
            pl.BlockSpec(memory_space=pl.ANY),
        ],
        out_specs=(
            pl.BlockSpec((None, lc, ssm_w), lambda bi, ci: (bi, ci, 0)),
            pl.BlockSpec((None, cw - 1, cc), lambda bi, ci: (bi, 0, 0)),
            pl.BlockSpec((None, heads, SSM_HEAD_DIM, n_state), lambda bi, ci: (bi, 0, 0, 0)),
            pl.BlockSpec((pages_per_step, ch, cdh), lambda bi, ci: (bi * nc + ci, 0, 0)),
        ),
        out_shape=(
            jax.ShapeDtypeStruct((b, s, ssm_w), BF16),
            jax.ShapeDtypeStruct((b, cw - 1, cc), F32),
            jax.ShapeDtypeStruct((b, heads, SSM_HEAD_DIM, n_state), F32),
            jax.ShapeDtypeStruct((b * nc * pages_per_step, ch, cdh), F32),
        ),
        scratch_shapes=[pltpu.VMEM((SUBLANES + lc, cc), F32),
                        pltpu.VMEM((2, pages_per_step, page, ch, cdh), F32),
                        pltpu.SemaphoreType.DMA((2,))],
        compiler_params=_params(("arbitrary", "arbitrary")),
        name="ssd_prompt",
    )(xbc, dtr, z, conv_w, conv_b, dtb, alog, dskip_x, ssm_g, expand, cache_k)


def _outproj_kernel(attn_ref, s_ref, x_ref, ag_ref, w_ref, o_ref, *, rc):
    for r in range(o_ref.shape[0] // rc):
        rows = slice(r * rc, (r + 1) * rc)
        a = _rms(attn_ref[rows, :], ag_ref[...]).astype(BF16)
        mix = jnp.concatenate([a, s_ref[rows, :]], axis=1)
        o_ref[rows, :] = x_ref[rows, :] + jnp.dot(mix, w_ref[...], preferred_element_type=F32)


def _outproj(attn, s_norm, x2, attn_g, w_out, *, tm, rc=256):
    m, attn_w = attn.shape
    ssm_w = s_norm.shape[1]
    d = x2.shape[1]
    return pl.pallas_call(
        functools.partial(_outproj_kernel, rc=min(rc, tm)),
        grid=(m // tm,),
        in_specs=[
            pl.BlockSpec((tm, attn_w), lambda i: (i, 0)),
            pl.BlockSpec((tm, ssm_w), lambda i: (i, 0)),
            pl.BlockSpec((tm, d), lambda i: (i, 0)),
            pl.BlockSpec((1, attn_w), lambda i: (0, 0)),
            pl.BlockSpec((attn_w + ssm_w, d), lambda i: (0, 0)),
        ],
        out_specs=pl.BlockSpec((tm, d), lambda i: (i, 0)),
        out_shape=jax.ShapeDtypeStruct((m, d), F32),
        compiler_params=_params(("parallel",)),
        name="outproj",
    )(attn, s_norm, x2, attn_g, w_out)


def _ffn_kernel(h_ref, g_ref, wg_ref, wu_ref, wd_ref, o_ref, hn_ref, *, rc):
    f = pl.program_id(1)

    @pl.when(f == 0)
    def _():
        hn_ref[...] = _rms(h_ref[...], g_ref[...]).astype(BF16)
        o_ref[...] = h_ref[...]

    for r in range(o_ref.shape[0] // rc):
        rows = slice(r * rc, (r + 1) * rc)
        hn = hn_ref[rows, :]
        gate = jnp.dot(hn, wg_ref[...], preferred_element_type=F32)
        up = jnp.dot(hn, wu_ref[...], preferred_element_type=F32)
        o_ref[rows, :] += jnp.dot((_silu(gate) * up).astype(BF16), wd_ref[...],
                                  preferred_element_type=F32)


def _ffn(h, g, w_gate, w_up, w_down, *, tm, tf=512, rc=512):
    m, d = h.shape
    dff = w_gate.shape[1]
    return pl.pallas_call(
        functools.partial(_ffn_kernel, rc=min(rc, tm)),
        grid=(m // tm, dff // tf),
        in_specs=[
            pl.BlockSpec((tm, d), lambda i, f: (i, 0)),
            pl.BlockSpec((1, d), lambda i, f: (0, 0)),
            pl.BlockSpec((d, tf), lambda i, f: (0, f)),
            pl.BlockSpec((d, tf), lambda i, f: (0, f)),
            pl.BlockSpec((tf, d), lambda i, f: (f, 0)),
        ],
        out_specs=pl.BlockSpec((tm, d), lambda i, f: (i, 0)),
        out_shape=jax.ShapeDtypeStruct((m, d), F32),
        scratch_shapes=[pltpu.VMEM((tm, d), BF16)],
        compiler_params=_params(("parallel", "arbitrary"), vmem=BIG_VMEM_LIMIT),
        name="ffn",
    )(h, g, w_gate, w_up, w_down)


def _sample_select_kernel(pt_ref, psa_ref, psb_ref, q_ref, sel_ref, blk_ref,
                          *, nb, ppb, n_heads, t_len, split):
    b = pl.program_id(0)

    def page_sum(pg):
        lo = psa_ref[jnp.minimum(pg, split - 1)]
        hi = psb_ref[jnp.maximum(pg - split, 0)]
        return jnp.where(pg < split, lo, hi)

    for n in range(nb):
        acc = page_sum(pt_ref[b, n * ppb])
        for pp in range(1, ppb):
            acc = acc + page_sum(pt_ref[b, n * ppb + pp])
        blk_ref[n] = acc / MOBA_BLOCK
    lane = lax.broadcasted_iota(jnp.int32, (SUBLANES, LANES), 1)
    for h in range(n_heads):
        qh = q_ref[:, h * LANES:(h + 1) * LANES]
        q8 = jnp.concatenate([qh, jnp.zeros((SUBLANES - t_len, LANES), F32)], axis=0)
        bm = blk_ref[:, h, :]
        gate = lax.dot_general(q8, bm, _NT, precision=lax.Precision.HIGHEST,
                               preferred_element_type=F32)
        lane_n = lax.broadcasted_iota(jnp.int32, gate.shape, 1).astype(F32)
        out = jnp.zeros((SUBLANES, LANES), jnp.int32)
        for kk in range(MOBA_TOP_K):
            mx = jnp.max(gate, axis=-1, keepdims=True)
            idx_f = jnp.min(jnp.where(gate == mx, lane_n, float(nb)), axis=-1, keepdims=True)
            idx = idx_f.astype(jnp.int32)
            ok = (mx > 0.5 * NEG).astype(jnp.int32)
            out = jnp.where(lane == kk, idx, out)
            out = jnp.where(lane == MOBA_TOP_K + kk, ok, out)
            gate = jnp.where(lane_n == idx_f, -jnp.inf, gate)
        sel_ref[h] = out


def _sample_select(page_table, ps_lo, ps_hi, q_s, *, n_heads, page, split):
    db, t_len, w = q_s.shape
    n_pages = page_table.shape[1]
    _, h, dh = ps_lo.shape
    ppb = MOBA_BLOCK // page
    nb = n_pages // ppb
    whole = lambda a: pl.BlockSpec(a.shape, lambda bi, pt: (0, 0, 0), pipeline_mode=pl.Buffered(1))
    grid_spec = pltpu.PrefetchScalarGridSpec(
        num_scalar_prefetch=1,
        grid=(db,),
        in_specs=[whole(ps_lo), whole(ps_hi),
                  pl.BlockSpec((None, t_len, w), lambda bi, pt: (bi, 0, 0))],
        out_specs=pl.BlockSpec((None, n_heads, SUBLANES, LANES), lambda bi, pt: (bi, 0, 0, 0)),
        scratch_shapes=[pltpu.VMEM((nb, h, dh), F32)],
    )
    return pl.pallas_call(
        functools.partial(_sample_select_kernel, nb=nb, ppb=ppb, n_heads=n_heads, t_len=t_len,
                          split=split),
        grid_spec=grid_spec,
        out_shape=jax.ShapeDtypeStruct((db, n_heads, SUBLANES, LANES), jnp.int32),
        compiler_params=_params(("arbitrary",)),
        name="sample_select",
    )(page_table, ps_lo, ps_hi, q_s)


def _sample_attend_kernel(pt_ref, idx_ref, ok_ref, q_ref, kn_ref, vn_ref, ck_hbm, cv_hbm, o_ref,
                          kbuf, vbuf, sem, *, t_len, page, ppb, past_len, scale, hps):
    b = pl.program_id(0)
    hg = pl.program_id(1)
    n_hg = pl.num_programs(1)
    step = b * n_hg + hg
    cur = step % 2
    n_sel = t_len * MOBA_TOP_K * ppb
    n_pages = past_len // page

    def copies(st, hl, hh, slot, phys):
        return (pltpu.make_async_copy(ck_hbm.at[phys, :, hh, :], kbuf.at[st, hl, slot], sem.at[st, 0]),
                pltpu.make_async_copy(cv_hbm.at[phys, :, hh, :], vbuf.at[st, hl, slot], sem.at[st, 1]))

    def slot_pages(bb, hh):
        out = []
        for t in range(t_len):
            for kk in range(MOBA_TOP_K):
                n = idx_ref[bb, (hh * t_len + t) * MOBA_TOP_K + kk]
                for pp in range(ppb):
                    lp = jnp.clip(n * ppb + pp, 0, n_pages - 1)
                    out.append(((t * MOBA_TOP_K + kk) * ppb + pp, pt_ref[bb, lp], n * ppb + pp))
        out.append((n_sel, pt_ref[bb, n_pages - 1], None))
        return out

    def start_all(st, bb, hgg):
        for hl in range(hps):
            hh = hgg * hps + hl
            for slot, phys, _ in slot_pages(bb, hh):
                for cp in copies(st, hl, hh, slot, phys):
                    cp.start()

    @pl.when(step == 0)
    def _():
        start_all(0, b, hg)

    wrap = hg + 1 == n_hg
    b_next = b + wrap.astype(jnp.int32)
    hg_next = jnp.where(wrap, 0, hg + 1)

    @pl.when(step + 1 < pl.num_programs(0) * n_hg)
    def _():
        start_all(1 - cur, b_next, hg_next)

    n_keys = (n_sel + 2) * page
    lane = lax.broadcasted_iota(jnp.int32, (1, page), 1)
    trow = lax.broadcasted_iota(jnp.int32, (SUBLANES, n_keys), 0)
    kcol = lax.broadcasted_iota(jnp.int32, (SUBLANES, n_keys), 1)
    q_pos = past_len + trow
    blk_start = (q_pos // MOBA_BLOCK) * MOBA_BLOCK
    per_t = MOBA_TOP_K * ppb * page
    sel_cols = n_sel * page
    pad = jnp.zeros((page - t_len, LANES), F32)

    head_pages = [slot_pages(b, hg * hps + hl) for hl in range(hps)]
    for hl in range(hps):
        hcols = slice(hl * LANES, (hl + 1) * LANES)
        kbuf[cur, hl, n_sel + 1] = jnp.concatenate([kn_ref[:, hcols], pad], axis=0)
        vbuf[cur, hl, n_sel + 1] = jnp.concatenate([vn_ref[:, hcols], pad], axis=0)

    for hl in range(hps):
        for slot, phys, _ in head_pages[hl]:
            for cp in copies(cur, hl, hg * hps + hl, slot, phys):
                cp.wait()

    for hl in range(hps):
        h = hg * hps + hl
        hcols = slice(hl * LANES, (hl + 1) * LANES)
        pos, okv = [], []
        for slot, _, lp in head_pages[hl][:-1]:
            n_ok = ok_ref[b, (h * t_len + slot // (MOBA_TOP_K * ppb)) * MOBA_TOP_K
                          + (slot // ppb) % MOBA_TOP_K]
            pos.append(lane + jnp.clip(lp, 0, n_pages - 1) * page)
            okv.append(jnp.broadcast_to(n_ok, (1, page)))
        pos.append(lane + (past_len - page))
        pos.append(lane + past_len)
        pos = jnp.concatenate(pos, axis=1)
        okv = jnp.concatenate(okv + [jnp.zeros((1, n_keys - sel_cols), jnp.int32)], axis=1)

        kall = kbuf[cur, hl].reshape(n_keys, LANES).astype(BF16)
        vall = vbuf[cur, hl].reshape(n_keys, LANES).astype(BF16)
        q8 = jnp.concatenate([q_ref[:, hcols], jnp.zeros((SUBLANES - t_len, LANES), F32)], axis=0)
        s = lax.dot_general(q8.astype(BF16), kall, _NT, preferred_element_type=F32)

        dist = (q_pos - pos).astype(F32)
        logits = s * scale - _alibi_slope(h) * dist
        sel_ok = (kcol // per_t == trow) & (okv > 0)
        own_ok = ((kcol >= sel_cols) & (pos <= q_pos) & (pos >= blk_start)
                  & (kcol < sel_cols + page + t_len))
        logits = jnp.where(sel_ok | own_ok, logits, NEG)
        m = jnp.max(logits, axis=-1, keepdims=True)
        p = jnp.exp(logits - m)
        denom = jnp.sum(p, axis=-1, keepdims=True)
        out = jnp.dot(p.astype(BF16), vall, preferred_element_type=F32) / denom
        o_ref[:, hcols] = out[:t_len, :]


def _sample_attend(page_table, idx, ok, q_s, k_s, v_s, cache_k, cache_v, *, n_heads, hps=4):
    db, t_len, w = q_s.shape
    page = cache_k.shape[1]
    n_pages = page_table.shape[1]
    ppb = MOBA_BLOCK // page
    n_slots = t_len * MOBA_TOP_K * ppb + 2
    assert n_heads % hps == 0
    tok = pl.BlockSpec((None, t_len, hps * LANES), lambda bi, hi, *_: (bi, 0, hi))
    grid_spec = pltpu.PrefetchScalarGridSpec(
        num_scalar_prefetch=3,
        grid=(db, n_heads // hps),
        in_specs=[tok, tok, tok,
                  pl.BlockSpec(memory_space=pl.ANY), pl.BlockSpec(memory_space=pl.ANY)],
        out_specs=tok,
        scratch_shapes=[pltpu.VMEM((2, hps, n_slots, page, LANES), F32),
                        pltpu.VMEM((2, hps, n_slots, page, LANES), F32),
                        pltpu.SemaphoreType.DMA((2, 2))],
    )
    return pl.pallas_call(
        functools.partial(_sample_attend_kernel, t_len=t_len, page=page, ppb=ppb,
                          past_len=n_pages * page, scale=LANES ** -0.5, hps=hps),
        grid_spec=grid_spec,
        out_shape=jax.ShapeDtypeStruct((db, t_len, w), F32),
        compiler_params=_params(("arbitrary", "arbitrary")),
        name="sample_attend",
    )(page_table, idx, ok, q_s, k_s, v_s, cache_k, cache_v)


def _ssd_sample_kernel(xbc_ref, sc_ref, dtr_ref, z_ref, h0_ref, cw_ref, cb_ref, dtb_ref, alog_ref,
                       dskip_ref, sg_ref, expand_ref, s_ref, convs_ref, h_ref, xp_ref,
                       *, cw, t_len, ssm_w, groups, n_state):
    p_dim = SSM_HEAD_DIM
    heads = ssm_w // p_dim
    hpg = heads // groups
    gw = hpg * p_dim
    tail = cw - 1
    xp_ref[0:tail, :] = sc_ref[...]
    xp_ref[tail:tail + t_len, :] = xbc_ref[...]
    conv = cb_ref[...]
    for i in range(cw):
        conv = conv + xp_ref[i:i + t_len, :] * cw_ref[i:i + 1, :]
    act = _silu(conv)
    convs_ref[...] = xp_ref[t_len:t_len + tail, :]

    dt = _softplus(dtr_ref[...] + dtb_ref[...])
    da = dt * (-jnp.exp(alog_ref[...]))
    cum = [da[0:1]]
    for t in range(1, t_len):
        cum.append(cum[-1] + da[t:t + 1])
    lane = lax.broadcasted_iota(jnp.int32, (1, LANES), 1)
    xs = act[:, :ssm_w]
    zpad = jnp.zeros((SUBLANES - t_len, n_state), F32)

    b8, c8, cbs = [], [], []
    for g in range(groups):
        bg = act[:, ssm_w + g * n_state:ssm_w + (g + 1) * n_state]
        cg = act[:, ssm_w + (groups + g) * n_state:ssm_w + (groups + g + 1) * n_state]
        b8.append(jnp.concatenate([bg, zpad], axis=0).astype(BF16))
        c8.append(jnp.concatenate([cg, zpad], axis=0).astype(BF16))
        cbs.append(lax.dot_general(c8[g], b8[g], _NT, preferred_element_type=F32))

    pairs = [(l, s) for l in range(t_len) for s in range(l + 1)]
    rows = []
    for l, s in pairs:
        cb_ls = jnp.zeros((1, LANES), F32)
        for g in range(groups):
            in_g = (lane >= g * hpg) & (lane < (g + 1) * hpg)
            cb_ls = jnp.where(in_g, cbs[g][l:l + 1, s:s + 1], cb_ls)
        rows.append(cb_ls * jnp.exp(cum[l] - cum[s]) * dt[s:s + 1])
    rows += [jnp.exp(cum[l]) for l in range(t_len)]
    rows += [jnp.exp(cum[-1] - cum[s]) * dt[s:s + 1] for s in range(t_len)]
    n_rows = -(-len(rows) // SUBLANES) * SUBLANES
    rows.append(jnp.zeros((n_rows - len(rows), LANES), F32))
    rx = jnp.dot(jnp.concatenate(rows, axis=0), expand_ref[...], precision=lax.Precision.HIGHEST,
                 preferred_element_type=F32)
    r_ecum = len(pairs)
    r_coef = r_ecum + t_len

    y_diag = []
    for l in range(t_len):
        acc = jnp.zeros((1, ssm_w), F32)
        for s in range(l + 1):
            i = pairs.index((l, s))
            acc = acc + rx[i:i + 1, :] * xs[s:s + 1, :]
        y_diag.append(acc)
    y = dskip_ref[...] * xs + jnp.concatenate(y_diag, axis=0)

    y_off = []
    for g in range(groups):
        hg = h0_ref[g * hpg:(g + 1) * hpg].reshape(gw, n_state)
        y_off.append(lax.dot_general(c8[g], hg.astype(BF16), _NT, preferred_element_type=F32)[:t_len])
    y = y + jnp.concatenate(y_off, axis=1) * rx[r_ecum:r_ecum + t_len, :]
    for g in range(groups):
        gl = slice(g * gw, (g + 1) * gw)
        s_ref[:, gl] = _gated_norm(y[:, gl], z_ref[:, gl], sg_ref[:, gl])

    e_last = jnp.exp(cum[-1])
    xw = xs * rx[r_coef:r_coef + t_len, :]
    for g in range(groups):
        xw8 = jnp.concatenate([xw[:, g * gw:(g + 1) * gw], jnp.zeros((SUBLANES - t_len, gw), F32)], axis=0)
        st = lax.dot_general(xw8.astype(BF16), b8[g], _TN, preferred_element_type=F32)
        for ee in range(hpg):
            e = g * hpg + ee
            h_ref[e] = h0_ref[e] * e_last[:, e:e + 1] + st[ee * p_dim:(ee + 1) * p_dim, :]


def _ssd_sample(xbc, state_conv, dtr, z, state_ssm, conv_w, conv_b, dtb, alog, dskip_x, ssm_g, expand,
                *, ssm_w, groups, n_state):
    db, t_len, cc = xbc.shape
    cw = conv_w.shape[0]
    heads = ssm_w // SSM_HEAD_DIM
    const = lambda shape: pl.BlockSpec(shape, lambda bi: (0,) * len(shape))
    return pl.pallas_call(
        functools.partial(_ssd_sample_kernel, cw=cw, t_len=t_len, ssm_w=ssm_w, groups=groups,
                          n_state=n_state),
        grid=(db,),
        in_specs=[
            pl.BlockSpec((None, t_len, cc), lambda bi: (bi, 0, 0)),
            pl.BlockSpec((None, cw - 1, cc), lambda bi: (bi, 0, 0)),
            pl.BlockSpec((None, t_len, LANES), lambda bi: (bi, 0, 0)),
            pl.BlockSpec((None, t_len, ssm_w), lambda bi: (bi, 0, 0)),
            pl.BlockSpec((None, heads, SSM_HEAD_DIM, n_state), lambda bi: (bi, 0, 0, 0)),
            const((cw, cc)), const((1, cc)), const((1, LANES)), const((1, LANES)), const((1, ssm_w)),
            const((1, ssm_w)), const((LANES, ssm_w)),
        ],
        out_specs=(
            pl.BlockSpec((None, t_len, ssm_w), lambda bi: (bi, 0, 0)),
            pl.BlockSpec((None, cw - 1, cc), lambda bi: (bi, 0, 0)),
            pl.BlockSpec((None, heads, SSM_HEAD_DIM, n_state), lambda bi: (bi, 0, 0, 0)),
        ),
        out_shape=(
            jax.ShapeDtypeStruct((db, t_len, ssm_w), BF16),
            jax.ShapeDtypeStruct((db, cw - 1, cc), F32),
            jax.ShapeDtypeStruct((db, heads, SSM_HEAD_DIM, n_state), F32),
        ),
        scratch_shapes=[pltpu.VMEM((2 * SUBLANES, cc), F32)],
        compiler_params=_params(("parallel",)),
        name="ssd_sample",
    )(xbc, state_conv, dtr, z, state_ssm, conv_w, conv_b, dtb, alog, dskip_x, ssm_g, expand)


def _row_tile(m, target):
    tm = min(m, target)
    assert m % tm == 0, (m, tm)
    return tm


def _page_split(n_phys, moba_steps, nb, ssd_steps):
    per_moba = max(1, n_phys * 6 // 10 // moba_steps)
    rest = max(n_phys - moba_steps * per_moba, 0)
    per_ssd = max(1, -(-rest // ssd_steps))
    return per_moba, per_ssd


def kernel(x_prompt, x_sample, cache_k, cache_v, page_table, state_conv, state_ssm, norm_mix_g, w_in, q_norm_g, k_norm_g, conv_w, conv_b, dt_bias, a_log, d_skip, attn_out_g, ssm_out_g, w_out, norm_ffn_g, w_gate, w_up, w_down):
    b, s, d = x_prompt.shape
    db, t_len, _ = x_sample.shape
    n_phys, page, n_heads, dh = cache_k.shape
    attn_w = n_heads * dh
    heads, p_dim, n_state = state_ssm.shape[1:]
    assert p_dim == SSM_HEAD_DIM and dh == LANES and heads <= LANES
    ssm_w = heads * p_dim
    conv_ch = state_conv.shape[2]
    groups = (conv_ch - ssm_w) // (2 * n_state)
    n_main = 3 * attn_w + ssm_w + conv_ch

    w_dt = jnp.pad(w_in[:, n_main:].astype(BF16), ((0, 0), (0, LANES - heads)))
    w_out_b, w_gate_b, w_up_b, w_down_b = (w.astype(BF16) for w in (w_out, w_gate, w_up, w_down))
    row = lambda v: v.reshape(1, -1).astype(F32)
    padl = lambda v: jnp.pad(v.astype(F32), (0, LANES - v.shape[0])).reshape(1, LANES)
    dskip_x = jnp.repeat(d_skip.astype(F32), p_dim).reshape(1, ssm_w)
    expand = (jnp.arange(LANES)[:, None] == (jnp.arange(ssm_w)[None, :] // p_dim)).astype(F32)
    in_args = (row(norm_mix_g), w_in, w_dt, row(q_norm_g), row(k_norm_g))
    in_kw = dict(attn_w=attn_w, ssm_w=ssm_w, conv_ch=conv_ch)
    ssd_kw = dict(ssm_w=ssm_w, groups=groups, n_state=n_state)
    ssd_par = (conv_w, row(conv_b), padl(dt_bias), padl(a_log), dskip_x)

    xp2 = x_prompt.reshape(b * s, d)
    q, k, v, z, xbc, dtr = _inproj(xp2, *in_args, tm=_row_tile(b * s, 1024), **in_kw)
    moba_steps, ssd_steps = b * n_heads, b * (s // SSD_CHUNK)
    pg_moba, pg_ssd = _page_split(n_phys, moba_steps, s // MOBA_BLOCK, ssd_steps)
    ps_ssd0 = n_phys - ssd_steps * pg_ssd
    assert 0 <= ps_ssd0 <= moba_steps * pg_moba <= n_phys
    attn, ps_a = _moba_prompt(q.reshape(b, s, attn_w), k.reshape(b, s, attn_w), v.reshape(b, s, attn_w),
                              cache_k, n_heads=n_heads, pages_per_step=pg_moba)
    sn, conv_p, ssm_p, ps_b = _ssd_prompt(xbc.reshape(b, s, conv_ch), dtr.reshape(b, s, LANES),
                                          z.reshape(b, s, ssm_w), *ssd_par, row(ssm_out_g), expand, cache_k,
                                          page0=ps_ssd0, pages_per_step=pg_ssd, **ssd_kw)
    h_p = _outproj(attn.reshape(b * s, attn_w), sn.reshape(b * s, ssm_w), xp2, row(attn_out_g), w_out_b,
                   tm=_row_tile(b * s, 512))
    y_p = _ffn(h_p, row(norm_ffn_g), w_gate_b, w_up_b, w_down_b,
               tm=_row_tile(b * s, 1024)).reshape(b, s, d)
    page_shape = (b, s // page, page, n_heads, dh)
    k_p, v_p = k.reshape(page_shape), v.reshape(page_shape)

    ms = db * t_len
    xs2 = x_sample.reshape(ms, d)
    qs, k_s, v_s, zs, xbcs, dtrs = _inproj(xs2, *in_args, tm=_row_tile(ms, 1024), **in_kw)
    qs3, ks3, vs3 = (a.reshape(db, t_len, attn_w) for a in (qs, k_s, v_s))
    sel = _sample_select(page_table, ps_a, ps_b, qs3, n_heads=n_heads, page=page, split=ps_ssd0)
    idx = sel[:, :, :t_len, :MOBA_TOP_K].reshape(db, n_heads * t_len * MOBA_TOP_K)
    ok = sel[:, :, :t_len, MOBA_TOP_K:2 * MOBA_TOP_K].reshape(db, n_heads * t_len * MOBA_TOP_K)
    attn_s = _sample_attend(page_table, idx, ok, qs3, ks3, vs3, cache_k, cache_v, n_heads=n_heads)
    sns, conv_s, ssm_s = _ssd_sample(xbcs.reshape(db, t_len, conv_ch), state_conv,
                                     dtrs.reshape(db, t_len, LANES), zs.reshape(db, t_len, ssm_w), state_ssm,
                                     *ssd_par, row(ssm_out_g), expand, **ssd_kw)
    tms = _row_tile(ms, 512)
    h_s = _outproj(attn_s.reshape(ms, attn_w), sns.reshape(ms, ssm_w), xs2, row(attn_out_g), w_out_b, tm=tms)
    y_s = _ffn(h_s, row(norm_ffn_g), w_gate_b, w_up_b, w_down_b, tm=tms).reshape(db, t_len, d)
    tok_shape = (db, t_len, n_heads, dh)
    return (y_p, y_s, k_p, v_p, conv_p, ssm_p, k_s.reshape(tok_shape), v_s.reshape(tok_shape),
            conv_s, ssm_s)
```

```python
import functools

import jax
import jax.numpy as jnp
from jax import lax
from jax.experimental import pallas as pl
from jax.experimental.pallas import tpu as pltpu

F32 = jnp.float32
BF16 = jnp.bfloat16
EPS = 1e-6
NEG = -1e30

MOBA_BLOCK = 256
MOBA_TOP_K = 3
SSD_CHUNK = 128
SSM_HEAD_DIM = 64
LANES = 128
SUBLANES = 8
VMEM_LIMIT = 56 * 1024 * 1024
BIG_VMEM_LIMIT = 60 * 1024 * 1024

_NT = (((1,), (1,)), ((), ()))
_TN = (((0,), (0,)), ((), ()))


def _params(sem, vmem=VMEM_LIMIT):
    return pltpu.CompilerParams(dimension_semantics=sem, vmem_limit_bytes=vmem)


def _rms(x, g):
    return x * lax.rsqrt(jnp.mean(x * x, axis=-1, keepdims=True) + EPS) * g


def _silu(x):
    return x * (1.0 / (1.0 + jnp.exp(-x)))


def _softplus(x):
    return jnp.maximum(x, 0.0) + jnp.log1p(jnp.exp(-jnp.abs(x)))


def _split3(x):
    hi = x.astype(BF16)
    rest = x - hi.astype(F32)
    mid = rest.astype(BF16)
    return hi, mid, (rest - mid.astype(F32)).astype(BF16)


def _alibi_slope(h):
    bits = (jnp.full((1, 1), 126, jnp.int32) - h) << 23
    return lax.bitcast_convert_type(bits, F32)


def _inproj_kernel(*refs, tn, seg, n_side):
    x_ref, g_ref, w_ref, wdt_ref, qg_ref, kg_ref = refs[:6]
    q_ref, k_ref, v_ref, z_ref, xbc_ref, dt_ref = refs[6 + n_side:12 + n_side]
    xn_ref = refs[-1]
    for src, dst in zip(refs[6:6 + n_side], refs[12 + n_side:12 + 2 * n_side]):
        dst[...] = src[...].astype(BF16)
    j = pl.program_id(1)

    @pl.when(j == 0)
    def _():
        xn_ref[...] = _rms(x_ref[...], g_ref[...]).astype(BF16)
        dt_ref[...] = jnp.dot(xn_ref[...], wdt_ref[...], preferred_element_type=F32)

    acc = jnp.dot(xn_ref[...], w_ref[...].astype(BF16), preferred_element_type=F32)

    def headnorm(out_ref, g):
        for hh in range(tn // LANES):
            sl = slice(hh * LANES, (hh + 1) * LANES)
            out_ref[:, sl] = _rms(acc[:, sl], g)

    @pl.when(j < seg[0])
    def _():
        headnorm(q_ref, qg_ref[...])

    @pl.when((j >= seg[0]) & (j < seg[1]))
    def _():
        headnorm(k_ref, kg_ref[...])

    @pl.when((j >= seg[1]) & (j < seg[2]))
    def _():
        v_ref[...] = acc

    @pl.when((j >= seg[2]) & (j < seg[3]))
    def _():
        z_ref[...] = acc

    @pl.when(j >= seg[3])
    def _():
        xbc_ref[...] = acc


def _grid_tiling(shape, ni, nj):
    for (gr, gc), imap in (((ni, nj), lambda i, j: (i, j)), ((nj, ni), lambda i, j: (j, i))):
        if shape[0] % gr == 0 and shape[1] % gc == 0:
            br, bc = shape[0] // gr, shape[1] // gc
            if br % (2 * SUBLANES) == 0 and bc % LANES == 0:
                return pl.BlockSpec((br, bc), imap)
    return None


def _inproj(x2, norm_g, w_all, w_dt, q_g, k_g, side=(), *, attn_w, ssm_w, conv_ch, tm, tn=512):
    m, d = x2.shape
    n_main = 3 * attn_w + ssm_w + conv_ch
    assert n_main % tn == 0
    seg = (attn_w // tn, 2 * attn_w // tn, 3 * attn_w // tn, (3 * attn_w + ssm_w) // tn)
    nj = n_main // tn
    side_specs = [_grid_tiling(w.shape, m // tm, nj) for w in side]
    assert all(sp is not None for sp in side_specs)

    def col(lo, hi):
        return lambda i, j: (i, jnp.clip(j - lo, 0, hi - lo - 1))

    out_shape = (
        jax.ShapeDtypeStruct((m, attn_w), F32), jax.ShapeDtypeStruct((m, attn_w), F32),
        jax.ShapeDtypeStruct((m, attn_w), F32), jax.ShapeDtypeStruct((m, ssm_w), F32),
        jax.ShapeDtypeStruct((m, conv_ch), F32), jax.ShapeDtypeStruct((m, LANES), F32),
        *(jax.ShapeDtypeStruct(w.shape, BF16) for w in side))
    return pl.pallas_call(
        functools.partial(_inproj_kernel, tn=tn, seg=seg, n_side=len(side)),
        grid=(m // tm, nj),
        in_specs=[
            pl.BlockSpec((tm, d), lambda i, j: (i, 0)),
            pl.BlockSpec((1, d), lambda i, j: (0, 0)),
            pl.BlockSpec((d, tn), lambda i, j: (0, j)),
            pl.BlockSpec((d, LANES), lambda i, j: (0, 0)),
            pl.BlockSpec((1, LANES), lambda i, j: (0, 0)),
            pl.BlockSpec((1, LANES), lambda i, j: (0, 0)),
            *side_specs,
        ],
        out_specs=(
            pl.BlockSpec((tm, tn), col(0, seg[0])),
            pl.BlockSpec((tm, tn), col(seg[0], seg[1])),
            pl.BlockSpec((tm, tn), col(seg[1], seg[2])),
            pl.BlockSpec((tm, tn), col(seg[2], seg[3])),
            pl.BlockSpec((tm, tn), col(seg[3], nj)),
            pl.BlockSpec((tm, LANES), lambda i, j: (i, 0)),
            *side_specs,
        ),
        out_shape=out_shape,
        scratch_shapes=[pltpu.VMEM((tm, d), BF16)],
        compiler_params=_params(("parallel", "arbitrary")),
        name="inproj",
    )(x2, norm_g, w_all, w_dt, q_g, k_g, *side)


def _block_select(gate_t, n_past):
    row = lax.broadcasted_iota(jnp.int32, gate_t.shape, 0)
    valid = row < n_past
    if n_past <= MOBA_TOP_K:
        return valid & (gate_t > 0.5 * NEG)
    rank = jnp.zeros(gate_t.shape, jnp.int32)
    for j in range(n_past):
        other = gate_t[j:j + 1, :]
        rank = rank + ((other > gate_t) | ((other == gate_t) & (row > j))).astype(jnp.int32)
    return valid & (rank < MOBA_TOP_K) & (gate_t > 0.5 * NEG)


def _page_chunk_copy(ck_hbm, buf, sem, first_page, n_pages):
    return pltpu.make_async_copy(ck_hbm.at[pl.ds(first_page, n_pages)], buf, sem)


def _page_key_sums(pages, lanes_of_adds=4):
    n, page, heads, dh = pages.shape
    part = pages.reshape(n * lanes_of_adds, page // lanes_of_adds, heads, dh).sum(axis=1)
    return part.reshape(n, lanes_of_adds, heads, dh).sum(axis=1)


def _stream_page_sums(ck_hbm, ps_ref, pbuf, psem, step, n_steps, page0):
    n_pg = pbuf.shape[1]
    slot = step % 2

    def page_copy(st, sl):
        return _page_chunk_copy(ck_hbm, pbuf.at[sl], psem.at[sl], page0 + st * n_pg, n_pg)

    @pl.when(step == 0)
    def _():
        page_copy(0, 0).start()

    page_copy(step, slot).wait()

    @pl.when(step + 1 < n_steps)
    def _():
        page_copy(step + 1, 1 - slot).start()

    ps_ref[...] = _page_key_sums(pbuf[slot])


def _causal_chunks(total, nb):
    weights = [i + 2 for i in range(nb - 1)] + [1]
    cum, acc = [0], 0
    for w in weights:
        acc += w
        cum.append((total * acc + sum(weights) // 2) // sum(weights))
    return [hi - lo for lo, hi in zip(cum[:-1], cum[1:])], cum[:-1]


def _moba_prompt_kernel(q_ref, k_ref, v_ref, ck_hbm, o_ref, ps_ref, pbuf, psem, *, nb, scale, pages):
    blk = MOBA_BLOCK
    h = pl.program_id(1)
    step = pl.program_id(0) * pl.num_programs(1) + h
    n_steps = pl.num_programs(0) * pl.num_programs(1)
    sizes, offs = _causal_chunks(pages, nb)
    chunks = [i for i in range(nb) if sizes[i] > 0]
    n_ch = len(chunks)
    n_slots = pbuf.shape[0]
    ahead = n_slots - 1
    assert n_ch >= ahead

    def page_copy(c):
        st, cc = divmod(c, n_ch)
        i = chunks[cc]
        slot = (step * n_ch + c) % n_slots
        return _page_chunk_copy(ck_hbm, pbuf.at[slot, pl.ds(0, sizes[i])], psem.at[slot],
                                (step + st) * pages + offs[i], sizes[i])

    @pl.when(step == 0)
    def _():
        for c in range(ahead):
            page_copy(c).start()

    slope = _alibi_slope(h)
    k = k_ref[...]
    kb = k.astype(BF16)
    vb = v_ref[...].astype(BF16)
    nb_pad = -(-nb // SUBLANES) * SUBLANES
    means = [jnp.mean(k[j * blk:(j + 1) * blk], axis=0, keepdims=True) for j in range(nb)]
    means = jnp.concatenate(means + [jnp.zeros((nb_pad - nb, LANES), F32)] * (nb_pad > nb), axis=0)
    for i in range(nb):
        qi = q_ref[i * blk:(i + 1) * blk, :]
        nk = (i + 1) * blk
        s = lax.dot_general(qi.astype(BF16), kb[:nk], _NT, preferred_element_type=F32)
        col = lax.broadcasted_iota(jnp.int32, (blk, nk), 1)
        logits = s * scale + slope * col.astype(F32)
        row = lax.broadcasted_iota(jnp.int32, (blk, blk), 0)
        colb = lax.broadcasted_iota(jnp.int32, (blk, blk), 1)
        pieces = []
        if i > 0:
            gate_t = lax.dot_general(means, qi, _NT, precision=lax.Precision.HIGHEST,
                                     preferred_element_type=F32)
            sel_t = _block_select(gate_t, i).astype(F32)
            sel_t = jnp.concatenate([sel_t, jnp.zeros((LANES - nb_pad, blk), F32)], axis=0)
            sel = jnp.concatenate([sel_t[:, r * LANES:(r + 1) * LANES].T
                                   for r in range(blk // LANES)], axis=0)
            pieces = [jnp.broadcast_to(sel[:, j:j + 1], (blk, blk)) for j in range(i)]
        pieces.append((colb <= row).astype(F32))
        allowed = jnp.concatenate(pieces, axis=1) if len(pieces) > 1 else pieces[0]
        logits = jnp.where(allowed > 0.5, logits, NEG)
        m = jnp.max(logits, axis=-1, keepdims=True)
        p = jnp.exp(logits - m)
        denom = jnp.sum(p, axis=-1, keepdims=True)
        out = jnp.dot(p.astype(BF16), vb[:nk], preferred_element_type=F32)
        o_ref[i * blk:(i + 1) * blk, :] = out / denom
        if i in chunks:
            c = chunks.index(i)
            page_copy(c).wait()
            if c + ahead < n_ch:
                page_copy(c + ahead).start()
            else:
                @pl.when(step + 1 < n_steps)
                def _():
                    page_copy(c + ahead).start()
            slot = (step * n_ch + c) % n_slots
            ps_ref[offs[i]:offs[i] + sizes[i]] = _page_key_sums(pbuf[slot, 0:sizes[i]])


def _moba_prompt(q, k, v, cache_k, *, n_heads, pages_per_step):
    b, s, w = q.shape
    dh = w // n_heads
    nb = s // MOBA_BLOCK
    npc = max(_causal_chunks(pages_per_step, nb)[0])
    page, ch, cdh = cache_k.shape[1:]
    spec = pl.BlockSpec((None, s, dh), lambda bi, hi: (bi, 0, hi))
    return pl.pallas_call(
        functools.partial(_moba_prompt_kernel, nb=nb, scale=dh ** -0.5, pages=pages_per_step),
        grid=(b, n_heads),
        in_specs=[spec, spec, spec, pl.BlockSpec(memory_space=pl.ANY)],
        out_specs=(spec, pl.BlockSpec((pages_per_step, ch, cdh), lambda bi, hi: (bi * n_heads + hi, 0, 0))),
        out_shape=(jax.ShapeDtypeStruct((b, s, w), F32),
                   jax.ShapeDtypeStruct((b * n_heads * pages_per_step, ch, cdh), F32)),
        scratch_shapes=[pltpu.VMEM((3, npc, page, ch, cdh), F32), pltpu.SemaphoreType.DMA((3,))],
        compiler_params=_params(("arbitrary", "arbitrary")),
        name="moba_prompt",
    )(q, k, v, cache_k)


def _gated_norm(y, z, g):
    return _rms(y * _silu(z), g).astype(BF16)


def _ssd_prompt_kernel(xbc_ref, dtr_ref, z_ref, cw_ref, cb_ref, dtb_ref, alog_ref, dskip_ref, sg_ref,
                       expand_ref, ck_hbm, s_ref, convp_ref, h_ref, ps_ref, xp_ref, pbuf, psem,
                       *, cw, ssm_w, groups, n_state, page0):
    c = pl.program_id(1)
    nc = pl.num_programs(1)
    _stream_page_sums(ck_hbm, ps_ref, pbuf, psem, pl.program_id(0) * nc + c, pl.num_programs(0) * nc,
                      page0)
    lc = SSD_CHUNK
    p_dim = SSM_HEAD_DIM
    heads = ssm_w // p_dim
    hpg = heads // groups
    gw = hpg * p_dim
    tail = cw - 1

    @pl.when(c == 0)
    def _():
        xp_ref[0:SUBLANES, :] = jnp.zeros((SUBLANES, xp_ref.shape[1]), F32)
        h_ref[...] = jnp.zeros(h_ref.shape, F32)

    xc = xbc_ref[...]
    xp_ref[SUBLANES:SUBLANES + lc, :] = xc
    conv = cb_ref[...]
    for i in range(cw):
        conv = conv + xp_ref[SUBLANES - tail + i:SUBLANES - tail + i + lc, :] * cw_ref[i:i + 1, :]
    act = _silu(conv)
    xp_ref[SUBLANES - tail:SUBLANES, :] = xc[lc - tail:lc, :]

    @pl.when(c == nc - 1)
    def _():
        convp_ref[...] = xc[lc - tail:lc, :]

    dt = _softplus(dtr_ref[...] + dtb_ref[...])
    a = -jnp.exp(alog_ref[...])
    row = lax.broadcasted_iota(jnp.int32, (lc, lc), 0)
    colm = lax.broadcasted_iota(jnp.int32, (lc, lc), 1)
    causal = row >= colm
    tri = causal.astype(BF16)
    cum = sum(jnp.dot(tri, t, preferred_element_type=F32) for t in _split3(dt * a))
    cum_t = cum.T
    dt_t = dt.T
    cum_last = cum[lc - 1:lc, :]
    dec_end = jnp.exp(cum_last - cum) * dt
    ecum = jnp.exp(cum)
    expand = expand_ref[...].astype(BF16)
    ecum_x = sum(jnp.dot(t, expand, preferred_element_type=F32) for t in _split3(ecum))
    dec_end_x = sum(jnp.dot(t, expand, preferred_element_type=F32) for t in _split3(dec_end))

    xs = act[:, :ssm_w]
    xs_b = xs.astype(BF16)
    y = dskip_ref[...] * xs
    lane = lax.broadcasted_iota(jnp.int32, (lc, 2 * p_dim), 1)
    for g in range(groups):
        bg = act[:, ssm_w + g * n_state:ssm_w + (g + 1) * n_state].astype(BF16)
        cg = act[:, ssm_w + (groups + g) * n_state:ssm_w + (groups + g + 1) * n_state].astype(BF16)
        cb = lax.dot_general(cg, bg, _NT, preferred_element_type=F32)
        hg = h_ref[g * hpg:(g + 1) * hpg].reshape(gw, n_state)
        y_off = lax.dot_general(cg, hg.astype(BF16), _NT, preferred_element_type=F32)
        y_off = y_off * ecum_x[:, g * gw:(g + 1) * gw]
        y_diag = []
        for pr in range(hpg // 2):
            ws = []
            for e in (g * hpg + 2 * pr, g * hpg + 2 * pr + 1):
                diff = cum[:, e:e + 1] - cum_t[e:e + 1, :]
                decay = jnp.where(causal, jnp.exp(diff), 0.0)
                ws.append((cb * decay * dt_t[e:e + 1, :]).astype(BF16))
            lo = (g * hpg + 2 * pr) * p_dim
            xpair = xs_b[:, lo:lo + 2 * p_dim]
            zero = jnp.zeros_like(xpair)
            rhs = jnp.concatenate([jnp.where(lane < p_dim, xpair, zero),
                                   jnp.where(lane >= p_dim, xpair, zero)], axis=0)
            y_diag.append(jnp.dot(jnp.concatenate(ws, axis=1), rhs, preferred_element_type=F32))
        y_g = jnp.concatenate(y_diag, axis=1) + y_off
        xw = (xs[:, g * gw:(g + 1) * gw] * dec_end_x[:, g * gw:(g + 1) * gw]).astype(BF16)
        st = lax.dot_general(xw, bg, _TN, preferred_element_type=F32)
        for ee in range(hpg):
            e = g * hpg + ee
            cd = jnp.exp(cum_t[e:e + 1, lc - 1:lc])
            h_ref[e] = h_ref[e] * cd + st[ee * p_dim:(ee + 1) * p_dim, :]
        gl = slice(g * gw, (g + 1) * gw)
        s_ref[:, gl] = _gated_norm(y[:, gl] + y_g, z_ref[:, gl], sg_ref[:, gl])


def _ssd_prompt(xbc, dtr, z, conv_w, conv_b, dtb, alog, dskip_x, ssm_g, expand, cache_k,
                *, ssm_w, groups, n_state, page0, pages_per_step):
    b, s, cc = xbc.shape
    cw = conv_w.shape[0]
    heads = ssm_w // SSM_HEAD_DIM
    lc = SSD_CHUNK
    nc = s // lc
    page, ch, cdh = cache_k.shape[1:]
    const = lambda shape: pl.BlockSpec(shape, lambda bi, ci: (0,) * len(shape))
    return pl.pallas_call(
        functools.partial(_ssd_prompt_kernel, cw=cw, ssm_w=ssm_w, groups=groups, n_state=n_state,
                          page0=page0),
        grid=(b, nc),
        in_specs=[
            pl.BlockSpec((None, lc, cc), lambda bi, ci: (bi, ci, 0)),
            pl.BlockSpec((None, lc, LANES), lambda bi, ci: (bi, ci, 0)),
            pl.BlockSpec((None, lc, ssm_w), lambda bi, ci: (bi, ci, 0)),
            const((cw, cc)), const((1, cc)), const((1, LANES)), const((1, LANES)),
            const((1, ssm_w)), const((1, ssm_w)), const((LANES, ssm_w)),
            pl.BlockSpec(memory_space=pl.ANY),
        ],
        out_specs=(
            pl.BlockSpec((None, lc, ssm_w), lambda bi, ci: (bi, ci, 0)),
            pl.BlockSpec((None, cw - 1, cc), lambda bi, ci: (bi, 0, 0)),
            pl.BlockSpec((None, heads, SSM_HEAD_DIM, n_state), lambda bi, ci: (bi, 0, 0, 0)),
            pl.BlockSpec((pages_per_step, ch, cdh), lambda bi, ci: (bi * nc + ci, 0, 0)),
        ),
        out_shape=(
            jax.ShapeDtypeStruct((b, s, ssm_w), BF16),
            jax.ShapeDtypeStruct((b, cw - 1, cc), F32),
            jax.ShapeDtypeStruct((b, heads, SSM_HEAD_DIM, n_state), F32),
            jax.ShapeDtypeStruct((b * nc * pages_per_step, ch, cdh), F32),
        ),
        scratch_shapes=[pltpu.VMEM((SUBLANES + lc, cc), F32),
                        pltpu.VMEM((2, pages_per_step, page, ch, cdh), F32),
                        pltpu.SemaphoreType.DMA((2,))],
        compiler_params=_params(("arbitrary", "arbitrary")),
        name="ssd_prompt",
    )(xbc, dtr, z, conv_w, conv_b, dtb, alog, dskip_x, ssm_g, expand, cache_k)


def _outproj_kernel(attn_ref, s_ref, x_ref, ag_ref, w_ref, o_ref, *, rc):
    for r in range(o_ref.shape[0] // rc):
        rows = slice(r * rc, (r + 1) * rc)
        a = _rms(attn_ref[rows, :], ag_ref[...]).astype(BF16)
        mix = jnp.concatenate([a, s_ref[rows, :]], axis=1)
        o_ref[rows, :] = x_ref[rows, :] + jnp.dot(mix, w_ref[...], preferred_element_type=F32)


def _outproj(attn, s_norm, x2, attn_g, w_out, *, tm, rc=256):
    m, attn_w = attn.shape
    ssm_w = s_norm.shape[1]
    d = x2.shape[1]
    return pl.pallas_call(
        functools.partial(_outproj_kernel, rc=min(rc, tm)),
        grid=(m // tm,),
        in_specs=[
            pl.BlockSpec((tm, attn_w), lambda i: (i, 0)),
            pl.BlockSpec((tm, ssm_w), lambda i: (i, 0)),
            pl.BlockSpec((tm, d), lambda i: (i, 0)),
            pl.BlockSpec((1, attn_w), lambda i: (0, 0)),
            pl.BlockSpec((attn_w + ssm_w, d), lambda i: (0, 0)),
        ],
        out_specs=pl.BlockSpec((tm, d), lambda i: (i, 0)),
        out_shape=jax.ShapeDtypeStruct((m, d), F32),
        compiler_params=_params(("parallel",)),
        name="outproj",
    )(attn, s_norm, x2, attn_g, w_out)


def _ffn_kernel(h_ref, g_ref, wg_ref, wu_ref, wd_ref, o_ref, hn_ref, *, rc):
    f = pl.program_id(1)

    @pl.when(f == 0)
    def _():
        hn_ref[...] = _rms(h_ref[...], g_ref[...]).astype(BF16)
        o_ref[...] = h_ref[...]

    for r in range(o_ref.shape[0] // rc):
        rows = slice(r * rc, (r + 1) * rc)
        hn = hn_ref[rows, :]
        gate = jnp.dot(hn, wg_ref[...], preferred_element_type=F32)
        up = jnp.dot(hn, wu_ref[...], preferred_element_type=F32)
        o_ref[rows, :] += jnp.dot((_silu(gate) * up).astype(BF16), wd_ref[...],
                                  preferred_element_type=F32)


def _ffn(h, g, w_gate, w_up, w_down, *, tm, tf=512, rc=512):
    m, d = h.shape
    dff = w_gate.shape[1]
    return pl.pallas_call(
        functools.partial(_ffn_kernel, rc=min(rc, tm)),
        grid=(m // tm, dff // tf),
        in_specs=[
            pl.BlockSpec((tm, d), lambda i, f: (i, 0)),
            pl.BlockSpec((1, d), lambda i, f: (0, 0)),
            pl.BlockSpec((d, tf), lambda i, f: (0, f)),
            pl.BlockSpec((d, tf), lambda i, f: (0, f)),
            pl.BlockSpec((tf, d), lambda i, f: (f, 0)),
        ],
        out_specs=pl.BlockSpec((tm, d), lambda i, f: (i, 0)),
        out_shape=jax.ShapeDtypeStruct((m, d), F32),
        scratch_shapes=[pltpu.VMEM((tm, d), BF16)],
        compiler_params=_params(("parallel", "arbitrary"), vmem=BIG_VMEM_LIMIT),
        name="ffn",
    )(h, g, w_gate, w_up, w_down)


def _sample_select_kernel(pt_ref, psa_ref, psb_ref, q_ref, sel_ref, blk_ref,
                          *, nb, ppb, n_heads, t_len, split):
    b = pl.program_id(0)

    def page_sum(pg):
        lo = psa_ref[jnp.minimum(pg, split - 1)]
        hi = psb_ref[jnp.maximum(pg - split, 0)]
        return jnp.where(pg < split, lo, hi)

    for n in range(nb):
        acc = page_sum(pt_ref[b, n * ppb])
        for pp in range(1, ppb):
            acc = acc + page_sum(pt_ref[b, n * ppb + pp])
        blk_ref[n] = acc / MOBA_BLOCK
    lane = lax.broadcasted_iota(jnp.int32, (SUBLANES, LANES), 1)
    for h in range(n_heads):
        qh = q_ref[:, h * LANES:(h + 1) * LANES]
        q8 = jnp.concatenate([qh, jnp.zeros((SUBLANES - t_len, LANES), F32)], axis=0)
        bm = blk_ref[:, h, :]
        gate = lax.dot_general(q8, bm, _NT, precision=lax.Precision.HIGHEST,
                               preferred_element_type=F32)
        lane_n = lax.broadcasted_iota(jnp.int32, gate.shape, 1).astype(F32)
        out = jnp.zeros((SUBLANES, LANES), jnp.int32)
        for kk in range(MOBA_TOP_K):
            mx = jnp.max(gate, axis=-1, keepdims=True)
            idx_f = jnp.min(jnp.where(gate == mx, lane_n, float(nb)), axis=-1, keepdims=True)
            idx = idx_f.astype(jnp.int32)
            ok = (mx > 0.5 * NEG).astype(jnp.int32)
            out = jnp.where(lane == kk, idx, out)
            out = jnp.where(lane == MOBA_TOP_K + kk, ok, out)
            gate = jnp.where(lane_n == idx_f, -jnp.inf, gate)
        sel_ref[h] = out


def _sample_select(page_table, ps_lo, ps_hi, q_s, *, n_heads, page, split):
    db, t_len, w = q_s.shape
    n_pages = page_table.shape[1]
    _, h, dh = ps_lo.shape
    ppb = MOBA_BLOCK // page
    nb = n_pages // ppb
    whole = lambda a: pl.BlockSpec(a.shape, lambda bi, pt: (0, 0, 0), pipeline_mode=pl.Buffered(1))
    grid_spec = pltpu.PrefetchScalarGridSpec(
        num_scalar_prefetch=1,
        grid=(db,),
        in_specs=[whole(ps_lo), whole(ps_hi),
                  pl.BlockSpec((None, t_len, w), lambda bi, pt: (bi, 0, 0))],
        out_specs=pl.BlockSpec((None, n_heads, SUBLANES, LANES), lambda bi, pt: (bi, 0, 0, 0)),
        scratch_shapes=[pltpu.VMEM((nb, h, dh), F32)],
    )
    return pl.pallas_call(
        functools.partial(_sample_select_kernel, nb=nb, ppb=ppb, n_heads=n_heads, t_len=t_len,
                          split=split),
        grid_spec=grid_spec,
        out_shape=jax.ShapeDtypeStruct((db, n_heads, SUBLANES, LANES), jnp.int32),
        compiler_params=_params(("arbitrary",)),
        name="sample_select",
    )(page_table, ps_lo, ps_hi, q_s)


def _sample_attend_kernel(pt_ref, idx_ref, ok_ref, q_ref, kn_ref, vn_ref, ck_hbm, cv_hbm, o_ref,
                          kbuf, vbuf, sem, *, t_len, page, ppb, past_len, scale, hps):
    b = pl.program_id(0)
    hg = pl.program_id(1)
    n_hg = pl.num_programs(1)
    step = b * n_hg + hg
    cur = step % 2
    n_sel = t_len * MOBA_TOP_K * ppb
    n_pages = past_len // page

    def copies(st, hl, hh, slot, phys):
        return (pltpu.make_async_copy(ck_hbm.at[phys, :, hh, :], kbuf.at[st, hl, slot], sem.at[st, 0]),
                pltpu.make_async_copy(cv_hbm.at[phys, :, hh, :], vbuf.at[st, hl, slot], sem.at[st, 1]))

    def slot_pages(bb, hh):
        out = []
        for t in range(t_len):
            for kk in range(MOBA_TOP_K):
                n = idx_ref[bb, (hh * t_len + t) * MOBA_TOP_K + kk]
                for pp in range(ppb):
                    lp = jnp.clip(n * ppb + pp, 0, n_pages - 1)
                    out.append(((t * MOBA_TOP_K + kk) * ppb + pp, pt_ref[bb, lp], n * ppb + pp))
        out.append((n_sel, pt_ref[bb, n_pages - 1], None))
        return out

    def start_all(st, bb, hgg):
        for hl in range(hps):
            hh = hgg * hps + hl
            for slot, phys, _ in slot_pages(bb, hh):
                for cp in copies(st, hl, hh, slot, phys):
                    cp.start()

    @pl.when(step == 0)
    def _():
        start_all(0, b, hg)

    wrap = hg + 1 == n_hg
    b_next = b + wrap.astype(jnp.int32)
    hg_next = jnp.where(wrap, 0, hg + 1)

    @pl.when(step + 1 < pl.num_programs(0) * n_hg)
    def _():
        start_all(1 - cur, b_next, hg_next)

    n_keys = (n_sel + 2) * page
    lane = lax.broadcasted_iota(jnp.int32, (1, page), 1)
    trow = lax.broadcasted_iota(jnp.int32, (SUBLANES, n_keys), 0)
    kcol = lax.broadcasted_iota(jnp.int32, (SUBLANES, n_keys), 1)
    q_pos = past_len + trow
    blk_start = (q_pos // MOBA_BLOCK) * MOBA_BLOCK
    per_t = MOBA_TOP_K * ppb * page
    sel_cols = n_sel * page
    pad = jnp.zeros((page - t_len, LANES), F32)

    head_pages = [slot_pages(b, hg * hps + hl) for hl in range(hps)]
    for hl in range(hps):
        hcols = slice(hl * LANES, (hl + 1) * LANES)
        kbuf[cur, hl, n_sel + 1] = jnp.concatenate([kn_ref[:, hcols], pad], axis=0)
        vbuf[cur, hl, n_sel + 1] = jnp.concatenate([vn_ref[:, hcols], pad], axis=0)

    for hl in range(hps):
        for slot, phys, _ in head_pages[hl]:
            for cp in copies(cur, hl, hg * hps + hl, slot, phys):
                cp.wait()

    for hl in range(hps):
        h = hg * hps + hl
        hcols = slice(hl * LANES, (hl + 1) * LANES)
        pos, okv = [], []
        for slot, _, lp in head_pages[hl][:-1]:
            n_ok = ok_ref[b, (h * t_len + slot // (MOBA_TOP_K * ppb)) * MOBA_TOP_K
                          + (slot // ppb) % MOBA_TOP_K]
            pos.append(lane + jnp.clip(lp, 0, n_pages - 1) * page)
            okv.append(jnp.broadcast_to(n_ok, (1, page)))
        pos.append(lane + (past_len - page))
        pos.append(lane + past_len)
        pos = jnp.concatenate(pos, axis=1)
        okv = jnp.concatenate(okv + [jnp.zeros((1, n_keys - sel_cols), jnp.int32)], axis=1)

        kall = kbuf[cur, hl].reshape(n_keys, LANES).astype(BF16)
        vall = vbuf[cur, hl].reshape(n_keys, LANES).astype(BF16)
        q8 = jnp.concatenate([q_ref[:, hcols], jnp.zeros((SUBLANES - t_len, LANES), F32)], axis=0)
        s = lax.dot_general(q8.astype(BF16), kall, _NT, preferred_element_type=F32)

        dist = (q_pos - pos).astype(F32)
        logits = s * scale - _alibi_slope(h) * dist
        sel_ok = (kcol // per_t == trow) & (okv > 0)
        own_ok = ((kcol >= sel_cols) & (pos <= q_pos) & (pos >= blk_start)
                  & (kcol < sel_cols + page + t_len))
        logits = jnp.where(sel_ok | own_ok, logits, NEG)
        m = jnp.max(logits, axis=-1, keepdims=True)
        p = jnp.exp(logits - m)
        denom = jnp.sum(p, axis=-1, keepdims=True)
        out = jnp.dot(p.astype(BF16), vall, preferred_element_type=F32) / denom
        o_ref[:, hcols] = out[:t_len, :]


def _sample_attend(page_table, idx, ok, q_s, k_s, v_s, cache_k, cache_v, *, n_heads, hps=4):
    db, t_len, w = q_s.shape
    page = cache_k.shape[1]
    n_pages = page_table.shape[1]
    ppb = MOBA_BLOCK // page
    n_slots = t_len * MOBA_TOP_K * ppb + 2
    assert n_heads % hps == 0
    tok = pl.BlockSpec((None, t_len, hps * LANES), lambda bi, hi, *_: (bi, 0, hi))
    grid_spec = pltpu.PrefetchScalarGridSpec(
        num_scalar_prefetch=3,
        grid=(db, n_heads // hps),
        in_specs=[tok, tok, tok,
                  pl.BlockSpec(memory_space=pl.ANY), pl.BlockSpec(memory_space=pl.ANY)],
        out_specs=tok,
        scratch_shapes=[pltpu.VMEM((2, hps, n_slots, page, LANES), F32),
                        pltpu.VMEM((2, hps, n_slots, page, LANES), F32),
                        pltpu.SemaphoreType.DMA((2, 2))],
    )
    return pl.pallas_call(
        functools.partial(_sample_attend_kernel, t_len=t_len, page=page, ppb=ppb,
                          past_len=n_pages * page, scale=LANES ** -0.5, hps=hps),
        grid_spec=grid_spec,
        out_shape=jax.ShapeDtypeStruct((db, t_len, w), F32),
        compiler_params=_params(("arbitrary", "arbitrary")),
        name="sample_attend",
    )(page_table, idx, ok, q_s, k_s, v_s, cache_k, cache_v)


def _ssd_sample_kernel(xbc_ref, sc_ref, dtr_ref, z_ref, h0_ref, cw_ref, cb_ref, dtb_ref, alog_ref,
                       dskip_ref, sg_ref, expand_ref, s_ref, convs_ref, h_ref, xp_ref,
                       *, cw, t_len, ssm_w, groups, n_state):
    p_dim = SSM_HEAD_DIM
    heads = ssm_w // p_dim
    hpg = heads // groups
    gw = hpg * p_dim
    tail = cw - 1
    xp_ref[0:tail, :] = sc_ref[...]
    xp_ref[tail:tail + t_len, :] = xbc_ref[...]
    conv = cb_ref[...]
    for i in range(cw):
        conv = conv + xp_ref[i:i + t_len, :] * cw_ref[i:i + 1, :]
    act = _silu(conv)
    convs_ref[...] = xp_ref[t_len:t_len + tail, :]

    dt = _softplus(dtr_ref[...] + dtb_ref[...])
    da = dt * (-jnp.exp(alog_ref[...]))
    cum = [da[0:1]]
    for t in range(1, t_len):
        cum.append(cum[-1] + da[t:t + 1])
    lane = lax.broadcasted_iota(jnp.int32, (1, LANES), 1)
    xs = act[:, :ssm_w]
    zpad = jnp.zeros((SUBLANES - t_len, n_state), F32)

    b8, c8, cbs = [], [], []
    for g in range(groups):
        bg = act[:, ssm_w + g * n_state:ssm_w + (g + 1) * n_state]
        cg = act[:, ssm_w + (groups + g) * n_state:ssm_w + (groups + g + 1) * n_state]
        b8.append(jnp.concatenate([bg, zpad], axis=0).astype(BF16))
        c8.append(jnp.concatenate([cg, zpad], axis=0).astype(BF16))
        cbs.append(lax.dot_general(c8[g], b8[g], _NT, preferred_element_type=F32))

    pairs = [(l, s) for l in range(t_len) for s in range(l + 1)]
    rows = []
    for l, s in pairs:
        cb_ls = jnp.zeros((1, LANES), F32)
        for g in range(groups):
            in_g = (lane >= g * hpg) & (lane < (g + 1) * hpg)
            cb_ls = jnp.where(in_g, cbs[g][l:l + 1, s:s + 1], cb_ls)
        rows.append(cb_ls * jnp.exp(cum[l] - cum[s]) * dt[s:s + 1])
    rows += [jnp.exp(cum[l]) for l in range(t_len)]
    rows += [jnp.exp(cum[-1] - cum[s]) * dt[s:s + 1] for s in range(t_len)]
    n_rows = -(-len(rows) // SUBLANES) * SUBLANES
    rows.append(jnp.zeros((n_rows - len(rows), LANES), F32))
    rx = jnp.dot(jnp.concatenate(rows, axis=0), expand_ref[...], precision=lax.Precision.HIGHEST,
                 preferred_element_type=F32)
    r_ecum = len(pairs)
    r_coef = r_ecum + t_len

    y_diag = []
    for l in range(t_len):
        acc = jnp.zeros((1, ssm_w), F32)
        for s in range(l + 1):
            i = pairs.index((l, s))
            acc = acc + rx[i:i + 1, :] * xs[s:s + 1, :]
        y_diag.append(acc)
    y = dskip_ref[...] * xs + jnp.concatenate(y_diag, axis=0)

    y_off = []
    for g in range(groups):
        hg = h0_ref[g * hpg:(g + 1) * hpg].reshape(gw, n_state)
        y_off.append(lax.dot_general(c8[g], hg.astype(BF16), _NT, preferred_element_type=F32)[:t_len])
    y = y + jnp.concatenate(y_off, axis=1) * rx[r_ecum:r_ecum + t_len, :]
    for g in range(groups):
        gl = slice(g * gw, (g + 1) * gw)
        s_ref[:, gl] = _gated_norm(y[:, gl], z_ref[:, gl], sg_ref[:, gl])

    e_last = jnp.exp(cum[-1])
    xw = xs * rx[r_coef:r_coef + t_len, :]
    for g in range(groups):
        xw8 = jnp.concatenate([xw[:, g * gw:(g + 1) * gw], jnp.zeros((SUBLANES - t_len, gw), F32)], axis=0)
        st = lax.dot_general(xw8.astype(BF16), b8[g], _TN, preferred_element_type=F32)
        for ee in range(hpg):
            e = g * hpg + ee
            h_ref[e] = h0_ref[e] * e_last[:, e:e + 1] + st[ee * p_dim:(ee + 1) * p_dim, :]


def _ssd_sample(xbc, state_conv, dtr, z, state_ssm, conv_w, conv_b, dtb, alog, dskip_x, ssm_g, expand,
                *, ssm_w, groups, n_state):
    db, t_len, cc = xbc.shape
    cw = conv_w.shape[0]
    heads = ssm_w // SSM_HEAD_DIM
    const = lambda shape: pl.BlockSpec(shape, lambda bi: (0,) * len(shape))
    return pl.pallas_call(
        functools.partial(_ssd_sample_kernel, cw=cw, t_len=t_len, ssm_w=ssm_w, groups=groups,
                          n_state=n_state),
        grid=(db,),
        in_specs=[
            pl.BlockSpec((None, t_len, cc), lambda bi: (bi, 0, 0)),
            pl.BlockSpec((None, cw - 1, cc), lambda bi: (bi, 0, 0)),
            pl.BlockSpec((None, t_len, LANES), lambda bi: (bi, 0, 0)),
            pl.BlockSpec((None, t_len, ssm_w), lambda bi: (bi, 0, 0)),
            pl.BlockSpec((None, heads, SSM_HEAD_DIM, n_state), lambda bi: (bi, 0, 0, 0)),
            const((cw, cc)), const((1, cc)), const((1, LANES)), const((1, LANES)), const((1, ssm_w)),
            const((1, ssm_w)), const((LANES, ssm_w)),
        ],
        out_specs=(
            pl.BlockSpec((None, t_len, ssm_w), lambda bi: (bi, 0, 0)),
            pl.BlockSpec((None, cw - 1, cc), lambda bi: (bi, 0, 0)),
            pl.BlockSpec((None, heads, SSM_HEAD_DIM, n_state), lambda bi: (bi, 0, 0, 0)),
        ),
        out_shape=(
            jax.ShapeDtypeStruct((db, t_len, ssm_w), BF16),
            jax.ShapeDtypeStruct((db, cw - 1, cc), F32),
            jax.ShapeDtypeStruct((db, heads, SSM_HEAD_DIM, n_state), F32),
        ),
        scratch_shapes=[pltpu.VMEM((2 * SUBLANES, cc), F32)],
        compiler_params=_params(("parallel",)),
        name="ssd_sample",
    )(xbc, state_conv, dtr, z, state_ssm, conv_w, conv_b, dtb, alog, dskip_x, ssm_g, expand)


def _row_tile(m, target):
    tm = min(m, target)
    assert m % tm == 0, (m, tm)
    return tm


def _page_split(n_phys, moba_steps, nb, ssd_steps):
    per_moba = max(1, n_phys * 6 // 10 // moba_steps)
    rest = max(n_phys - moba_steps * per_moba, 0)
    per_ssd = max(1, -(-rest // ssd_steps))
    return per_moba, per_ssd


def kernel(x_prompt, x_sample, cache_k, cache_v, page_table, state_conv, state_ssm, norm_mix_g, w_in, q_norm_g, k_norm_g, conv_w, conv_b, dt_bias, a_log, d_skip, attn_out_g, ssm_out_g, w_out, norm_ffn_g, w_gate, w_up, w_down):
    b, s, d = x_prompt.shape
    db, t_len, _ = x_sample.shape
    n_phys, page, n_heads, dh = cache_k.shape
    attn_w = n_heads * dh
    heads, p_dim, n_state = state_ssm.shape[1:]
    assert p_dim == SSM_HEAD_DIM and dh == LANES and heads <= LANES
    ssm_w = heads * p_dim
    conv_ch = state_conv.shape[2]
    groups = (conv_ch - ssm_w) // (2 * n_state)
    n_main = 3 * attn_w + ssm_w + conv_ch

    w_dt = jnp.pad(w_in[:, n_main:].astype(BF16), ((0, 0), (0, LANES - heads)))
    w_out_b = w_out.astype(BF16)
    row = lambda v: v.reshape(1, -1).astype(F32)
    padl = lambda v: jnp.pad(v.astype(F32), (0, LANES - v.shape[0])).reshape(1, LANES)
    dskip_x = jnp.repeat(d_skip.astype(F32), p_dim).reshape(1, ssm_w)
    expand = (jnp.arange(LANES)[:, None] == (jnp.arange(ssm_w)[None, :] // p_dim)).astype(F32)
    in_args = (row(norm_mix_g), w_in, w_dt, row(q_norm_g), row(k_norm_g))
    in_kw = dict(attn_w=attn_w, ssm_w=ssm_w, conv_ch=conv_ch)
    ssd_kw = dict(ssm_w=ssm_w, groups=groups, n_state=n_state)
    ssd_par = (conv_w, row(conv_b), padl(dt_bias), padl(a_log), dskip_x)

    xp2 = x_prompt.reshape(b * s, d)
    q, k, v, z, xbc, dtr, w_gate_b, w_up_b, w_down_b = _inproj(
        xp2, *in_args, side=(w_gate, w_up, w_down), tm=_row_tile(b * s, 1024), **in_kw)
    moba_steps, ssd_steps = b * n_heads, b * (s // SSD_CHUNK)
    pg_moba, pg_ssd = _page_split(n_phys, moba_steps, s // MOBA_BLOCK, ssd_steps)
    ps_ssd0 = n_phys - ssd_steps * pg_ssd
    assert 0 <= ps_ssd0 <= moba_steps * pg_moba <= n_phys
    attn, ps_a = _moba_prompt(q.reshape(b, s, attn_w), k.reshape(b, s, attn_w), v.reshape(b, s, attn_w),
                              cache_k, n_heads=n_heads, pages_per_step=pg_moba)
    sn, conv_p, ssm_p, ps_b = _ssd_prompt(xbc.reshape(b, s, conv_ch), dtr.reshape(b, s, LANES),
                                          z.reshape(b, s, ssm_w), *ssd_par, row(ssm_out_g), expand, cache_k,
                                          page0=ps_ssd0, pages_per_step=pg_ssd, **ssd_kw)
    h_p = _outproj(attn.reshape(b * s, attn_w), sn.reshape(b * s, ssm_w), xp2, row(attn_out_g), w_out_b,
                   tm=_row_tile(b * s, 512))
    y_p = _ffn(h_p, row(norm_ffn_g), w_gate_b, w_up_b, w_down_b,
               tm=_row_tile(b * s, 1024)).reshape(b, s, d)
    page_shape = (b, s // page, page, n_heads, dh)
    k_p, v_p = k.reshape(page_shape), v.reshape(page_shape)

    ms = db * t_len
    xs2 = x_sample.reshape(ms, d)
    qs, k_s, v_s, zs, xbcs, dtrs = _inproj(xs2, *in_args, tm=_row_tile(ms, 1024), **in_kw)
    qs3, ks3, vs3 = (a.reshape(db, t_len, attn_w) for a in (qs, k_s, v_s))
    sel = _sample_select(page_table, ps_a, ps_b, qs3, n_heads=n_heads, page=page, split=ps_ssd0)
    idx = sel[:, :, :t_len, :MOBA_TOP_K].reshape(db, n_heads * t_len * MOBA_TOP_K)
    ok = sel[:, :, :t_len, MOBA_TOP_K:2 * MOBA_TOP_K].reshape(db, n_heads * t_len * MOBA_TOP_K)
    attn_s = _sample_attend(page_table, idx, ok, qs3, ks3, vs3, cache_k, cache_v, n_heads=n_heads)
    sns, conv_s, ssm_s = _ssd_sample(xbcs.reshape(db, t_len, conv_ch), state_conv,
                                     dtrs.reshape(db, t_len, LANES), zs.reshape(db, t_len, ssm_w), state_ssm,
                                     *ssd_par, row(ssm_out_g), expand, **ssd_kw)
    tms = _row_tile(ms, 512)
    h_s = _outproj(attn_s.reshape(ms, attn_w), sns.reshape(ms, ssm_w), xs2, row(attn_out_g), w_out_b, tm=tms)
    y_s = _ffn(h_s, row(norm_ffn_g), w_gate_b, w_up_b, w_down_b, tm=tms).reshape(db, t_len, d)
    tok_shape = (db, t_len, n_heads, dh)
    return (y_p, y_s, k_p, v_p, conv_p, ssm_p, k_s.reshape(tok_shape), v_s.reshape(tok_shape),
            conv_s, ssm_s)
```

```python
import functools

import jax
import jax.numpy as jnp
from jax import lax
from jax.experimental import pallas as pl
from jax.experimental.pallas import tpu as pltpu

F32 = jnp.float32
BF16 = jnp.bfloat16
EPS = 1e-6
NEG = -1e30

MOBA_BLOCK = 256
MOBA_TOP_K = 3
SSD_CHUNK = 128
SSM_HEAD_DIM = 64
LANES = 128
SUBLANES = 8
VMEM_LIMIT = 56 * 1024 * 1024
BIG_VMEM_LIMIT = 60 * 1024 * 1024

_NT = (((1,), (1,)), ((), ()))
_TN = (((0,), (0,)), ((), ()))


def _params(sem, vmem=VMEM_LIMIT):
    return pltpu.CompilerParams(dimension_semantics=sem, vmem_limit_bytes=vmem)


def _rms(x, g):
    return x * lax.rsqrt(jnp.mean(x * x, axis=-1, keepdims=True) + EPS) * g


def _silu(x):
    return x * (1.0 / (1.0 + jnp.exp(-x)))


def _softplus(x):
    return jnp.maximum(x, 0.0) + jnp.log1p(jnp.exp(-jnp.abs(x)))


def _split3(x):
    hi = x.astype(BF16)
    rest = x - hi.astype(F32)
    mid = rest.astype(BF16)
    return hi, mid, (rest - mid.astype(F32)).astype(BF16)


def _alibi_slope(h):
    bits = (jnp.full((1, 1), 126, jnp.int32) - h) << 23
    return lax.bitcast_convert_type(bits, F32)


def _inproj_kernel(*refs, tn, seg, n_side):
    x_ref, g_ref, w_ref, wdt_ref, qg_ref, kg_ref = refs[:6]
    q_ref, k_ref, v_ref, z_ref, xbc_ref, dt_ref = refs[6 + n_side:12 + n_side]
    xn_ref = refs[-1]
    for src, dst in zip(refs[6:6 + n_side], refs[12 + n_side:12 + 2 * n_side]):
        dst[...] = src[...].astype(BF16)
    j = pl.program_id(1)

    @pl.when(j == 0)
    def _():
        xn_ref[...] = _rms(x_ref[...], g_ref[...]).astype(BF16)
        dt_ref[...] = jnp.dot(xn_ref[...], wdt_ref[...], preferred_element_type=F32)

    acc = jnp.dot(xn_ref[...], w_ref[...].astype(BF16), preferred_element_type=F32)

    def headnorm(out_ref, g):
        for hh in range(tn // LANES):
            sl = slice(hh * LANES, (hh + 1) * LANES)
            out_ref[:, sl] = _rms(acc[:, sl], g)

    @pl.when(j < seg[0])
    def _():
        headnorm(q_ref, qg_ref[...])

    @pl.when((j >= seg[0]) & (j < seg[1]))
    def _():
        headnorm(k_ref, kg_ref[...])

    @pl.when((j >= seg[1]) & (j < seg[2]))
    def _():
        v_ref[...] = acc

    @pl.when((j >= seg[2]) & (j < seg[3]))
    def _():
        z_ref[...] = acc

    @pl.when(j >= seg[3])
    def _():
        xbc_ref[...] = acc


def _grid_tiling(shape, ni, nj):
    for (gr, gc), imap in (((ni, nj), lambda i, j: (i, j)), ((nj, ni), lambda i, j: (j, i))):
        if shape[0] % gr == 0 and shape[1] % gc == 0:
            br, bc = shape[0] // gr, shape[1] // gc
            if br % (2 * SUBLANES) == 0 and bc % LANES == 0:
                return pl.BlockSpec((br, bc), imap)
    return None


def _inproj(x2, norm_g, w_all, w_dt, q_g, k_g, side=(), *, attn_w, ssm_w, conv_ch, tm, tn=512):
    m, d = x2.shape
    n_main = 3 * attn_w + ssm_w + conv_ch
    assert n_main % tn == 0
    seg = (attn_w // tn, 2 * attn_w // tn, 3 * attn_w // tn, (3 * attn_w + ssm_w) // tn)
    nj = n_main // tn
    side_specs = [_grid_tiling(w.shape, m // tm, nj) for w in side]
    assert all(sp is not None for sp in side_specs)

    def col(lo, hi):
        return lambda i, j: (i, jnp.clip(j - lo, 0, hi - lo - 1))

    out_shape = (
        jax.ShapeDtypeStruct((m, attn_w), F32), jax.ShapeDtypeStruct((m, attn_w), F32),
        jax.ShapeDtypeStruct((m, attn_w), F32), jax.ShapeDtypeStruct((m, ssm_w), F32),
        jax.ShapeDtypeStruct((m, conv_ch), F32), jax.ShapeDtypeStruct((m, LANES), F32),
        *(jax.ShapeDtypeStruct(w.shape, BF16) for w in side))
    return pl.pallas_call(
        functools.partial(_inproj_kernel, tn=tn, seg=seg, n_side=len(side)),
        grid=(m // tm, nj),
        in_specs=[
            pl.BlockSpec((tm, d), lambda i, j: (i, 0)),
            pl.BlockSpec((1, d), lambda i, j: (0, 0)),
            pl.BlockSpec((d, tn), lambda i, j: (0, j)),
            pl.BlockSpec((d, LANES), lambda i, j: (0, 0)),
            pl.BlockSpec((1, LANES), lambda i, j: (0, 0)),
            pl.BlockSpec((1, LANES), lambda i, j: (0, 0)),
            *side_specs,
        ],
        out_specs=(
            pl.BlockSpec((tm, tn), col(0, seg[0])),
            pl.BlockSpec((tm, tn), col(seg[0], seg[1])),
            pl.BlockSpec((tm, tn), col(seg[1], seg[2])),
            pl.BlockSpec((tm, tn), col(seg[2], seg[3])),
            pl.BlockSpec((tm, tn), col(seg[3], nj)),
            pl.BlockSpec((tm, LANES), lambda i, j: (i, 0)),
            *side_specs,
        ),
        out_shape=out_shape,
        scratch_shapes=[pltpu.VMEM((tm, d), BF16)],
        compiler_params=_params(("parallel", "arbitrary")),
        name="inproj",
    )(x2, norm_g, w_all, w_dt, q_g, k_g, *side)


def _block_select(gate_t, n_past):
    row = lax.broadcasted_iota(jnp.int32, gate_t.shape, 0)
    valid = row < n_past
    if n_past <= MOBA_TOP_K:
        return valid & (gate_t > 0.5 * NEG)
    rank = jnp.zeros(gate_t.shape, jnp.int32)
    for j in range(n_past):
        other = gate_t[j:j + 1, :]
        rank = rank + ((other > gate_t) | ((other == gate_t) & (row > j))).astype(jnp.int32)
    return valid & (rank < MOBA_TOP_K) & (gate_t > 0.5 * NEG)


def _page_chunk_copy(ck_hbm, buf, sem, first_page, n_pages):
    return pltpu.make_async_copy(ck_hbm.at[pl.ds(first_page, n_pages)], buf, sem)


def _page_key_sums(pages, lanes_of_adds=4):
    n, page, heads, dh = pages.shape
    part = pages.reshape(n * lanes_of_adds, page // lanes_of_adds, heads, dh).sum(axis=1)
    return part.reshape(n, lanes_of_adds, heads, dh).sum(axis=1)


def _stream_page_sums(ck_hbm, ps_ref, pbuf, psem, step, n_steps, page0):
    n_pg = pbuf.shape[1]
    slot = step % 2

    def page_copy(st, sl):
        return _page_chunk_copy(ck_hbm, pbuf.at[sl], psem.at[sl], page0 + st * n_pg, n_pg)

    @pl.when(step == 0)
    def _():
        page_copy(0, 0).start()

    page_copy(step, slot).wait()

    @pl.when(step + 1 < n_steps)
    def _():
        page_copy(step + 1, 1 - slot).start()

    ps_ref[...] = _page_key_sums(pbuf[slot])


def _causal_chunks(total, nb):
    weights = [i + 2 for i in range(nb - 1)] + [1]
    cum, acc = [0], 0
    for w in weights:
        acc += w
        cum.append((total * acc + sum(weights) // 2) // sum(weights))
    return [hi - lo for lo, hi in zip(cum[:-1], cum[1:])], cum[:-1]


def _moba_prompt_kernel(q_ref, k_ref, v_ref, ck_hbm, o_ref, ps_ref, pbuf, psem, *, nb, scale, pages):
    blk = MOBA_BLOCK
    h = pl.program_id(1)
    step = pl.program_id(0) * pl.num_programs(1) + h
    n_steps = pl.num_programs(0) * pl.num_programs(1)
    sizes, offs = _causal_chunks(pages, nb)
    chunks = [i for i in range(nb) if sizes[i] > 0]
    n_ch = len(chunks)
    n_slots = pbuf.shape[0]
    ahead = n_slots - 1
    assert n_ch >= ahead

    def page_copy(c):
        st, cc = divmod(c, n_ch)
        i = chunks[cc]
        slot = (step * n_ch + c) % n_slots
        return _page_chunk_copy(ck_hbm, pbuf.at[slot, pl.ds(0, sizes[i])], psem.at[slot],
                                (step + st) * pages + offs[i], sizes[i])

    @pl.when(step == 0)
    def _():
        for c in range(ahead):
            page_copy(c).start()

    slope = _alibi_slope(h)
    k = k_ref[...]
    kb = k.astype(BF16)
    vb = v_ref[...].astype(BF16)
    nb_pad = -(-nb // SUBLANES) * SUBLANES
    means = [jnp.mean(k[j * blk:(j + 1) * blk], axis=0, keepdims=True) for j in range(nb)]
    means = jnp.concatenate(means + [jnp.zeros((nb_pad - nb, LANES), F32)] * (nb_pad > nb), axis=0)
    for i in range(nb):
        qi = q_ref[i * blk:(i + 1) * blk, :]
        nk = (i + 1) * blk
        s = lax.dot_general(qi.astype(BF16), kb[:nk], _NT, preferred_element_type=F32)
        col = lax.broadcasted_iota(jnp.int32, (blk, nk), 1)
        logits = s * scale + slope * col.astype(F32)
        row = lax.broadcasted_iota(jnp.int32, (blk, blk), 0)
        colb = lax.broadcasted_iota(jnp.int32, (blk, blk), 1)
        pieces = []
        if i > 0:
            gate_t = lax.dot_general(means, qi, _NT, precision=lax.Precision.HIGHEST,
                                     preferred_element_type=F32)
            sel_t = _block_select(gate_t, i).astype(F32)
            sel_t = jnp.concatenate([sel_t, jnp.zeros((LANES - nb_pad, blk), F32)], axis=0)
            sel = jnp.concatenate([sel_t[:, r * LANES:(r + 1) * LANES].T
                                   for r in range(blk // LANES)], axis=0)
            pieces = [jnp.broadcast_to(sel[:, j:j + 1], (blk, blk)) for j in range(i)]
        pieces.append((colb <= row).astype(F32))
        allowed = jnp.concatenate(pieces, axis=1) if len(pieces) > 1 else pieces[0]
        logits = jnp.where(allowed > 0.5, logits, NEG)
        m = jnp.max(logits, axis=-1, keepdims=True)
        p = jnp.exp(logits - m)
        denom = jnp.sum(p, axis=-1, keepdims=True)
        out = jnp.dot(p.astype(BF16), vb[:nk], preferred_element_type=F32)
        o_ref[i * blk:(i + 1) * blk, :] = out / denom
        if i in chunks:
            c = chunks.index(i)
            page_copy(c).wait()
            if c + ahead < n_ch:
                page_copy(c + ahead).start()
            else:
                @pl.when(step + 1 < n_steps)
                def _():
                    page_copy(c + ahead).start()
            slot = (step * n_ch + c) % n_slots
            ps_ref[offs[i]:offs[i] + sizes[i]] = _page_key_sums(pbuf[slot, 0:sizes[i]])


def _moba_prompt(q, k, v, cache_k, *, n_heads, pages_per_step):
    b, s, w = q.shape
    dh = w // n_heads
    nb = s // MOBA_BLOCK
    npc = max(_causal_chunks(pages_per_step, nb)[0])
    page, ch, cdh = cache_k.shape[1:]
    spec = pl.BlockSpec((None, s, dh), lambda bi, hi: (bi, 0, hi))
    return pl.pallas_call(
        functools.partial(_moba_prompt_kernel, nb=nb, scale=dh ** -0.5, pages=pages_per_step),
        grid=(b, n_heads),
        in_specs=[spec, spec, spec, pl.BlockSpec(memory_space=pl.ANY)],
        out_specs=(spec, pl.BlockSpec((pages_per_step, ch, cdh), lambda bi, hi: (bi * n_heads + hi, 0, 0))),
        out_shape=(jax.ShapeDtypeStruct((b, s, w), F32),
                   jax.ShapeDtypeStruct((b * n_heads * pages_per_step, ch, cdh), F32)),
        scratch_shapes=[pltpu.VMEM((3, npc, page, ch, cdh), F32), pltpu.SemaphoreType.DMA((3,))],
        compiler_params=_params(("arbitrary", "arbitrary")),
        name="moba_prompt",
    )(q, k, v, cache_k)


def _gated_norm(y, z, g):
    return _rms(y * _silu(z), g).astype(BF16)


def _ssd_prompt_kernel(xbc_ref, dtr_ref, z_ref, cw_ref, cb_ref, dtb_ref, alog_ref, dskip_ref, sg_ref,
                       expand_ref, wof_ref, ck_hbm, s_ref, convp_ref, h_ref, ps_ref, wob_ref,
                       xp_ref, pbuf, psem, *, cw, ssm_w, groups, n_state, page0):
    wob_ref[...] = wof_ref[...].astype(BF16)
    c = pl.program_id(1)
    nc = pl.num_programs(1)
    _stream_page_sums(ck_hbm, ps_ref, pbuf, psem, pl.program_id(0) * nc + c, pl.num_programs(0) * nc,
                      page0)
    lc = SSD_CHUNK
    p_dim = SSM_HEAD_DIM
    heads = ssm_w // p_dim
    hpg = heads // groups
    gw = hpg * p_dim
    tail = cw - 1

    @pl.when(c == 0)
    def _():
        xp_ref[0:SUBLANES, :] = jnp.zeros((SUBLANES, xp_ref.shape[1]), F32)
        h_ref[...] = jnp.zeros(h_ref.shape, F32)

    xc = xbc_ref[...]
    xp_ref[SUBLANES:SUBLANES + lc, :] = xc
    conv = cb_ref[...]
    for i in range(cw):
        conv = conv + xp_ref[SUBLANES - tail + i:SUBLANES - tail + i + lc, :] * cw_ref[i:i + 1, :]
    act = _silu(conv)
    xp_ref[SUBLANES - tail:SUBLANES, :] = xc[lc - tail:lc, :]

    @pl.when(c == nc - 1)
    def _():
        convp_ref[...] = xc[lc - tail:lc, :]

    dt = _softplus(dtr_ref[...] + dtb_ref[...])
    a = -jnp.exp(alog_ref[...])
    row = lax.broadcasted_iota(jnp.int32, (lc, lc), 0)
    colm = lax.broadcasted_iota(jnp.int32, (lc, lc), 1)
    causal = row >= colm
    tri = causal.astype(BF16)
    cum = sum(jnp.dot(tri, t, preferred_element_type=F32) for t in _split3(dt * a))
    cum_t = cum.T
    dt_t = dt.T
    cum_last = cum[lc - 1:lc, :]
    dec_end = jnp.exp(cum_last - cum) * dt
    ecum = jnp.exp(cum)
    expand = expand_ref[...].astype(BF16)
    ecum_x = sum(jnp.dot(t, expand, preferred_element_type=F32) for t in _split3(ecum))
    dec_end_x = sum(jnp.dot(t, expand, preferred_element_type=F32) for t in _split3(dec_end))

    xs = act[:, :ssm_w]
    xs_b = xs.astype(BF16)
    y = dskip_ref[...] * xs
    lane = lax.broadcasted_iota(jnp.int32, (lc, 2 * p_dim), 1)
    for g in range(groups):
        bg = act[:, ssm_w + g * n_state:ssm_w + (g + 1) * n_state].astype(BF16)
        cg = act[:, ssm_w + (groups + g) * n_state:ssm_w + (groups + g + 1) * n_state].astype(BF16)
        cb = lax.dot_general(cg, bg, _NT, preferred_element_type=F32)
        hg = h_ref[g * hpg:(g + 1) * hpg].reshape(gw, n_state)
        y_off = lax.dot_general(cg, hg.astype(BF16), _NT, preferred_element_type=F32)
        y_off = y_off * ecum_x[:, g * gw:(g + 1) * gw]
        y_diag = []
        for pr in range(hpg // 2):
            ws = []
            for e in (g * hpg + 2 * pr, g * hpg + 2 * pr + 1):
                diff = cum[:, e:e + 1] - cum_t[e:e + 1, :]
                decay = jnp.where(causal, jnp.exp(diff), 0.0)
                ws.append((cb * decay * dt_t[e:e + 1, :]).astype(BF16))
            lo = (g * hpg + 2 * pr) * p_dim
            xpair = xs_b[:, lo:lo + 2 * p_dim]
            zero = jnp.zeros_like(xpair)
            rhs = jnp.concatenate([jnp.where(lane < p_dim, xpair, zero),
                                   jnp.where(lane >= p_dim, xpair, zero)], axis=0)
            y_diag.append(jnp.dot(jnp.concatenate(ws, axis=1), rhs, preferred_element_type=F32))
        y_g = jnp.concatenate(y_diag, axis=1) + y_off
        xw = (xs[:, g * gw:(g + 1) * gw] * dec_end_x[:, g * gw:(g + 1) * gw]).astype(BF16)
        st = lax.dot_general(xw, bg, _TN, preferred_element_type=F32)
        for ee in range(hpg):
            e = g * hpg + ee
            cd = jnp.exp(cum_t[e:e + 1, lc - 1:lc])
            h_ref[e] = h_ref[e] * cd + st[ee * p_dim:(ee + 1) * p_dim, :]
        gl = slice(g * gw, (g + 1) * gw)
        s_ref[:, gl] = _gated_norm(y[:, gl] + y_g, z_ref[:, gl], sg_ref[:, gl])


def _ssd_prompt(xbc, dtr, z, conv_w, conv_b, dtb, alog, dskip_x, ssm_g, expand, w_side, cache_k,
                *, ssm_w, groups, n_state, page0, pages_per_step):
    b, s, cc = xbc.shape
    cw = conv_w.shape[0]
    heads = ssm_w // SSM_HEAD_DIM
    lc = SSD_CHUNK
    nc = s // lc
    page, ch, cdh = cache_k.shape[1:]
    const = lambda shape: pl.BlockSpec(shape, lambda bi, ci: (0,) * len(shape))
    side_spec = _grid_tiling(w_side.shape, b, nc)
    assert side_spec is not None
    return pl.pallas_call(
        functools.partial(_ssd_prompt_kernel, cw=cw, ssm_w=ssm_w, groups=groups, n_state=n_state,
                          page0=page0),
        grid=(b, nc),
        in_specs=[
            pl.BlockSpec((None, lc, cc), lambda bi, ci: (bi, ci, 0)),
            pl.BlockSpec((None, lc, LANES), lambda bi, ci: (bi, ci, 0)),
            pl.BlockSpec((None, lc, ssm_w), lambda bi, ci: (bi, ci, 0)),
            const((cw, cc)), const((1, cc)), const((1, LANES)), const((1, LANES)),
            const((1, ssm_w)), const((1, ssm_w)), const((LANES, ssm_w)),
            side_spec,
            pl.BlockSpec(memory_space=pl.ANY),
        ],
        out_specs=(
            pl.BlockSpec((None, lc, ssm_w), lambda bi, ci: (bi, ci, 0)),
            pl.BlockSpec((None, cw - 1, cc), lambda bi, ci: (bi, 0, 0)),
            pl.BlockSpec((None, heads, SSM_HEAD_DIM, n_state), lambda bi, ci: (bi, 0, 0, 0)),
            pl.BlockSpec((pages_per_step, ch, cdh), lambda bi, ci: (bi * nc + ci, 0, 0)),
            side_spec,
        ),
        out_shape=(
            jax.ShapeDtypeStruct((b, s, ssm_w), BF16),
            jax.ShapeDtypeStruct((b, cw - 1, cc), F32),
            jax.ShapeDtypeStruct((b, heads, SSM_HEAD_DIM, n_state), F32),
            jax.ShapeDtypeStruct((b * nc * pages_per_step, ch, cdh), F32),
            jax.ShapeDtypeStruct(w_side.shape, BF16),
        ),
        scratch_shapes=[pltpu.VMEM((SUBLANES + lc, cc), F32),
                        pltpu.VMEM((2, pages_per_step, page, ch, cdh), F32),
                        pltpu.SemaphoreType.DMA((2,))],
        compiler_params=_params(("arbitrary", "arbitrary")),
        name="ssd_prompt",
    )(xbc, dtr, z, conv_w, conv_b, dtb, alog, dskip_x, ssm_g, expand, w_side, cache_k)


def _outproj_kernel(attn_ref, s_ref, x_ref, ag_ref, w_ref, o_ref, *, rc):
    for r in range(o_ref.shape[0] // rc):
        rows = slice(r * rc, (r + 1) * rc)
        a = _rms(attn_ref[rows, :], ag_ref[...]).astype(BF16)
        mix = jnp.concatenate([a, s_ref[rows, :]], axis=1)
        o_ref[rows, :] = x_ref[rows, :] + jnp.dot(mix, w_ref[...], preferred_element_type=F32)


def _outproj(attn, s_norm, x2, attn_g, w_out, *, tm, rc=256):
    m, attn_w = attn.shape
    ssm_w = s_norm.shape[1]
    d = x2.shape[1]
    return pl.pallas_call(
        functools.partial(_outproj_kernel, rc=min(rc, tm)),
        grid=(m // tm,),
        in_specs=[
            pl.BlockSpec((tm, attn_w), lambda i: (i, 0)),
            pl.BlockSpec((tm, ssm_w), lambda i: (i, 0)),
            pl.BlockSpec((tm, d), lambda i: (i, 0)),
            pl.BlockSpec((1, attn_w), lambda i: (0, 0)),
            pl.BlockSpec((attn_w + ssm_w, d), lambda i: (0, 0)),
        ],
        out_specs=pl.BlockSpec((tm, d), lambda i: (i, 0)),
        out_shape=jax.ShapeDtypeStruct((m, d), F32),
        compiler_params=_params(("parallel",)),
        name="outproj",
    )(attn, s_norm, x2, attn_g, w_out)


def _ffn_kernel(h_ref, g_ref, wg_ref, wu_ref, wd_ref, o_ref, hn_ref, *, rc):
    f = pl.program_id(1)

    @pl.when(f == 0)
    def _():
        hn_ref[...] = _rms(h_ref[...], g_ref[...]).astype(BF16)
        o_ref[...] = h_ref[...]

    for r in range(o_ref.shape[0] // rc):
        rows = slice(r * rc, (r + 1) * rc)
        hn = hn_ref[rows, :]
        gate = jnp.dot(hn, wg_ref[...], preferred_element_type=F32)
        up = jnp.dot(hn, wu_ref[...], preferred_element_type=F32)
        o_ref[rows, :] += jnp.dot((_silu(gate) * up).astype(BF16), wd_ref[...],
                                  preferred_element_type=F32)


def _ffn(h, g, w_gate, w_up, w_down, *, tm, tf=512, rc=512):
    m, d = h.shape
    dff = w_gate.shape[1]
    return pl.pallas_call(
        functools.partial(_ffn_kernel, rc=min(rc, tm)),
        grid=(m // tm, dff // tf),
        in_specs=[
            pl.BlockSpec((tm, d), lambda i, f: (i, 0)),
            pl.BlockSpec((1, d), lambda i, f: (0, 0)),
            pl.BlockSpec((d, tf), lambda i, f: (0, f)),
            pl.BlockSpec((d, tf), lambda i, f: (0, f)),
            pl.BlockSpec((tf, d), lambda i, f: (f, 0)),
        ],
        out_specs=pl.BlockSpec((tm, d), lambda i, f: (i, 0)),
        out_shape=jax.ShapeDtypeStruct((m, d), F32),
        scratch_shapes=[pltpu.VMEM((tm, d), BF16)],
        compiler_params=_params(("parallel", "arbitrary"), vmem=BIG_VMEM_LIMIT),
        name="ffn",
    )(h, g, w_gate, w_up, w_down)


def _sample_select_kernel(pt_ref, psa_ref, psb_ref, q_ref, sel_ref, blk_ref,
                          *, nb, ppb, n_heads, t_len, split):
    b = pl.program_id(0)

    def page_sum(pg):
        lo = psa_ref[jnp.minimum(pg, split - 1)]
        hi = psb_ref[jnp.maximum(pg - split, 0)]
        return jnp.where(pg < split, lo, hi)

    for n in range(nb):
        acc = page_sum(pt_ref[b, n * ppb])
        for pp in range(1, ppb):
            acc = acc + page_sum(pt_ref[b, n * ppb + pp])
        blk_ref[n] = acc / MOBA_BLOCK
    lane = lax.broadcasted_iota(jnp.int32, (SUBLANES, LANES), 1)
    for h in range(n_heads):
        qh = q_ref[:, h * LANES:(h + 1) * LANES]
        q8 = jnp.concatenate([qh, jnp.zeros((SUBLANES - t_len, LANES), F32)], axis=0)
        bm = blk_ref[:, h, :]
        gate = lax.dot_general(q8, bm, _NT, precision=lax.Precision.HIGHEST,
                               preferred_element_type=F32)
        lane_n = lax.broadcasted_iota(jnp.int32, gate.shape, 1).astype(F32)
        out = jnp.zeros((SUBLANES, LANES), jnp.int32)
        for kk in range(MOBA_TOP_K):
            mx = jnp.max(gate, axis=-1, keepdims=True)
            idx_f = jnp.min(jnp.where(gate == mx, lane_n, float(nb)), axis=-1, keepdims=True)
            idx = idx_f.astype(jnp.int32)
            ok = (mx > 0.5 * NEG).astype(jnp.int32)
            out = jnp.where(lane == kk, idx, out)
            out = jnp.where(lane == MOBA_TOP_K + kk, ok, out)
            gate = jnp.where(lane_n == idx_f, -jnp.inf, gate)
        sel_ref[h] = out


def _sample_select(page_table, ps_lo, ps_hi, q_s, *, n_heads, page, split):
    db, t_len, w = q_s.shape
    n_pages = page_table.shape[1]
    _, h, dh = ps_lo.shape
    ppb = MOBA_BLOCK // page
    nb = n_pages // ppb
    whole = lambda a: pl.BlockSpec(a.shape, lambda bi, pt: (0, 0, 0), pipeline_mode=pl.Buffered(1))
    grid_spec = pltpu.PrefetchScalarGridSpec(
        num_scalar_prefetch=1,
        grid=(db,),
        in_specs=[whole(ps_lo), whole(ps_hi),
                  pl.BlockSpec((None, t_len, w), lambda bi, pt: (bi, 0, 0))],
        out_specs=pl.BlockSpec((None, n_heads, SUBLANES, LANES), lambda bi, pt: (bi, 0, 0, 0)),
        scratch_shapes=[pltpu.VMEM((nb, h, dh), F32)],
    )
    return pl.pallas_call(
        functools.partial(_sample_select_kernel, nb=nb, ppb=ppb, n_heads=n_heads, t_len=t_len,
                          split=split),
        grid_spec=grid_spec,
        out_shape=jax.ShapeDtypeStruct((db, n_heads, SUBLANES, LANES), jnp.int32),
        compiler_params=_params(("arbitrary",)),
        name="sample_select",
    )(page_table, ps_lo, ps_hi, q_s)


def _sample_attend_kernel(pt_ref, idx_ref, ok_ref, q_ref, kn_ref, vn_ref, ck_hbm, cv_hbm, o_ref,
                          kbuf, vbuf, sem, *, t_len, page, ppb, past_len, scale, hps):
    b = pl.program_id(0)
    hg = pl.program_id(1)
    n_hg = pl.num_programs(1)
    step = b * n_hg + hg
    cur = step % 2
    n_sel = t_len * MOBA_TOP_K * ppb
    n_pages = past_len // page

    def copies(st, hl, hh, slot, phys):
        return (pltpu.make_async_copy(ck_hbm.at[phys, :, hh, :], kbuf.at[st, hl, slot], sem.at[st, 0]),
                pltpu.make_async_copy(cv_hbm.at[phys, :, hh, :], vbuf.at[st, hl, slot], sem.at[st, 1]))

    def slot_pages(bb, hh):
        out = []
        for t in range(t_len):
            for kk in range(MOBA_TOP_K):
                n = idx_ref[bb, (hh * t_len + t) * MOBA_TOP_K + kk]
                for pp in range(ppb):
                    lp = jnp.clip(n * ppb + pp, 0, n_pages - 1)
                    out.append(((t * MOBA_TOP_K + kk) * ppb + pp, pt_ref[bb, lp], n * ppb + pp))
        out.append((n_sel, pt_ref[bb, n_pages - 1], None))
        return out

    def start_all(st, bb, hgg):
        for hl in range(hps):
            hh = hgg * hps + hl
            for slot, phys, _ in slot_pages(bb, hh):
                for cp in copies(st, hl, hh, slot, phys):
                    cp.start()

    @pl.when(step == 0)
    def _():
        start_all(0, b, hg)

    wrap = hg + 1 == n_hg
    b_next = b + wrap.astype(jnp.int32)
    hg_next = jnp.where(wrap, 0, hg + 1)

    @pl.when(step + 1 < pl.num_programs(0) * n_hg)
    def _():
        start_all(1 - cur, b_next, hg_next)

    n_keys = (n_sel + 2) * page
    lane = lax.broadcasted_iota(jnp.int32, (1, page), 1)
    trow = lax.broadcasted_iota(jnp.int32, (SUBLANES, n_keys), 0)
    kcol = lax.broadcasted_iota(jnp.int32, (SUBLANES, n_keys), 1)
    q_pos = past_len + trow
    blk_start = (q_pos // MOBA_BLOCK) * MOBA_BLOCK
    per_t = MOBA_TOP_K * ppb * page
    sel_cols = n_sel * page
    pad = jnp.zeros((page - t_len, LANES), F32)

    head_pages = [slot_pages(b, hg * hps + hl) for hl in range(hps)]
    for hl in range(hps):
        hcols = slice(hl * LANES, (hl + 1) * LANES)
        kbuf[cur, hl, n_sel + 1] = jnp.concatenate([kn_ref[:, hcols], pad], axis=0)
        vbuf[cur, hl, n_sel + 1] = jnp.concatenate([vn_ref[:, hcols], pad], axis=0)

    for hl in range(hps):
        for slot, phys, _ in head_pages[hl]:
            for cp in copies(cur, hl, hg * hps + hl, slot, phys):
                cp.wait()

    for hl in range(hps):
        h = hg * hps + hl
        hcols = slice(hl * LANES, (hl + 1) * LANES)
        pos, okv = [], []
        for slot, _, lp in head_pages[hl][:-1]:
            n_ok = ok_ref[b, (h * t_len + slot // (MOBA_TOP_K * ppb)) * MOBA_TOP_K
                          + (slot // ppb) % MOBA_TOP_K]
            pos.append(lane + jnp.clip(lp, 0, n_pages - 1) * page)
            okv.append(jnp.broadcast_to(n_ok, (1, page)))
        pos.append(lane + (past_len - page))
        pos.append(lane + past_len)
        pos = jnp.concatenate(pos, axis=1)
        okv = jnp.concatenate(okv + [jnp.zeros((1, n_keys - sel_cols), jnp.int32)], axis=1)

        kall = kbuf[cur, hl].reshape(n_keys, LANES).astype(BF16)
        vall = vbuf[cur, hl].reshape(n_keys, LANES).astype(BF16)
        q8 = jnp.concatenate([q_ref[:, hcols], jnp.zeros((SUBLANES - t_len, LANES), F32)], axis=0)
        s = lax.dot_general(q8.astype(BF16), kall, _NT, preferred_element_type=F32)

        dist = (q_pos - pos).astype(F32)
        logits = s * scale - _alibi_slope(h) * dist
        sel_ok = (kcol // per_t == trow) & (okv > 0)
        own_ok = ((kcol >= sel_cols) & (pos <= q_pos) & (pos >= blk_start)
                  & (kcol < sel_cols + page + t_len))
        logits = jnp.where(sel_ok | own_ok, logits, NEG)
        m = jnp.max(logits, axis=-1, keepdims=True)
        p = jnp.exp(logits - m)
        denom = jnp.sum(p, axis=-1, keepdims=True)
        out = jnp.dot(p.astype(BF16), vall, preferred_element_type=F32) / denom
        o_ref[:, hcols] = out[:t_len, :]


def _sample_attend(page_table, idx, ok, q_s, k_s, v_s, cache_k, cache_v, *, n_heads, hps=4):
    db, t_len, w = q_s.shape
    page = cache_k.shape[1]
    n_pages = page_table.shape[1]
    ppb = MOBA_BLOCK // page
    n_slots = t_len * MOBA_TOP_K * ppb + 2
    assert n_heads % hps == 0
    tok = pl.BlockSpec((None, t_len, hps * LANES), lambda bi, hi, *_: (bi, 0, hi))
    grid_spec = pltpu.PrefetchScalarGridSpec(
        num_scalar_prefetch=3,
        grid=(db, n_heads // hps),
        in_specs=[tok, tok, tok,
                  pl.BlockSpec(memory_space=pl.ANY), pl.BlockSpec(memory_space=pl.ANY)],
        out_specs=tok,
        scratch_shapes=[pltpu.VMEM((2, hps, n_slots, page, LANES), F32),
                        pltpu.VMEM((2, hps, n_slots, page, LANES), F32),
                        pltpu.SemaphoreType.DMA((2, 2))],
    )
    return pl.pallas_call(
        functools.partial(_sample_attend_kernel, t_len=t_len, page=page, ppb=ppb,
                          past_len=n_pages * page, scale=LANES ** -0.5, hps=hps),
        grid_spec=grid_spec,
        out_shape=jax.ShapeDtypeStruct((db, t_len, w), F32),
        compiler_params=_params(("arbitrary", "arbitrary")),
        name="sample_attend",
    )(page_table, idx, ok, q_s, k_s, v_s, cache_k, cache_v)


def _ssd_sample_kernel(xbc_ref, sc_ref, dtr_ref, z_ref, h0_ref, cw_ref, cb_ref, dtb_ref, alog_ref,
                       dskip_ref, sg_ref, expand_ref, s_ref, convs_ref, h_ref, xp_ref,
                       *, cw, t_len, ssm_w, groups, n_state):
    p_dim = SSM_HEAD_DIM
    heads = ssm_w // p_dim
    hpg = heads // groups
    gw = hpg * p_dim
    tail = cw - 1
    xp_ref[0:tail, :] = sc_ref[...]
    xp_ref[tail:tail + t_len, :] = xbc_ref[...]
    conv = cb_ref[...]
    for i in range(cw):
        conv = conv + xp_ref[i:i + t_len, :] * cw_ref[i:i + 1, :]
    act = _silu(conv)
    convs_ref[...] = xp_ref[t_len:t_len + tail, :]

    dt = _softplus(dtr_ref[...] + dtb_ref[...])
    da = dt * (-jnp.exp(alog_ref[...]))
    cum = [da[0:1]]
    for t in range(1, t_len):
        cum.append(cum[-1] + da[t:t + 1])
    lane = lax.broadcasted_iota(jnp.int32, (1, LANES), 1)
    xs = act[:, :ssm_w]
    zpad = jnp.zeros((SUBLANES - t_len, n_state), F32)

    b8, c8, cbs = [], [], []
    for g in range(groups):
        bg = act[:, ssm_w + g * n_state:ssm_w + (g + 1) * n_state]
        cg = act[:, ssm_w + (groups + g) * n_state:ssm_w + (groups + g + 1) * n_state]
        b8.append(jnp.concatenate([bg, zpad], axis=0).astype(BF16))
        c8.append(jnp.concatenate([cg, zpad], axis=0).astype(BF16))
        cbs.append(lax.dot_general(c8[g], b8[g], _NT, preferred_element_type=F32))

    pairs = [(l, s) for l in range(t_len) for s in range(l + 1)]
    rows = []
    for l, s in pairs:
        cb_ls = jnp.zeros((1, LANES), F32)
        for g in range(groups):
            in_g = (lane >= g * hpg) & (lane < (g + 1) * hpg)
            cb_ls = jnp.where(in_g, cbs[g][l:l + 1, s:s + 1], cb_ls)
        rows.append(cb_ls * jnp.exp(cum[l] - cum[s]) * dt[s:s + 1])
    rows += [jnp.exp(cum[l]) for l in range(t_len)]
    rows += [jnp.exp(cum[-1] - cum[s]) * dt[s:s + 1] for s in range(t_len)]
    n_rows = -(-len(rows) // SUBLANES) * SUBLANES
    rows.append(jnp.zeros((n_rows - len(rows), LANES), F32))
    rx = jnp.dot(jnp.concatenate(rows, axis=0), expand_ref[...], precision=lax.Precision.HIGHEST,
                 preferred_element_type=F32)
    r_ecum = len(pairs)
    r_coef = r_ecum + t_len

    y_diag = []
    for l in range(t_len):
        acc = jnp.zeros((1, ssm_w), F32)
        for s in range(l + 1):
            i = pairs.index((l, s))
            acc = acc + rx[i:i + 1, :] * xs[s:s + 1, :]
        y_diag.append(acc)
    y = dskip_ref[...] * xs + jnp.concatenate(y_diag, axis=0)

    y_off = []
    for g in range(groups):
        hg = h0_ref[g * hpg:(g + 1) * hpg].reshape(gw, n_state)
        y_off.append(lax.dot_general(c8[g], hg.astype(BF16), _NT, preferred_element_type=F32)[:t_len])
    y = y + jnp.concatenate(y_off, axis=1) * rx[r_ecum:r_ecum + t_len, :]
    for g in range(groups):
        gl = slice(g * gw, (g + 1) * gw)
        s_ref[:, gl] = _gated_norm(y[:, gl], z_ref[:, gl], sg_ref[:, gl])

    e_last = jnp.exp(cum[-1])
    xw = xs * rx[r_coef:r_coef + t_len, :]
    for g in range(groups):
        xw8 = jnp.concatenate([xw[:, g * gw:(g + 1) * gw], jnp.zeros((SUBLANES - t_len, gw), F32)], axis=0)
        st = lax.dot_general(xw8.astype(BF16), b8[g], _TN, preferred_element_type=F32)
        for ee in range(hpg):
            e = g * hpg + ee
            h_ref[e] = h0_ref[e] * e_last[:, e:e + 1] + st[ee * p_dim:(ee + 1) * p_dim, :]


def _ssd_sample(xbc, state_conv, dtr, z, state_ssm, conv_w, conv_b, dtb, alog, dskip_x, ssm_g, expand,
                *, ssm_w, groups, n_state):
    db, t_len, cc = xbc.shape
    cw = conv_w.shape[0]
    heads = ssm_w // SSM_HEAD_DIM
    const = lambda shape: pl.BlockSpec(shape, lambda bi: (0,) * len(shape))
    return pl.pallas_call(
        functools.partial(_ssd_sample_kernel, cw=cw, t_len=t_len, ssm_w=ssm_w, groups=groups,
                          n_state=n_state),
        grid=(db,),
        in_specs=[
            pl.BlockSpec((None, t_len, cc), lambda bi: (bi, 0, 0)),
            pl.BlockSpec((None, cw - 1, cc), lambda bi: (bi, 0, 0)),
            pl.BlockSpec((None, t_len, LANES), lambda bi: (bi, 0, 0)),
            pl.BlockSpec((None, t_len, ssm_w), lambda bi: (bi, 0, 0)),
            pl.BlockSpec((None, heads, SSM_HEAD_DIM, n_state), lambda bi: (bi, 0, 0, 0)),
            const((cw, cc)), const((1, cc)), const((1, LANES)), const((1, LANES)), const((1, ssm_w)),
            const((1, ssm_w)), const((LANES, ssm_w)),
        ],
        out_specs=(
            pl.BlockSpec((None, t_len, ssm_w), lambda bi: (bi, 0, 0)),
            pl.BlockSpec((None, cw - 1, cc), lambda bi: (bi, 0, 0)),
            pl.BlockSpec((None, heads, SSM_HEAD_DIM, n_state), lambda bi: (bi, 0, 0, 0)),
        ),
        out_shape=(
            jax.ShapeDtypeStruct((db, t_len, ssm_w), BF16),
            jax.ShapeDtypeStruct((db, cw - 1, cc), F32),
            jax.ShapeDtypeStruct((db, heads, SSM_HEAD_DIM, n_state), F32),
        ),
        scratch_shapes=[pltpu.VMEM((2 * SUBLANES, cc), F32)],
        compiler_params=_params(("parallel",)),
        name="ssd_sample",
    )(xbc, state_conv, dtr, z, state_ssm, conv_w, conv_b, dtb, alog, dskip_x, ssm_g, expand)


def _row_tile(m, target):
    tm = min(m, target)
    assert m % tm == 0, (m, tm)
    return tm


def _page_split(n_phys, moba_steps, nb, ssd_steps):
    per_moba = max(1, n_phys * 6 // 10 // moba_steps)
    rest = max(n_phys - moba_steps * per_moba, 0)
    per_ssd = max(1, -(-rest // ssd_steps))
    return per_moba, per_ssd


def kernel(x_prompt, x_sample, cache_k, cache_v, page_table, state_conv, state_ssm, norm_mix_g, w_in, q_norm_g, k_norm_g, conv_w, conv_b, dt_bias, a_log, d_skip, attn_out_g, ssm_out_g, w_out, norm_ffn_g, w_gate, w_up, w_down):
    b, s, d = x_prompt.shape
    db, t_len, _ = x_sample.shape
    n_phys, page, n_heads, dh = cache_k.shape
    attn_w = n_heads * dh
    heads, p_dim, n_state = state_ssm.shape[1:]
    assert p_dim == SSM_HEAD_DIM and dh == LANES and heads <= LANES
    ssm_w = heads * p_dim
    conv_ch = state_conv.shape[2]
    groups = (conv_ch - ssm_w) // (2 * n_state)
    n_main = 3 * attn_w + ssm_w + conv_ch

    w_dt = jnp.pad(w_in[:, n_main:].astype(BF16), ((0, 0), (0, LANES - heads)))
    row = lambda v: v.reshape(1, -1).astype(F32)
    padl = lambda v: jnp.pad(v.astype(F32), (0, LANES - v.shape[0])).reshape(1, LANES)
    dskip_x = jnp.repeat(d_skip.astype(F32), p_dim).reshape(1, ssm_w)
    expand = (jnp.arange(LANES)[:, None] == (jnp.arange(ssm_w)[None, :] // p_dim)).astype(F32)
    in_args = (row(norm_mix_g), w_in, w_dt, row(q_norm_g), row(k_norm_g))
    in_kw = dict(attn_w=attn_w, ssm_w=ssm_w, conv_ch=conv_ch)
    ssd_kw = dict(ssm_w=ssm_w, groups=groups, n_state=n_state)
    ssd_par = (conv_w, row(conv_b), padl(dt_bias), padl(a_log), dskip_x)

    xp2 = x_prompt.reshape(b * s, d)
    q, k, v, z, xbc, dtr, w_gate_b, w_up_b, w_down_b = _inproj(
        xp2, *in_args, side=(w_gate, w_up, w_down), tm=_row_tile(b * s, 1024), **in_kw)
    moba_steps, ssd_steps = b * n_heads, b * (s // SSD_CHUNK)
    pg_moba, pg_ssd = _page_split(n_phys, moba_steps, s // MOBA_BLOCK, ssd_steps)
    ps_ssd0 = n_phys - ssd_steps * pg_ssd
    assert 0 <= ps_ssd0 <= moba_steps * pg_moba <= n_phys
    attn, ps_a = _moba_prompt(q.reshape(b, s, attn_w), k.reshape(b, s, attn_w), v.reshape(b, s, attn_w),
                              cache_k, n_heads=n_heads, pages_per_step=pg_moba)
    sn, conv_p, ssm_p, ps_b, w_out_b = _ssd_prompt(
        xbc.reshape(b, s, conv_ch), dtr.reshape(b, s, LANES), z.reshape(b, s, ssm_w), *ssd_par,
        row(ssm_out_g), expand, w_out, cache_k, page0=ps_ssd0, pages_per_step=pg_ssd, **ssd_kw)
    h_p = _outproj(attn.reshape(b * s, attn_w), sn.reshape(b * s, ssm_w), xp2, row(attn_out_g), w_out_b,
                   tm=_row_tile(b * s, 512))
    y_p = _ffn(h_p, row(norm_ffn_g), w_gate_b, w_up_b, w_down_b,
               tm=_row_tile(b * s, 1024)).reshape(b, s, d)
    page_shape = (b, s // page, page, n_heads, dh)
    k_p, v_p = k.reshape(page_shape), v.reshape(page_shape)

    ms = db * t_len
    xs2 = x_sample.reshape(ms, d)
    qs, k_s, v_s, zs, xbcs, dtrs = _inproj(xs2, *in_args, tm=_row_tile(ms, 1024), **in_kw)
    qs3, ks3, vs3 = (a.reshape(db, t_len, attn_w) for a in (qs, k_s, v_s))
    sel = _sample_select(page_table, ps_a, ps_b, qs3, n_heads=n_heads, page=page, split=ps_ssd0)
    idx = sel[:, :, :t_len, :MOBA_TOP_K].reshape(db, n_heads * t_len * MOBA_TOP_K)
    ok = sel[:, :, :t_len, MOBA_TOP_K:2 * MOBA_TOP_K].reshape(db, n_heads * t_len * MOBA_TOP_K)
    attn_s = _sample_attend(page_table, idx, ok, qs3, ks3, vs3, cache_k, cache_v, n_heads=n_heads)
    sns, conv_s, ssm_s = _ssd_sample(xbcs.reshape(db, t_len, conv_ch), state_conv,
                                     dtrs.reshape(db, t_len, LANES), zs.reshape(db, t_len, ssm_w), state_ssm,
                                     *ssd_par, row(ssm_out_g), expand, **ssd_kw)
    tms = _row_tile(ms, 512)
    h_s = _outproj(attn_s.reshape(ms, attn_w), sns.reshape(ms, ssm_w), xs2, row(attn_out_g), w_out_b, tm=tms)
    y_s = _ffn(h_s, row(norm_ffn_g), w_gate_b, w_up_b, w_down_b, tm=tms).reshape(db, t_len, d)
    tok_shape = (db, t_len, n_heads, dh)
    return (y_p, y_s, k_p, v_p, conv_p, ssm_p, k_s.reshape(tok_shape), v_s.reshape(tok_shape),
            conv_s, ssm_s)
```

```python
import functools

import jax
import jax.numpy as jnp
from jax import lax
from jax.experimental import pallas as pl
from jax.experimental.pallas import tpu as pltpu

F32 = jnp.float32
BF16 = jnp.bfloat16
EPS = 1e-6
NEG = -1e30

MOBA_BLOCK = 256
MOBA_TOP_K = 3
SSD_CHUNK = 128
SSM_HEAD_DIM = 64
LANES = 128
SUBLANES = 8
VMEM_LIMIT = 56 * 1024 * 1024
BIG_VMEM_LIMIT = 60 * 1024 * 1024

_NT = (((1,), (1,)), ((), ()))
_TN = (((0,), (0,)), ((), ()))


def _params(sem, vmem=VMEM_LIMIT):
    return pltpu.CompilerParams(dimension_semantics=sem, vmem_limit_bytes=vmem)


def _rms(x, g):
    return x * lax.rsqrt(jnp.mean(x * x, axis=-1, keepdims=True) + EPS) * g


def _silu(x):
    return x * (1.0 / (1.0 + jnp.exp(-x)))


def _softplus(x):
    return jnp.maximum(x, 0.0) + jnp.log1p(jnp.exp(-jnp.abs(x)))


def _split3(x):
    hi = x.astype(BF16)
    rest = x - hi.astype(F32)
    mid = rest.astype(BF16)
    return hi, mid, (rest - mid.astype(F32)).astype(BF16)


def _alibi_slope(h):
    bits = (jnp.full((1, 1), 126, jnp.int32) - h) << 23
    return lax.bitcast_convert_type(bits, F32)


def _inproj_kernel(*refs, tn, seg, n_side):
    x_ref, g_ref, w_ref, wdt_ref, qg_ref, kg_ref = refs[:6]
    q_ref, k_ref, v_ref, z_ref, xbc_ref, dt_ref = refs[6 + n_side:12 + n_side]
    xn_ref = refs[-1]
    for src, dst in zip(refs[6:6 + n_side], refs[12 + n_side:12 + 2 * n_side]):
        dst[...] = src[...].astype(BF16)
    j = pl.program_id(1)

    @pl.when(j == 0)
    def _():
        xn_ref[...] = _rms(x_ref[...], g_ref[...]).astype(BF16)
        dt_ref[...] = jnp.dot(xn_ref[...], wdt_ref[...], preferred_element_type=F32)

    acc = jnp.dot(xn_ref[...], w_ref[...].astype(BF16), preferred_element_type=F32)

    def headnorm(out_ref, g):
        for hh in range(tn // LANES):
            sl = slice(hh * LANES, (hh + 1) * LANES)
            out_ref[:, sl] = _rms(acc[:, sl], g)

    @pl.when(j < seg[0])
    def _():
        headnorm(q_ref, qg_ref[...])

    @pl.when((j >= seg[0]) & (j < seg[1]))
    def _():
        headnorm(k_ref, kg_ref[...])

    @pl.when((j >= seg[1]) & (j < seg[2]))
    def _():
        v_ref[...] = acc

    @pl.when((j >= seg[2]) & (j < seg[3]))
    def _():
        z_ref[...] = acc

    @pl.when(j >= seg[3])
    def _():
        xbc_ref[...] = acc


def _grid_tiling(shape, ni, nj):
    for (gr, gc), imap in (((ni, nj), lambda i, j: (i, j)), ((nj, ni), lambda i, j: (j, i))):
        if shape[0] % gr == 0 and shape[1] % gc == 0:
            br, bc = shape[0] // gr, shape[1] // gc
            if br % (2 * SUBLANES) == 0 and bc % LANES == 0:
                return pl.BlockSpec((br, bc), imap)
    return None


def _inproj(x2, norm_g, w_all, w_dt, q_g, k_g, side=(), *, attn_w, ssm_w, conv_ch, tm, tn=512):
    m, d = x2.shape
    n_main = 3 * attn_w + ssm_w + conv_ch
    assert n_main % tn == 0
    seg = (attn_w // tn, 2 * attn_w // tn, 3 * attn_w // tn, (3 * attn_w + ssm_w) // tn)
    nj = n_main // tn
    side_specs = [_grid_tiling(w.shape, m // tm, nj) for w in side]
    assert all(sp is not None for sp in side_specs)

    def col(lo, hi):
        return lambda i, j: (i, jnp.clip(j - lo, 0, hi - lo - 1))

    out_shape = (
        jax.ShapeDtypeStruct((m, attn_w), F32), jax.ShapeDtypeStruct((m, attn_w), F32),
        jax.ShapeDtypeStruct((m, attn_w), F32), jax.ShapeDtypeStruct((m, ssm_w), F32),
        jax.ShapeDtypeStruct((m, conv_ch), F32), jax.ShapeDtypeStruct((m, LANES), F32),
        *(jax.ShapeDtypeStruct(w.shape, BF16) for w in side))
    return pl.pallas_call(
        functools.partial(_inproj_kernel, tn=tn, seg=seg, n_side=len(side)),
        grid=(m // tm, nj),
        in_specs=[
            pl.BlockSpec((tm, d), lambda i, j: (i, 0)),
            pl.BlockSpec((1, d), lambda i, j: (0, 0)),
            pl.BlockSpec((d, tn), lambda i, j: (0, j)),
            pl.BlockSpec((d, LANES), lambda i, j: (0, 0)),
            pl.BlockSpec((1, LANES), lambda i, j: (0, 0)),
            pl.BlockSpec((1, LANES), lambda i, j: (0, 0)),
            *side_specs,
        ],
        out_specs=(
            pl.BlockSpec((tm, tn), col(0, seg[0])),
            pl.BlockSpec((tm, tn), col(seg[0], seg[1])),
            pl.BlockSpec((tm, tn), col(seg[1], seg[2])),
            pl.BlockSpec((tm, tn), col(seg[2], seg[3])),
            pl.BlockSpec((tm, tn), col(seg[3], nj)),
            pl.BlockSpec((tm, LANES), lambda i, j: (i, 0)),
            *side_specs,
        ),
        out_shape=out_shape,
        scratch_shapes=[pltpu.VMEM((tm, d), BF16)],
        compiler_params=_params(("parallel", "arbitrary")),
        name="inproj",
    )(x2, norm_g, w_all, w_dt, q_g, k_g, *side)


def _block_select(gate_t, n_past):
    row = lax.broadcasted_iota(jnp.int32, gate_t.shape, 0)
    valid = row < n_past
    if n_past <= MOBA_TOP_K:
        return valid & (gate_t > 0.5 * NEG)
    rank = jnp.zeros(gate_t.shape, jnp.int32)
    for j in range(n_past):
        other = gate_t[j:j + 1, :]
        rank = rank + ((other > gate_t) | ((other == gate_t) & (row > j))).astype(jnp.int32)
    return valid & (rank < MOBA_TOP_K) & (gate_t > 0.5 * NEG)


def _page_chunk_copy(ck_hbm, buf, sem, first_page, n_pages):
    return pltpu.make_async_copy(ck_hbm.at[pl.ds(first_page, n_pages)], buf, sem)


def _page_key_sums(pages, lanes_of_adds=4):
    n, page, heads, dh = pages.shape
    part = pages.reshape(n * lanes_of_adds, page // lanes_of_adds, heads, dh).sum(axis=1)
    return part.reshape(n, lanes_of_adds, heads, dh).sum(axis=1)


def _stream_page_sums(ck_hbm, ps_ref, pbuf, psem, step, n_steps, page0):
    n_pg = pbuf.shape[1]
    slot = step % 2

    def page_copy(st, sl):
        return _page_chunk_copy(ck_hbm, pbuf.at[sl], psem.at[sl], page0 + st * n_pg, n_pg)

    @pl.when(step == 0)
    def _():
        page_copy(0, 0).start()

    page_copy(step, slot).wait()

    @pl.when(step + 1 < n_steps)
    def _():
        page_copy(step + 1, 1 - slot).start()

    ps_ref[...] = _page_key_sums(pbuf[slot])


def _causal_chunks(total, nb):
    weights = [i + 2 for i in range(nb - 1)] + [1]
    cum, acc = [0], 0
    for w in weights:
        acc += w
        cum.append((total * acc + sum(weights) // 2) // sum(weights))
    return [hi - lo for lo, hi in zip(cum[:-1], cum[1:])], cum[:-1]


def _moba_prompt_kernel(q_ref, k_ref, v_ref, ck_hbm, o_ref, ps_ref, pbuf, psem, *, nb, scale, pages):
    blk = MOBA_BLOCK
    h = pl.program_id(1)
    step = pl.program_id(0) * pl.num_programs(1) + h
    n_steps = pl.num_programs(0) * pl.num_programs(1)
    sizes, offs = _causal_chunks(pages, nb)
    chunks = [i for i in range(nb) if sizes[i] > 0]
    n_ch = len(chunks)
    n_slots = pbuf.shape[0]
    ahead = n_slots - 1
    assert n_ch >= ahead

    def page_copy(c):
        st, cc = divmod(c, n_ch)
        i = chunks[cc]
        slot = (step * n_ch + c) % n_slots
        return _page_chunk_copy(ck_hbm, pbuf.at[slot, pl.ds(0, sizes[i])], psem.at[slot],
                                (step + st) * pages + offs[i], sizes[i])

    @pl.when(step == 0)
    def _():
        for c in range(ahead):
            page_copy(c).start()

    slope = _alibi_slope(h)
    k = k_ref[...]
    kb = k.astype(BF16)
    vb = v_ref[...].astype(BF16)
    nb_pad = -(-nb // SUBLANES) * SUBLANES
    means = [jnp.mean(k[j * blk:(j + 1) * blk], axis=0, keepdims=True) for j in range(nb)]
    means = jnp.concatenate(means + [jnp.zeros((nb_pad - nb, LANES), F32)] * (nb_pad > nb), axis=0)
    for i in range(nb):
        qi = q_ref[i * blk:(i + 1) * blk, :]
        nk = (i + 1) * blk
        s = lax.dot_general(qi.astype(BF16), kb[:nk], _NT, preferred_element_type=F32)
        col = lax.broadcasted_iota(jnp.int32, (blk, nk), 1)
        logits = s * scale + slope * col.astype(F32)
        row = lax.broadcasted_iota(jnp.int32, (blk, blk), 0)
        colb = lax.broadcasted_iota(jnp.int32, (blk, blk), 1)
        pieces = []
        if i > 0:
            gate_t = lax.dot_general(means, qi, _NT, precision=lax.Precision.HIGHEST,
                                     preferred_element_type=F32)
            sel_t = _block_select(gate_t, i).astype(F32)
            sel_t = jnp.concatenate([sel_t, jnp.zeros((LANES - nb_pad, blk), F32)], axis=0)
            sel = jnp.concatenate([sel_t[:, r * LANES:(r + 1) * LANES].T
                                   for r in range(blk // LANES)], axis=0)
            pieces = [jnp.broadcast_to(sel[:, j:j + 1], (blk, blk)) for j in range(i)]
        pieces.append((colb <= row).astype(F32))
        allowed = jnp.concatenate(pieces, axis=1) if len(pieces) > 1 else pieces[0]
        logits = jnp.where(allowed > 0.5, logits, NEG)
        m = jnp.max(logits, axis=-1, keepdims=True)
        p = jnp.exp(logits - m)
        denom = jnp.sum(p, axis=-1, keepdims=True)
        out = jnp.dot(p.astype(BF16), vb[:nk], preferred_element_type=F32)
        o_ref[i * blk:(i + 1) * blk, :] = out / denom
        if i in chunks:
            c = chunks.index(i)
            page_copy(c).wait()
            if c + ahead < n_ch:
                page_copy(c + ahead).start()
            else:
                @pl.when(step + 1 < n_steps)
                def _():
                    page_copy(c + ahead).start()
            slot = (step * n_ch + c) % n_slots
            ps_ref[offs[i]:offs[i] + sizes[i]] = _page_key_sums(pbuf[slot, 0:sizes[i]])


def _moba_prompt(q, k, v, cache_k, *, n_heads, pages_per_step):
    b, s, w = q.shape
    dh = w // n_heads
    nb = s // MOBA_BLOCK
    npc = max(_causal_chunks(pages_per_step, nb)[0])
    page, ch, cdh = cache_k.shape[1:]
    spec = pl.BlockSpec((None, s, dh), lambda bi, hi: (bi, 0, hi))
    return pl.pallas_call(
        functools.partial(_moba_prompt_kernel, nb=nb, scale=dh ** -0.5, pages=pages_per_step),
        grid=(b, n_heads),
        in_specs=[spec, spec, spec, pl.BlockSpec(memory_space=pl.ANY)],
        out_specs=(spec, pl.BlockSpec((pages_per_step, ch, cdh), lambda bi, hi: (bi * n_heads + hi, 0, 0))),
        out_shape=(jax.ShapeDtypeStruct((b, s, w), F32),
                   jax.ShapeDtypeStruct((b * n_heads * pages_per_step, ch, cdh), F32)),
        scratch_shapes=[pltpu.VMEM((3, npc, page, ch, cdh), F32), pltpu.SemaphoreType.DMA((3,))],
        compiler_params=_params(("arbitrary", "arbitrary")),
        name="moba_prompt",
    )(q, k, v, cache_k)


def _gated_norm(y, z, g):
    return _rms(y * _silu(z), g).astype(BF16)


def _ssd_prompt_kernel(xbc_ref, dtr_ref, z_ref, cw_ref, cb_ref, dtb_ref, alog_ref, dskip_ref, sg_ref,
                       expand_ref, wof_ref, ck_hbm, s_ref, convp_ref, h_ref, ps_ref, wob_ref,
                       xp_ref, pbuf, psem, *, cw, ssm_w, groups, n_state, page0):
    wob_ref[...] = wof_ref[...].astype(BF16)
    c = pl.program_id(1)
    nc = pl.num_programs(1)
    _stream_page_sums(ck_hbm, ps_ref, pbuf, psem, pl.program_id(0) * nc + c, pl.num_programs(0) * nc,
                      page0)
    lc = SSD_CHUNK
    p_dim = SSM_HEAD_DIM
    heads = ssm_w // p_dim
    hpg = heads // groups
    gw = hpg * p_dim
    tail = cw - 1

    @pl.when(c == 0)
    def _():
        xp_ref[0:SUBLANES, :] = jnp.zeros((SUBLANES, xp_ref.shape[1]), F32)
        h_ref[...] = jnp.zeros(h_ref.shape, F32)

    xc = xbc_ref[...]
    xp_ref[SUBLANES:SUBLANES + lc, :] = xc
    conv = cb_ref[...]
    for i in range(cw):
        conv = conv + xp_ref[SUBLANES - tail + i:SUBLANES - tail + i + lc, :] * cw_ref[i:i + 1, :]
    act = _silu(conv)
    xp_ref[SUBLANES - tail:SUBLANES, :] = xc[lc - tail:lc, :]

    @pl.when(c == nc - 1)
    def _():
        convp_ref[...] = xc[lc - tail:lc, :]

    dt = _softplus(dtr_ref[...] + dtb_ref[...])
    a = -jnp.exp(alog_ref[...])
    row = lax.broadcasted_iota(jnp.int32, (lc, lc), 0)
    colm = lax.broadcasted_iota(jnp.int32, (lc, lc), 1)
    causal = row >= colm
    tri = causal.astype(BF16)
    cum = sum(jnp.dot(tri, t, preferred_element_type=F32) for t in _split3(dt * a))
    cum_t = cum.T
    dt_t = dt.T
    cum_last = cum[lc - 1:lc, :]
    dec_end = jnp.exp(cum_last - cum) * dt
    ecum = jnp.exp(cum)
    expand = expand_ref[...].astype(BF16)
    ecum_x = sum(jnp.dot(t, expand, preferred_element_type=F32) for t in _split3(ecum))
    dec_end_x = sum(jnp.dot(t, expand, preferred_element_type=F32) for t in _split3(dec_end))

    xs = act[:, :ssm_w]
    xs_b = xs.astype(BF16)
    y = dskip_ref[...] * xs
    lane = lax.broadcasted_iota(jnp.int32, (lc, 2 * p_dim), 1)
    for g in range(groups):
        bg = act[:, ssm_w + g * n_state:ssm_w + (g + 1) * n_state].astype(BF16)
        cg = act[:, ssm_w + (groups + g) * n_state:ssm_w + (groups + g + 1) * n_state].astype(BF16)
        cb = lax.dot_general(cg, bg, _NT, preferred_element_type=F32)
        hg = h_ref[g * hpg:(g + 1) * hpg].reshape(gw, n_state)
        y_off = lax.dot_general(cg, hg.astype(BF16), _NT, preferred_element_type=F32)
        y_off = y_off * ecum_x[:, g * gw:(g + 1) * gw]
        y_diag = []
        for pr in range(hpg // 2):
            ws = []
            for e in (g * hpg + 2 * pr, g * hpg + 2 * pr + 1):
                diff = cum[:, e:e + 1] - cum_t[e:e + 1, :]
                decay = jnp.where(causal, jnp.exp(diff), 0.0)
                ws.append((cb * decay * dt_t[e:e + 1, :]).astype(BF16))
            lo = (g * hpg + 2 * pr) * p_dim
            xpair = xs_b[:, lo:lo + 2 * p_dim]
            zero = jnp.zeros_like(xpair)
            rhs = jnp.concatenate([jnp.where(lane < p_dim, xpair, zero),
                                   jnp.where(lane >= p_dim, xpair, zero)], axis=0)
            y_diag.append(jnp.dot(jnp.concatenate(ws, axis=1), rhs, preferred_element_type=F32))
        y_g = jnp.concatenate(y_diag, axis=1) + y_off
        xw = (xs[:, g * gw:(g + 1) * gw] * dec_end_x[:, g * gw:(g + 1) * gw]).astype(BF16)
        st = lax.dot_general(xw, bg, _TN, preferred_element_type=F32)
        for ee in range(hpg):
            e = g * hpg + ee
            cd = jnp.exp(cum_t[e:e + 1, lc - 1:lc])
            h_ref[e] = h_ref[e] * cd + st[ee * p_dim:(ee + 1) * p_dim, :]
        gl = slice(g * gw, (g + 1) * gw)
        s_ref[:, gl] = _gated_norm(y[:, gl] + y_g, z_ref[:, gl], sg_ref[:, gl])


def _ssd_prompt(xbc, dtr, z, conv_w, conv_b, dtb, alog, dskip_x, ssm_g, expand, w_side, cache_k,
                *, ssm_w, groups, n_state, page0, pages_per_step):
    b, s, cc = xbc.shape
    cw = conv_w.shape[0]
    heads = ssm_w // SSM_HEAD_DIM
    lc = SSD_CHUNK
    nc = s // lc
    page, ch, cdh = cache_k.shape[1:]
    const = lambda shape: pl.BlockSpec(shape, lambda bi, ci: (0,) * len(shape))
    side_spec = _grid_tiling(w_side.shape, b, nc)
    assert side_spec is not None
    return pl.pallas_call(
        functools.partial(_ssd_prompt_kernel, cw=cw, ssm_w=ssm_w, groups=groups, n_state=n_state,
                          page0=page0),
        grid=(b, nc),
        in_specs=[
            pl.BlockSpec((None, lc, cc), lambda bi, ci: (bi, ci, 0)),
            pl.BlockSpec((None, lc, LANES), lambda bi, ci: (bi, ci, 0)),
            pl.BlockSpec((None, lc, ssm_w), lambda bi, ci: (bi, ci, 0)),
            const((cw, cc)), const((1, cc)), const((1, LANES)), const((1, LANES)),
            const((1, ssm_w)), const((1, ssm_w)), const((LANES, ssm_w)),
            side_spec,
            pl.BlockSpec(memory_space=pl.ANY),
        ],
        out_specs=(
            pl.BlockSpec((None, lc, ssm_w), lambda bi, ci: (bi, ci, 0)),
            pl.BlockSpec((None, cw - 1, cc), lambda bi, ci: (bi, 0, 0)),
            pl.BlockSpec((None, heads, SSM_HEAD_DIM, n_state), lambda bi, ci: (bi, 0, 0, 0)),
            pl.BlockSpec((pages_per_step, ch, cdh), lambda bi, ci: (bi * nc + ci, 0, 0)),
            side_spec,
        ),
        out_shape=(
            jax.ShapeDtypeStruct((b, s, ssm_w), BF16),
            jax.ShapeDtypeStruct((b, cw - 1, cc), F32),
            jax.ShapeDtypeStruct((b, heads, SSM_HEAD_DIM, n_state), F32),
            jax.ShapeDtypeStruct((b * nc * pages_per_step, ch, cdh), F32),
            jax.ShapeDtypeStruct(w_side.shape, BF16),
        ),
        scratch_shapes=[pltpu.VMEM((SUBLANES + lc, cc), F32),
                        pltpu.VMEM((2, pages_per_step, page, ch, cdh), F32),
                        pltpu.SemaphoreType.DMA((2,))],
        compiler_params=_params(("arbitrary", "arbitrary")),
        name="ssd_prompt",
    )(xbc, dtr, z, conv_w, conv_b, dtb, alog, dskip_x, ssm_g, expand, w_side, cache_k)


def _outproj_kernel(attn_ref, s_ref, x_ref, ag_ref, w_ref, o_ref, *, rc):
    for r in range(o_ref.shape[0] // rc):
        rows = slice(r * rc, (r + 1) * rc)
        a = _rms(attn_ref[rows, :], ag_ref[...]).astype(BF16)
        mix = jnp.concatenate([a, s_ref[rows, :]], axis=1)
        o_ref[rows, :] = x_ref[rows, :] + jnp.dot(mix, w_ref[...], preferred_element_type=F32)


def _outproj(attn, s_norm, x2, attn_g, w_out, *, tm, rc=256):
    m, attn_w = attn.shape
    ssm_w = s_norm.shape[1]
    d = x2.shape[1]
    return pl.pallas_call(
        functools.partial(_outproj_kernel, rc=min(rc, tm)),
        grid=(m // tm,),
        in_specs=[
            pl.BlockSpec((tm, attn_w), lambda i: (i, 0)),
            pl.BlockSpec((tm, ssm_w), lambda i: (i, 0)),
            pl.BlockSpec((tm, d), lambda i: (i, 0)),
            pl.BlockSpec((1, attn_w), lambda i: (0, 0)),
            pl.BlockSpec((attn_w + ssm_w, d), lambda i: (0, 0)),
        ],
        out_specs=pl.BlockSpec((tm, d), lambda i: (i, 0)),
        out_shape=jax.ShapeDtypeStruct((m, d), F32),
        compiler_params=_params(("parallel",)),
        name="outproj",
    )(attn, s_norm, x2, attn_g, w_out)


def _ffn_kernel(h_ref, g_ref, wg_ref, wu_ref, wd_ref, o_ref, hn_ref, *, rc):
    f = pl.program_id(1)

    @pl.when(f == 0)
    def _():
        hn_ref[...] = _rms(h_ref[...], g_ref[...]).astype(BF16)
        o_ref[...] = h_ref[...]

    for r in range(o_ref.shape[0] // rc):
        rows = slice(r * rc, (r + 1) * rc)
        hn = hn_ref[rows, :]
        gate = jnp.dot(hn, wg_ref[...], preferred_element_type=F32)
        up = jnp.dot(hn, wu_ref[...], preferred_element_type=F32)
        o_ref[rows, :] += jnp.dot((_silu(gate) * up).astype(BF16), wd_ref[...],
                                  preferred_element_type=F32)


def _ffn(h, g, w_gate, w_up, w_down, *, tm, tf=512, rc=512):
    m, d = h.shape
    dff = w_gate.shape[1]
    return pl.pallas_call(
        functools.partial(_ffn_kernel, rc=min(rc, tm)),
        grid=(m // tm, dff // tf),
        in_specs=[
            pl.BlockSpec((tm, d), lambda i, f: (i, 0)),
            pl.BlockSpec((1, d), lambda i, f: (0, 0)),
            pl.BlockSpec((d, tf), lambda i, f: (0, f)),
            pl.BlockSpec((d, tf), lambda i, f: (0, f)),
            pl.BlockSpec((tf, d), lambda i, f: (f, 0)),
        ],
        out_specs=pl.BlockSpec((tm, d), lambda i, f: (i, 0)),
        out_shape=jax.ShapeDtypeStruct((m, d), F32),
        scratch_shapes=[pltpu.VMEM((tm, d), BF16)],
        compiler_params=_params(("parallel", "arbitrary"), vmem=BIG_VMEM_LIMIT),
        name="ffn",
    )(h, g, w_gate, w_up, w_down)


def _sample_select_kernel(pt_ref, psa_ref, psb_ref, q_ref, sel_ref, blk_ref,
                          *, nb, ppb, n_heads, t_len, split):
    b = pl.program_id(0)

    def page_sum(pg):
        lo = psa_ref[jnp.minimum(pg, split - 1)]
        hi = psb_ref[jnp.maximum(pg - split, 0)]
        return jnp.where(pg < split, lo, hi)

    for n in range(nb):
        acc = page_sum(pt_ref[b, n * ppb])
        for pp in range(1, ppb):
            acc = acc + page_sum(pt_ref[b, n * ppb + pp])
        blk_ref[n] = acc / MOBA_BLOCK
    lane = lax.broadcasted_iota(jnp.int32, (SUBLANES, LANES), 1)
    for h in range(n_heads):
        qh = q_ref[:, h * LANES:(h + 1) * LANES]
        q8 = jnp.concatenate([qh, jnp.zeros((SUBLANES - t_len, LANES), F32)], axis=0)
        bm = blk_ref[:, h, :]
        gate = lax.dot_general(q8, bm, _NT, precision=lax.Precision.HIGHEST,
                               preferred_element_type=F32)
        lane_n = lax.broadcasted_iota(jnp.int32, gate.shape, 1).astype(F32)
        out = jnp.zeros((SUBLANES, LANES), jnp.int32)
        for kk in range(MOBA_TOP_K):
            mx = jnp.max(gate, axis=-1, keepdims=True)
            idx_f = jnp.min(jnp.where(gate == mx, lane_n, float(nb)), axis=-1, keepdims=True)
            idx = idx_f.astype(jnp.int32)
            ok = (mx > 0.5 * NEG).astype(jnp.int32)
            out = jnp.where(lane == kk, idx, out)
            out = jnp.where(lane == MOBA_TOP_K + kk, ok, out)
            gate = jnp.where(lane_n == idx_f, -jnp.inf, gate)
        sel_ref[h] = out


def _sample_select(page_table, ps_lo, ps_hi, q_s, *, n_heads, page, split):
    db, t_len, w = q_s.shape
    n_pages = page_table.shape[1]
    _, h, dh = ps_lo.shape
    ppb = MOBA_BLOCK // page
    nb = n_pages // ppb
    whole = lambda a: pl.BlockSpec(a.shape, lambda bi, pt: (0, 0, 0), pipeline_mode=pl.Buffered(1))
    grid_spec = pltpu.PrefetchScalarGridSpec(
        num_scalar_prefetch=1,
        grid=(db,),
        in_specs=[whole(ps_lo), whole(ps_hi),
                  pl.BlockSpec((None, t_len, w), lambda bi, pt: (bi, 0, 0))],
        out_specs=pl.BlockSpec((None, n_heads, SUBLANES, LANES), lambda bi, pt: (bi, 0, 0, 0)),
        scratch_shapes=[pltpu.VMEM((nb, h, dh), F32)],
    )
    return pl.pallas_call(
        functools.partial(_sample_select_kernel, nb=nb, ppb=ppb, n_heads=n_heads, t_len=t_len,
                          split=split),
        grid_spec=grid_spec,
        out_shape=jax.ShapeDtypeStruct((db, n_heads, SUBLANES, LANES), jnp.int32),
        compiler_params=_params(("arbitrary",)),
        name="sample_select",
    )(page_table, ps_lo, ps_hi, q_s)


def _sample_attend_kernel(pt_ref, idx_ref, ok_ref, q_ref, kn_ref, vn_ref, ck_hbm, cv_hbm, o_ref,
                          kbuf, vbuf, sem, *, t_len, page, ppb, past_len, scale, hps):
    b = pl.program_id(0)
    hg = pl.program_id(1)
    n_hg = pl.num_programs(1)
    step = b * n_hg + hg
    cur = step % 2
    n_sel = t_len * MOBA_TOP_K * ppb
    n_pages = past_len // page

    def copies(st, hl, hh, slot, phys):
        return (pltpu.make_async_copy(ck_hbm.at[phys, :, hh, :], kbuf.at[st, hl, slot], sem.at[st, 0]),
                pltpu.make_async_copy(cv_hbm.at[phys, :, hh, :], vbuf.at[st, hl, slot], sem.at[st, 1]))

    def slot_pages(bb, hh):
        out = []
        for t in range(t_len):
            for kk in range(MOBA_TOP_K):
                n = idx_ref[bb, (hh * t_len + t) * MOBA_TOP_K + kk]
                for pp in range(ppb):
                    lp = jnp.clip(n * ppb + pp, 0, n_pages - 1)
                    out.append(((t * MOBA_TOP_K + kk) * ppb + pp, pt_ref[bb, lp], n * ppb + pp))
        out.append((n_sel, pt_ref[bb, n_pages - 1], None))
        return out

    def start_all(st, bb, hgg):
        for hl in range(hps):
            hh = hgg * hps + hl
            for slot, phys, _ in slot_pages(bb, hh):
                for cp in copies(st, hl, hh, slot, phys):
                    cp.start()

    @pl.when(step == 0)
    def _():
        start_all(0, b, hg)

    wrap = hg + 1 == n_hg
    b_next = b + wrap.astype(jnp.int32)
    hg_next = jnp.where(wrap, 0, hg + 1)

    @pl.when(step + 1 < pl.num_programs(0) * n_hg)
    def _():
        start_all(1 - cur, b_next, hg_next)

    n_keys = (n_sel + 2) * page
    lane = lax.broadcasted_iota(jnp.int32, (1, page), 1)
    trow = lax.broadcasted_iota(jnp.int32, (SUBLANES, n_keys), 0)
    kcol = lax.broadcasted_iota(jnp.int32, (SUBLANES, n_keys), 1)
    q_pos = past_len + trow
    blk_start = (q_pos // MOBA_BLOCK) * MOBA_BLOCK
    per_t = MOBA_TOP_K * ppb * page
    sel_cols = n_sel * page
    pad = jnp.zeros((page - t_len, LANES), F32)

    head_pages = [slot_pages(b, hg * hps + hl) for hl in range(hps)]
    for hl in range(hps):
        hcols = slice(hl * LANES, (hl + 1) * LANES)
        kbuf[cur, hl, n_sel + 1] = jnp.concatenate([kn_ref[:, hcols], pad], axis=0)
        vbuf[cur, hl, n_sel + 1] = jnp.concatenate([vn_ref[:, hcols], pad], axis=0)

    for hl in range(hps):
        for slot, phys, _ in head_pages[hl]:
            for cp in copies(cur, hl, hg * hps + hl, slot, phys):
                cp.wait()

    for hl in range(hps):
        h = hg * hps + hl
        hcols = slice(hl * LANES, (hl + 1) * LANES)
        pos, okv = [], []
        for slot, _, lp in head_pages[hl][:-1]:
            n_ok = ok_ref[b, (h * t_len + slot // (MOBA_TOP_K * ppb)) * MOBA_TOP_K
                          + (slot // ppb) % MOBA_TOP_K]
            pos.append(lane + jnp.clip(lp, 0, n_pages - 1) * page)
            okv.append(jnp.broadcast_to(n_ok, (1, page)))
        pos.append(lane + (past_len - page))
        pos.append(lane + past_len)
        pos = jnp.concatenate(pos, axis=1)
        okv = jnp.concatenate(okv + [jnp.zeros((1, n_keys - sel_cols), jnp.int32)], axis=1)

        kall = kbuf[cur, hl].reshape(n_keys, LANES).astype(BF16)
        vall = vbuf[cur, hl].reshape(n_keys, LANES).astype(BF16)
        q8 = jnp.concatenate([q_ref[:, hcols], jnp.zeros((SUBLANES - t_len, LANES), F32)], axis=0)
        s = lax.dot_general(q8.astype(BF16), kall, _NT, preferred_element_type=F32)

        dist = (q_pos - pos).astype(F32)
        logits = s * scale - _alibi_slope(h) * dist
        sel_ok = (kcol // per_t == trow) & (okv > 0)
        own_ok = ((kcol >= sel_cols) & (pos <= q_pos) & (pos >= blk_start)
                  & (kcol < sel_cols + page + t_len))
        logits = jnp.where(sel_ok | own_ok, logits, NEG)
        m = jnp.max(logits, axis=-1, keepdims=True)
        p = jnp.exp(logits - m)
        denom = jnp.sum(p, axis=-1, keepdims=True)
        out = jnp.dot(p.astype(BF16), vall, preferred_element_type=F32) / denom
        o_ref[:, hcols] = out[:t_len, :]


def _sample_attend(page_table, idx, ok, q_s, k_s, v_s, cache_k, cache_v, *, n_heads, hps=4):
    db, t_len, w = q_s.shape
    page = cache_k.shape[1]
    n_pages = page_table.shape[1]
    ppb = MOBA_BLOCK // page
    n_slots = t_len * MOBA_TOP_K * ppb + 2
    assert n_heads % hps == 0
    tok = pl.BlockSpec((None, t_len, hps * LANES), lambda bi, hi, *_: (bi, 0, hi))
    grid_spec = pltpu.PrefetchScalarGridSpec(
        num_scalar_prefetch=3,
        grid=(db, n_heads // hps),
        in_specs=[tok, tok, tok,
                  pl.BlockSpec(memory_space=pl.ANY), pl.BlockSpec(memory_space=pl.ANY)],
        out_specs=tok,
        scratch_shapes=[pltpu.VMEM((2, hps, n_slots, page, LANES), F32),
                        pltpu.VMEM((2, hps, n_slots, page, LANES), F32),
                        pltpu.SemaphoreType.DMA((2, 2))],
    )
    return pl.pallas_call(
        functools.partial(_sample_attend_kernel, t_len=t_len, page=page, ppb=ppb,
                          past_len=n_pages * page, scale=LANES ** -0.5, hps=hps),
        grid_spec=grid_spec,
        out_shape=jax.ShapeDtypeStruct((db, t_len, w), F32),
        compiler_params=_params(("arbitrary", "arbitrary")),
        name="sample_attend",
    )(page_table, idx, ok, q_s, k_s, v_s, cache_k, cache_v)


def _ssd_sample_kernel(xbc_ref, sc_ref, dtr_ref, z_ref, h0_ref, cw_ref, cb_ref, dtb_ref, alog_ref,
                       dskip_ref, sg_ref, expand_ref, s_ref, convs_ref, h_ref, xp_ref,
                       *, cw, t_len, ssm_w, groups, n_state):
    p_dim = SSM_HEAD_DIM
    heads = ssm_w // p_dim
    hpg = heads // groups
    gw = hpg * p_dim
    tail = cw - 1
    xp_ref[0:tail, :] = sc_ref[...]
    xp_ref[tail:tail + t_len, :] = xbc_ref[...]
    conv = cb_ref[...]
    for i in range(cw):
        conv = conv + xp_ref[i:i + t_len, :] * cw_ref[i:i + 1, :]
    act = _silu(conv)
    convs_ref[...] = xp_ref[t_len:t_len + tail, :]

    dt = _softplus(dtr_ref[...] + dtb_ref[...])
    da = dt * (-jnp.exp(alog_ref[...]))
    cum = [da[0:1]]
    for t in range(1, t_len):
        cum.append(cum[-1] + da[t:t + 1])
    lane = lax.broadcasted_iota(jnp.int32, (1, LANES), 1)
    xs = act[:, :ssm_w]
    zpad = jnp.zeros((SUBLANES - t_len, n_state), F32)

    b8, c8, cbs = [], [], []
    for g in range(groups):
        bg = act[:, ssm_w + g * n_state:ssm_w + (g + 1) * n_state]
        cg = act[:, ssm_w + (groups + g) * n_state:ssm_w + (groups + g + 1) * n_state]
        b8.append(jnp.concatenate([bg, zpad], axis=0).astype(BF16))
        c8.append(jnp.concatenate([cg, zpad], axis=0).astype(BF16))
        cbs.append(lax.dot_general(c8[g], b8[g], _NT, preferred_element_type=F32))

    pairs = [(l, s) for l in range(t_len) for s in range(l + 1)]
    rows = []
    for l, s in pairs:
        cb_ls = jnp.zeros((1, LANES), F32)
        for g in range(groups):
            in_g = (lane >= g * hpg) & (lane < (g + 1) * hpg)
            cb_ls = jnp.where(in_g, cbs[g][l:l + 1, s:s + 1], cb_ls)
        rows.append(cb_ls * jnp.exp(cum[l] - cum[s]) * dt[s:s + 1])
    rows += [jnp.exp(cum[l]) for l in range(t_len)]
    rows += [jnp.exp(cum[-1] - cum[s]) * dt[s:s + 1] for s in range(t_len)]
    n_rows = -(-len(rows) // SUBLANES) * SUBLANES
    rows.append(jnp.zeros((n_rows - len(rows), LANES), F32))
    rx = jnp.dot(jnp.concatenate(rows, axis=0), expand_ref[...], precision=lax.Precision.HIGHEST,
                 preferred_element_type=F32)
    r_ecum = len(pairs)
    r_coef = r_ecum + t_len

    y_diag = []
    for l in range(t_len):
        acc = jnp.zeros((1, ssm_w), F32)
        for s in range(l + 1):
            i = pairs.index((l, s))
            acc = acc + rx[i:i + 1, :] * xs[s:s + 1, :]
        y_diag.append(acc)
    y = dskip_ref[...] * xs + jnp.concatenate(y_diag, axis=0)

    y_off = []
    for g in range(groups):
        hg = h0_ref[g * hpg:(g + 1) * hpg].reshape(gw, n_state)
        y_off.append(lax.dot_general(c8[g], hg.astype(BF16), _NT, preferred_element_type=F32)[:t_len])
    y = y + jnp.concatenate(y_off, axis=1) * rx[r_ecum:r_ecum + t_len, :]
    for g in range(groups):
        gl = slice(g * gw, (g + 1) * gw)
        s_ref[:, gl] = _gated_norm(y[:, gl], z_ref[:, gl], sg_ref[:, gl])

    e_last = jnp.exp(cum[-1])
    xw = xs * rx[r_coef:r_coef + t_len, :]
    for g in range(groups):
        xw8 = jnp.concatenate([xw[:, g * gw:(g + 1) * gw], jnp.zeros((SUBLANES - t_len, gw), F32)], axis=0)
        st = lax.dot_general(xw8.astype(BF16), b8[g], _TN, preferred_element_type=F32)
        for ee in range(hpg):
            e = g * hpg + ee
            h_ref[e] = h0_ref[e] * e_last[:, e:e + 1] + st[ee * p_dim:(ee + 1) * p_dim, :]


def _ssd_sample(xbc, state_conv, dtr, z, state_ssm, conv_w, conv_b, dtb, alog, dskip_x, ssm_g, expand,
                *, ssm_w, groups, n_state):
    db, t_len, cc = xbc.shape
    cw = conv_w.shape[0]
    heads = ssm_w // SSM_HEAD_DIM
    const = lambda shape: pl.BlockSpec(shape, lambda bi: (0,) * len(shape))
    return pl.pallas_call(
        functools.partial(_ssd_sample_kernel, cw=cw, t_len=t_len, ssm_w=ssm_w, groups=groups,
                          n_state=n_state),
        grid=(db,),
        in_specs=[
            pl.BlockSpec((None, t_len, cc), lambda bi: (bi, 0, 0)),
            pl.BlockSpec((None, cw - 1, cc), lambda bi: (bi, 0, 0)),
            pl.BlockSpec((None, t_len, LANES), lambda bi: (bi, 0, 0)),
            pl.BlockSpec((None, t_len, ssm_w), lambda bi: (bi, 0, 0)),
            pl.BlockSpec((None, heads, SSM_HEAD_DIM, n_state), lambda bi: (bi, 0, 0, 0)),
            const((cw, cc)), const((1, cc)), const((1, LANES)), const((1, LANES)), const((1, ssm_w)),
            const((1, ssm_w)), const((LANES, ssm_w)),
        ],
        out_specs=(
            pl.BlockSpec((None, t_len, ssm_w), lambda bi: (bi, 0, 0)),
            pl.BlockSpec((None, cw - 1, cc), lambda bi: (bi, 0, 0)),
            pl.BlockSpec((None, heads, SSM_HEAD_DIM, n_state), lambda bi: (bi, 0, 0, 0)),
        ),
        out_shape=(
            jax.ShapeDtypeStruct((db, t_len, ssm_w), BF16),
            jax.ShapeDtypeStruct((db, cw - 1, cc), F32),
            jax.ShapeDtypeStruct((db, heads, SSM_HEAD_DIM, n_state), F32),
        ),
        scratch_shapes=[pltpu.VMEM((2 * SUBLANES, cc), F32)],
        compiler_params=_params(("parallel",)),
        name="ssd_sample",
    )(xbc, state_conv, dtr, z, state_ssm, conv_w, conv_b, dtb, alog, dskip_x, ssm_g, expand)


def _row_tile(m, target):
    tm = min(m, target)
    assert m % tm == 0, (m, tm)
    return tm


def _page_split(n_phys, moba_steps, nb, ssd_steps):
    per_moba = max(1, n_phys * 7 // 10 // moba_steps)
    rest = max(n_phys - moba_steps * per_moba, 0)
    per_ssd = max(1, -(-rest // ssd_steps))
    return per_moba, per_ssd


def kernel(x_prompt, x_sample, cache_k, cache_v, page_table, state_conv, state_ssm, norm_mix_g, w_in, q_norm_g, k_norm_g, conv_w, conv_b, dt_bias, a_log, d_skip, attn_out_g, ssm_out_g, w_out, norm_ffn_g, w_gate, w_up, w_down):
    b, s, d = x_prompt.shape
    db, t_len, _ = x_sample.shape
    n_phys, page, n_heads, dh = cache_k.shape
    attn_w = n_heads * dh
    heads, p_dim, n_state = state_ssm.shape[1:]
    assert p_dim == SSM_HEAD_DIM and dh == LANES and heads <= LANES
    ssm_w = heads * p_dim
    conv_ch = state_conv.shape[2]
    groups = (conv_ch - ssm_w) // (2 * n_state)
    n_main = 3 * attn_w + ssm_w + conv_ch

    w_dt = jnp.pad(w_in[:, n_main:].astype(BF16), ((0, 0), (0, LANES - heads)))
    row = lambda v: v.reshape(1, -1).astype(F32)
    padl = lambda v: jnp.pad(v.astype(F32), (0, LANES - v.shape[0])).reshape(1, LANES)
    dskip_x = jnp.repeat(d_skip.astype(F32), p_dim).reshape(1, ssm_w)
    expand = (jnp.arange(LANES)[:, None] == (jnp.arange(ssm_w)[None, :] // p_dim)).astype(F32)
    in_args = (row(norm_mix_g), w_in, w_dt, row(q_norm_g), row(k_norm_g))
    in_kw = dict(attn_w=attn_w, ssm_w=ssm_w, conv_ch=conv_ch)
    ssd_kw = dict(ssm_w=ssm_w, groups=groups, n_state=n_state)
    ssd_par = (conv_w, row(conv_b), padl(dt_bias), padl(a_log), dskip_x)

    xp2 = x_prompt.reshape(b * s, d)
    q, k, v, z, xbc, dtr, w_gate_b, w_up_b, w_down_b = _inproj(
        xp2, *in_args, side=(w_gate, w_up, w_down), tm=_row_tile(b * s, 1024), **in_kw)
    moba_steps, ssd_steps = b * n_heads, b * (s // SSD_CHUNK)
    pg_moba, pg_ssd = _page_split(n_phys, moba_steps, s // MOBA_BLOCK, ssd_steps)
    ps_ssd0 = n_phys - ssd_steps * pg_ssd
    assert 0 <= ps_ssd0 <= moba_steps * pg_moba <= n_phys
    attn, ps_a = _moba_prompt(q.reshape(b, s, attn_w), k.reshape(b, s, attn_w), v.reshape(b, s, attn_w),
                              cache_k, n_heads=n_heads, pages_per_step=pg_moba)
    sn, conv_p, ssm_p, ps_b, w_out_b = _ssd_prompt(
        xbc.reshape(b, s, conv_ch), dtr.reshape(b, s, LANES), z.reshape(b, s, ssm_w), *ssd_par,
        row(ssm_out_g), expand, w_out, cache_k, page0=ps_ssd0, pages_per_step=pg_ssd, **ssd_kw)
    h_p = _outproj(attn.reshape(b * s, attn_w), sn.reshape(b * s, ssm_w), xp2, row(attn_out_g), w_out_b,
                   tm=_row_tile(b * s, 512))
    y_p = _ffn(h_p, row(norm_ffn_g), w_gate_b, w_up_b, w_down_b,
               tm=_row_tile(b * s, 1024)).reshape(b, s, d)
    page_shape = (b, s // page, page, n_heads, dh)
    k_p, v_p = k.reshape(page_shape), v.reshape(page_shape)

    ms = db * t_len
    xs2 = x_sample.reshape(ms, d)
    qs, k_s, v_s, zs, xbcs, dtrs = _inproj(xs2, *in_args, tm=_row_tile(ms, 1024), **in_kw)
    qs3, ks3, vs3 = (a.reshape(db, t_len, attn_w) for a in (qs, k_s, v_s))
    sel = _sample_select(page_table, ps_a, ps_b, qs3, n_heads=n_heads, page=page, split=ps_ssd0)
    idx = sel[:, :, :t_len, :MOBA_TOP_K].reshape(db, n_heads * t_len * MOBA_TOP_K)
    ok = sel[:, :, :t_len, MOBA_TOP_K:2 * MOBA_TOP_K].reshape(db, n_heads * t_len * MOBA_TOP_K)
    attn_s = _sample_attend(page_table, idx, ok, qs3, ks3, vs3, cache_k, cache_v, n_heads=n_heads)
    sns, conv_s, ssm_s = _ssd_sample(xbcs.reshape(db, t_len, conv_ch), state_conv,
                                     dtrs.reshape(db, t_len, LANES), zs.reshape(db, t_len, ssm_w), state_ssm,
                                     *ssd_par, row(ssm_out_g), expand, **ssd_kw)
    tms = _row_tile(ms, 512)
    h_s = _outproj(attn_s.reshape(ms, attn_w), sns.reshape(ms, ssm_w), xs2, row(attn_out_g), w_out_b, tm=tms)
    y_s = _ffn(h_s, row(norm_ffn_g), w_gate_b, w_up_b, w_down_b, tm=tms).reshape(db, t_len, d)
    tok_shape = (db, t_len, n_heads, dh)
    return (y_p, y_s, k_p, v_p, conv_p, ssm_p, k_s.reshape(tok_shape), v_s.reshape(tok_shape),
            conv_s, ssm_s)
```
